```python
import math
import jax, jax.numpy as jnp
from jax import lax
import numpy as np

D_MODEL = 1024
BATCH = 2
SEQ = 8192
DEPTH = 2
DEC_BATCH = 16
DEC_SEQ = 4096
PAST_LEN = 128

N_EVEN = (DEPTH + 1) // 2
N_ODD = DEPTH // 2
BLOCK = 128
ROPE_THETA = 10000.0
EPS = 1e-6
HEAD_DIM = 64

CONV_CH = D_MODEL // 2
CONV_WIDTH = 31
SWA_HEADS = (D_MODEL // 2) // HEAD_DIM
SWA_KV_HEADS = 2
WINDOW = 128
DIFF_HEADS = (D_MODEL // 2) // (2 * HEAD_DIM)
DIFF_WIDTH = DIFF_HEADS * 2 * HEAD_DIM
SSM_INNER = D_MODEL // 2
SSM_HEAD_DIM = 64
SSM_HEADS = SSM_INNER // SSM_HEAD_DIM
SSM_GROUPS = 2
SSM_STATE = 128
SSM_CONV = 5
SSM_CHUNK = 128
SSM_CONV_DIM = SSM_INNER + 2 * SSM_GROUPS * SSM_STATE

EVEN_SPLITS = (2 * CONV_CH, CONV_CH, SWA_HEADS * HEAD_DIM, SWA_KV_HEADS * HEAD_DIM,
               SWA_KV_HEADS * HEAD_DIM, SWA_HEADS * HEAD_DIM)
ODD_SPLITS = (DIFF_WIDTH, DIFF_WIDTH, DIFF_WIDTH, DIFF_WIDTH,
              SSM_INNER, SSM_CONV_DIM, SSM_HEADS, SSM_HEADS)
IN_EVEN = sum(EVEN_SPLITS)
IN_ODD = sum(ODD_SPLITS)
OUT_EVEN = CONV_CH + SWA_HEADS * HEAD_DIM
OUT_ODD = DIFF_WIDTH + SSM_INNER

kernel_name = "hybrid_conv_swa_diffattn_ssd_encoder"

F32 = jnp.float32


def split_cols(u, sizes):
    idx = [int(c) for c in np.cumsum(sizes)[:-1]]
    return jnp.split(u, idx, axis=-1)


def rmsnorm(x, g):
    xf = x.astype(F32)
    y = xf * lax.rsqrt(jnp.mean(xf * xf, axis=-1, keepdims=True) + EPS)
    return (y * g.astype(F32)).astype(x.dtype)


def layernorm(x, g, b):
    xf = x.astype(F32)
    mu = jnp.mean(xf, axis=-1, keepdims=True)
    var = jnp.mean(jnp.square(xf - mu), axis=-1, keepdims=True)
    return ((xf - mu) * lax.rsqrt(var + EPS) * g.astype(F32) + b.astype(F32)).astype(x.dtype)


def rope_tables(seq):
    inv = 1.0 / (ROPE_THETA ** (jnp.arange(0, HEAD_DIM, 2, dtype=F32) / HEAD_DIM))
    pos = jnp.arange(seq, dtype=F32)
    f = pos[:, None] * inv[None, :]
    emb = jnp.concatenate([f, f], axis=-1)
    return jnp.cos(emb)[:, None, :], jnp.sin(emb)[:, None, :]


def apply_rope(x, cos, sin):
    xf = x.astype(F32)
    x1, x2 = jnp.split(xf, 2, axis=-1)
    rot = jnp.concatenate([-x2, x1], axis=-1)
    return (xf * cos + rot * sin).astype(x.dtype)


def depthwise_conv(x, w, b):
    width, ch = w.shape
    out = lax.conv_general_dilated(
        x, w[:, None, :].astype(x.dtype), window_strides=(1,),
        padding=[((width - 1) // 2, width // 2)],
        dimension_numbers=('NWC', 'WIO', 'NWC'), feature_group_count=ch)
    return out + b.astype(x.dtype)


def windowed_gqa(q, k, v, sink):
    b, s, h, dh = q.shape
    hkv = k.shape[2]
    g = h // hkv
    nb = s // BLOCK
    qb = q.reshape(b, nb, BLOCK, hkv, g, dh)

    def key_windows(t):
        tp = jnp.pad(t, ((0, 0), (BLOCK, BLOCK), (0, 0), (0, 0))).reshape(b, nb + 2, BLOCK, hkv, dh)
        return jnp.concatenate([tp[:, :-2], tp[:, 1:-1], tp[:, 2:]], axis=2)

    kw, vw = key_windows(k), key_windows(v)
    scores = jnp.einsum('bnqkgd,bnckd->bnkgqc', qb, kw).astype(F32) * (dh ** -0.5)
    qi = jnp.arange(BLOCK)[:, None]
    ci = jnp.arange(3 * BLOCK)[None, :]
    band = jnp.abs(ci - BLOCK - qi) <= WINDOW
    kpos = jnp.arange(nb)[:, None] * BLOCK - BLOCK + jnp.arange(3 * BLOCK)[None, :]
    valid = (kpos >= 0) & (kpos < s)
    mask = band[None, :, :] & valid[:, None, :]
    scores = jnp.where(mask[None, :, None, None], scores, -jnp.inf)
    sink_logit = jnp.broadcast_to(sink.astype(F32).reshape(1, 1, hkv, g, 1, 1), scores.shape[:-1] + (1,))
    probs = jax.nn.softmax(jnp.concatenate([scores, sink_logit], axis=-1), axis=-1)[..., :-1]
    out = jnp.einsum('bnkgqc,bnckd->bnqkgd', probs.astype(v.dtype), vw)
    return out.reshape(b, s, h * dh)


def diff_attention(q, k, v, lam, norm_g, lam_init):
    b, s, h, _, d = q.shape
    nb = s // BLOCK
    qb = jnp.moveaxis(q.reshape(b, nb, BLOCK, h, 2, d), 1, 0)
    scale = d ** -0.5

    def block(qblk):
        sc = jnp.einsum('bqhtd,bkhtd->bhtqk', qblk, k).astype(F32) * scale
        p = jax.nn.softmax(sc, axis=-1)
        w = p[:, :, 0] - lam * p[:, :, 1]
        return jnp.einsum('bhqk,bkhe->bqhe', w.astype(v.dtype), v)

    o = jnp.moveaxis(lax.map(block, qb), 0, 1).reshape(b, s, h, 2 * d)
    of = o.astype(F32)
    of = of * lax.rsqrt(jnp.mean(of * of, axis=-1, keepdims=True) + EPS) * norm_g.astype(F32) * (1.0 - lam_init)
    return of.astype(v.dtype).reshape(b, s, h * 2 * d)


def ssd_scan(x, dt, a_coef, bm, cm):
    b, s, h, p = x.shape
    g, n = bm.shape[2], bm.shape[3]
    hg = h // g
    nc = s // SSM_CHUNK
    L = SSM_CHUNK
    xdt = (x.astype(F32) * dt[..., None]).reshape(b, nc, L, g, hg, p)
    a = (dt * a_coef).reshape(b, nc, L, g, hg)
    bc = bm.astype(F32).reshape(b, nc, L, g, n)
    cc = cm.astype(F32).reshape(b, nc, L, g, n)
    a_cum = jnp.cumsum(a, axis=2)
    seg = a_cum[:, :, :, None] - a_cum[:, :, None, :]
    causal = (jnp.arange(L)[:, None] >= jnp.arange(L)[None, :])[None, None, :, :, None, None]
    decay = jnp.where(causal, jnp.exp(jnp.where(causal, seg, 0.0)), 0.0)
    cb = jnp.einsum('bclgn,bcsgn->bclsg', cc, bc)
    y_diag = jnp.einsum('bclsg,bclsgh,bcsghp->bclghp', cb, decay, xdt)
    decay_to_end = jnp.exp(a_cum[:, :, -1:] - a_cum)
    states = jnp.einsum('bclgn,bclgh,bclghp->bcghpn', bc, decay_to_end, xdt)
    chunk_decay = jnp.exp(a_cum[:, :, -1])

    def step(carry, inp):
        st, dec = inp
        return carry * dec[..., None, None] + st, carry

    init = jnp.zeros((b, g, hg, p, n), F32)
    _, prev = lax.scan(step, init, (jnp.moveaxis(states, 1, 0), jnp.moveaxis(chunk_decay, 1, 0)))
    prev = jnp.moveaxis(prev, 0, 1)
    y_off = jnp.einsum('bclgn,bcghpn,bclgh->bclghp', cc, prev, jnp.exp(a_cum))
    return (y_diag + y_off).reshape(b, s, h, p)


def mamba2_bidir(z, xbc, dt_f, dt_b, conv_w, conv_b, dtb_f, dtb_b, alog_f, alog_b, d_skip, norm_g):
    b, s, _ = z.shape
    xbc = jax.nn.silu(depthwise_conv(xbc, conv_w, conv_b))
    xs, bm, cm = split_cols(xbc, (SSM_INNER, SSM_GROUPS * SSM_STATE, SSM_GROUPS * SSM_STATE))
    xs = xs.reshape(b, s, SSM_HEADS, SSM_HEAD_DIM)
    bm = bm.reshape(b, s, SSM_GROUPS, SSM_STATE)
    cm = cm.reshape(b, s, SSM_GROUPS, SSM_STATE)
    dtf = jax.nn.softplus(dt_f.astype(F32) + dtb_f.astype(F32))
    dtbk = jax.nn.softplus(dt_b.astype(F32) + dtb_b.astype(F32))
    a_f = -jnp.exp(alog_f.astype(F32))
    a_b = -jnp.exp(alog_b.astype(F32))
    flip = lambda t: jnp.flip(t, axis=1)
    y_f = ssd_scan(xs, dtf, a_f, bm, cm)
    y_b = flip(ssd_scan(flip(xs), flip(dtbk), a_b, flip(bm), flip(cm)))
    y = y_f + y_b + xs.astype(F32) * d_skip.astype(F32)[:, None]
    y = y.reshape(b, s, SSM_INNER) * jax.nn.silu(z.astype(F32))
    return rmsnorm(y, norm_g).astype(z.dtype)


def even_layer(h, cos, sin, w_in, conv_w, conv_b, ln_g, ln_b, sink, w_out):
    b, s, _ = h.shape
    u = h @ w_in
    glu, a_gate, q, k, v, b_gate = split_cols(u, EVEN_SPLITS)
    a_val, a_glu = jnp.split(glu, 2, axis=-1)
    a = a_val * jax.nn.sigmoid(a_glu)
    a = layernorm(depthwise_conv(a, conv_w, conv_b), ln_g, ln_b)
    a = jax.nn.silu(a) * jax.nn.silu(a_gate)
    q = apply_rope(q.reshape(b, s, SWA_HEADS, HEAD_DIM), cos, sin)
    k = apply_rope(k.reshape(b, s, SWA_KV_HEADS, HEAD_DIM), cos, sin)
    v = v.reshape(b, s, SWA_KV_HEADS, HEAD_DIM)
    o = windowed_gqa(q, k, v, sink) * jax.nn.silu(b_gate)
    return jnp.concatenate([a, o], axis=-1) @ w_out


def odd_layer(h, cos, sin, lam_init, w_in, lq1, lk1, lq2, lk2, diff_g, sconv_w, sconv_b,
              dtb_f, dtb_b, alog_f, alog_b, d_skip, ssm_g, w_out):
    b, s, _ = h.shape
    u = h @ w_in
    q, k, v, c_gate, z, xbc, dt_f, dt_b = split_cols(u, ODD_SPLITS)
    q = apply_rope(q.reshape(b, s, 2 * DIFF_HEADS, HEAD_DIM), cos, sin).reshape(b, s, DIFF_HEADS, 2, HEAD_DIM)
    k = apply_rope(k.reshape(b, s, 2 * DIFF_HEADS, HEAD_DIM), cos, sin).reshape(b, s, DIFF_HEADS, 2, HEAD_DIM)
    v = v.reshape(b, s, DIFF_HEADS, 2 * HEAD_DIM)
    lam = (jnp.exp(jnp.sum(lq1.astype(F32) * lk1.astype(F32)))
           - jnp.exp(jnp.sum(lq2.astype(F32) * lk2.astype(F32))) + lam_init)
    c = diff_attention(q, k, v, lam, diff_g, lam_init) * jax.nn.silu(c_gate)
    d = mamba2_bidir(z, xbc, dt_f, dt_b, sconv_w, sconv_b, dtb_f, dtb_b, alog_f, alog_b, d_skip, ssm_g)
    return jnp.concatenate([c, d.astype(c.dtype)], axis=-1) @ w_out


def trunk(x, norm_g, w_in0, conv_w, conv_b, conv_ln_g, conv_ln_b, sink, w_out0,
          w_in1, lambda_q1, lambda_k1, lambda_q2, lambda_k2, diff_norm_g, ssm_conv_w, ssm_conv_b,
          dt_bias_f, dt_bias_b, a_log_f, a_log_b, d_skip, ssm_norm_g, w_out1, final_norm_g):
    cos, sin = rope_tables(x.shape[1])
    for layer in range(DEPTH):
        i = layer // 2
        h = rmsnorm(x, norm_g[layer])
        if layer % 2 == 0:
            x = x + even_layer(h, cos, sin, w_in0[i], conv_w[i], conv_b[i], conv_ln_g[i], conv_ln_b[i],
                               sink[i], w_out0[i])
        else:
            lam_init = 0.8 - 0.6 * math.exp(-0.3 * layer)
            x = x + odd_layer(h, cos, sin, lam_init, w_in1[i], lambda_q1[i], lambda_k1[i], lambda_q2[i],
                              lambda_k2[i], diff_norm_g[i], ssm_conv_w[i], ssm_conv_b[i], dt_bias_f[i],
                              dt_bias_b[i], a_log_f[i], a_log_b[i], d_skip[i], ssm_norm_g[i], w_out1[i])
    return rmsnorm(x, final_norm_g)


def setup_inputs(seed: int = 0) -> dict:
    key = jax.random.key(seed)
    ks = jax.random.split(key, 32)
    nrm = jax.random.normal

    def dt_bias(k):
        u = jax.random.uniform(k, (N_ODD, SSM_HEADS), F32)
        dt = jnp.exp(u * (math.log(0.1) - math.log(0.001)) + math.log(0.001))
        return dt + jnp.log(-jnp.expm1(-dt))

    return {
        "x_prompt": nrm(ks[0], (BATCH, SEQ, D_MODEL), F32),
        "x_sample": nrm(ks[1], (DEC_BATCH, DEC_SEQ, D_MODEL), F32),
        "norm_g": 1.0 + 0.02 * nrm(ks[2], (DEPTH, D_MODEL), F32),
        "w_in0": nrm(ks[3], (N_EVEN, D_MODEL, IN_EVEN), F32) * D_MODEL ** -0.5,
        "conv_w": nrm(ks[4], (N_EVEN, CONV_WIDTH, CONV_CH), F32) * CONV_WIDTH ** -0.5,
        "conv_b": 0.02 * nrm(ks[5], (N_EVEN, CONV_CH), F32),
        "conv_ln_g": 1.0 + 0.02 * nrm(ks[6], (N_EVEN, CONV_CH), F32),
        "conv_ln_b": 0.02 * nrm(ks[7], (N_EVEN, CONV_CH), F32),
        "sink": 0.5 * nrm(ks[8], (N_EVEN, SWA_HEADS), F32),
        "w_out0": nrm(ks[9], (N_EVEN, OUT_EVEN, D_MODEL), F32) * OUT_EVEN ** -0.5,
        "w_in1": nrm(ks[10], (N_ODD, D_MODEL, IN_ODD), F32) * D_MODEL ** -0.5,
        "lambda_q1": 0.1 * nrm(ks[11], (N_ODD, HEAD_DIM), F32),
        "lambda_k1": 0.1 * nrm(ks[12], (N_ODD, HEAD_DIM), F32),
        "lambda_q2": 0.1 * nrm(ks[13], (N_ODD, HEAD_DIM), F32),
        "lambda_k2": 0.1 * nrm(ks[14], (N_ODD, HEAD_DIM), F32),
        "diff_norm_g": 1.0 + 0.02 * nrm(ks[15], (N_ODD, 2 * HEAD_DIM), F32),
        "ssm_conv_w": nrm(ks[16], (N_ODD, SSM_CONV, SSM_CONV_DIM), F32) * SSM_CONV ** -0.5,
        "ssm_conv_b": 0.02 * nrm(ks[17], (N_ODD, SSM_CONV_DIM), F32),
        "dt_bias_f": dt_bias(ks[18]),
        "dt_bias_b": dt_bias(ks[19]),
        "a_log_f": jnp.log(jax.random.uniform(ks[20], (N_ODD, SSM_HEADS), F32, 1.0, 16.0)),
        "a_log_b": jnp.log(jax.random.uniform(ks[21], (N_ODD, SSM_HEADS), F32, 1.0, 16.0)),
        "d_skip": 1.0 + 0.02 * nrm(ks[22], (N_ODD, SSM_HEADS), F32),
        "ssm_norm_g": 1.0 + 0.02 * nrm(ks[23], (N_ODD, SSM_INNER), F32),
        "w_out1": nrm(ks[24], (N_ODD, OUT_ODD, D_MODEL), F32) * OUT_ODD ** -0.5,
        "final_norm_g": 1.0 + 0.02 * nrm(ks[25], (D_MODEL,), F32),
    }


def reference(x_prompt, x_sample, norm_g, w_in0, conv_w, conv_b, conv_ln_g, conv_ln_b, sink, w_out0,
              w_in1, lambda_q1, lambda_k1, lambda_q2, lambda_k2, diff_norm_g, ssm_conv_w, ssm_conv_b,
              dt_bias_f, dt_bias_b, a_log_f, a_log_b, d_skip, ssm_norm_g, w_out1, final_norm_g):
    y_prompt = trunk(x_prompt, norm_g, w_in0, conv_w, conv_b, conv_ln_g, conv_ln_b, sink, w_out0,
                     w_in1, lambda_q1, lambda_k1, lambda_q2, lambda_k2, diff_norm_g, ssm_conv_w, ssm_conv_b,
                     dt_bias_f, dt_bias_b, a_log_f, a_log_b, d_skip, ssm_norm_g, w_out1, final_norm_g)
    y_sample = trunk(x_sample, norm_g, w_in0, conv_w, conv_b, conv_ln_g, conv_ln_b, sink, w_out0,
                     w_in1, lambda_q1, lambda_k1, lambda_q2, lambda_k2, diff_norm_g, ssm_conv_w, ssm_conv_b,
                     dt_bias_f, dt_bias_b, a_log_f, a_log_b, d_skip, ssm_norm_g, w_out1, final_norm_g)
    return (y_prompt, y_sample)
```

```python
import functools
import math

import jax
import jax.numpy as jnp
from jax import lax
from jax.experimental import pallas as pl
from jax.experimental.pallas import tpu as pltpu

F32 = jnp.float32
BF16 = jnp.bfloat16

D_MODEL = 1024
EPS = 1e-6
ROPE_THETA = 10000.0
HEAD_DIM = 64
LANES = 128
HALO = 16
CONV_CH = 512
CONV_WIDTH = 31
SWA_HEADS = 8
BLOCK = 128
DIFF_HEADS = 4
SSM_INNER = 512
SSM_HEADS = 8
SSM_STATE = 128
SSM_CONV = 5
SSM_CONV_DIM = 1024
IN_EVEN = 2816
IN_ODD = 3600
IN_ODD_PAD = 3712
NEG_BIG = -1e30
VMEM_LIMIT = 56 * 1024 * 1024


def _cparams(sem):
    return pltpu.CompilerParams(dimension_semantics=sem, vmem_limit_bytes=VMEM_LIMIT)


def _sigmoid(x):
    return 1.0 / (1.0 + jnp.exp(-x))


def _silu(x):
    return x * _sigmoid(x)


def _rmsnorm(x, g):
    return x * lax.rsqrt(jnp.mean(x * x, axis=-1, keepdims=True) + EPS) * g


def _rope_cols(x, cos, sa, sb):
    outs = []
    for j in range(x.shape[1] // LANES):
        xj = x[:, j * LANES:(j + 1) * LANES]
        outs.append(xj * cos + pltpu.roll(xj, LANES - 32, 1) * sa + pltpu.roll(xj, 32, 1) * sb)
    return outs[0] if len(outs) == 1 else jnp.concatenate(outs, axis=1)


def _inproj0_kernel(x_ref, g_ref, w_ref, cos_ref, sa_ref, sb_ref,
                    apre_ref, ga_ref, q_ref, k_ref, v_ref, gb_ref):
    hb = _rmsnorm(x_ref[...], g_ref[...]).astype(BF16)

    def mm(a, b):
        return jnp.dot(hb, w_ref[:, a:b], preferred_element_type=F32)

    cos, sa, sb = cos_ref[...], sa_ref[...], sb_ref[...]
    apre_ref[...] = (mm(0, 512) * _sigmoid(mm(512, 1024))).astype(BF16)
    ga_ref[...] = _silu(mm(1024, 1536)).astype(BF16)
    q_ref[...] = (_rope_cols(mm(1536, 2048), cos, sa, sb) * (HEAD_DIM ** -0.5)).astype(BF16)
    k_ref[...] = _rope_cols(mm(2048, 2176), cos, sa, sb).astype(BF16)
    v_ref[...] = mm(2176, 2304).astype(BF16)
    gb_ref[...] = _silu(mm(2304, 2816)).astype(BF16)


def _inproj0(x2, g, w, cos, sa, sb, seq, tm):
    t = x2.shape[0]
    ns = seq // tm
    row = lambda i: (i, 0)
    pos = lambda i: (i % ns, 0)
    full = lambda i: (0, 0)
    widths = (512, 512, 512, 128, 128, 512)
    return pl.pallas_call(
        _inproj0_kernel,
        grid=(t // tm,),
        in_specs=[pl.BlockSpec((tm, D_MODEL), row), pl.BlockSpec((1, D_MODEL), full),
                  pl.BlockSpec((D_MODEL, IN_EVEN), full),
                  pl.BlockSpec((tm, LANES), pos), pl.BlockSpec((tm, LANES), pos), pl.BlockSpec((tm, LANES), pos)],
        out_specs=[pl.BlockSpec((tm, wd), row) for wd in widths],
        out_shape=[jax.ShapeDtypeStruct((t, wd), BF16) for wd in widths],
        compiler_params=_cparams(("parallel",)),
        name="inproj0",
    )(x2, g, w, cos, sa, sb)


def _fill_halo_pad(pad_ref, xc, xp, xn, ts):
    i = pl.program_id(1)
    last = pl.num_programs(1) - 1
    pad_ref[0:HALO, :] = jnp.where(i > 0, xp.astype(F32), 0.0)
    pad_ref[HALO:HALO + ts, :] = xc.astype(F32)
    pad_ref[HALO + ts:2 * HALO + ts, :] = jnp.where(i < last, xn.astype(F32), 0.0)


def _convmod_kernel(xc_ref, xp_ref, xn_ref, ga_ref, cw_ref, cb_ref, lg_ref, lb_ref, o_ref, pad_ref, *, ts, rs):
    _fill_halo_pad(pad_ref, xc_ref[0], xp_ref[0], xn_ref[0], ts)
    first = HALO - (CONV_WIDTH - 1) // 2
    for r0 in range(0, ts, rs):
        acc = jnp.broadcast_to(cb_ref[...], (rs, CONV_CH))
        for w in range(CONV_WIDTH):
            acc = acc + pad_ref[r0 + first + w:r0 + first + w + rs, :] * cw_ref[w:w + 1, :]
        mu = jnp.mean(acc, axis=-1, keepdims=True)
        xc = acc - mu
        var = jnp.mean(xc * xc, axis=-1, keepdims=True)
        y = xc * lax.rsqrt(var + EPS) * lg_ref[...] + lb_ref[...]
        o_ref[0, r0:r0 + rs, :] = (_silu(y) * ga_ref[0, r0:r0 + rs, :].astype(F32)).astype(BF16)


def _halo_specs(ts, seq, width):
    hb = ts // HALO
    nh = seq // HALO
    return [pl.BlockSpec((1, ts, width), lambda b, i: (b, i, 0)),
            pl.BlockSpec((1, HALO, width), lambda b, i: (b, jnp.maximum(i * hb - 1, 0), 0)),
            pl.BlockSpec((1, HALO, width), lambda b, i: (b, jnp.minimum((i + 1) * hb, nh - 1), 0))]


def _convmod(apre, ga, cw, cb, lg, lb, ts=256, rs=32):
    bsz, seq, _ = apre.shape
    vec = pl.BlockSpec((1, CONV_CH), lambda b, i: (0, 0))
    return pl.pallas_call(
        functools.partial(_convmod_kernel, ts=ts, rs=rs),
        grid=(bsz, seq // ts),
        in_specs=_halo_specs(ts, seq, CONV_CH) + [
            pl.BlockSpec((1, ts, CONV_CH), lambda b, i: (b, i, 0)),
            pl.BlockSpec((CONV_WIDTH, CONV_CH), lambda b, i: (0, 0)), vec, vec, vec],
        out_specs=pl.BlockSpec((1, ts, CONV_CH), lambda b, i: (b, i, 0)),
        out_shape=jax.ShapeDtypeStruct((bsz, seq, CONV_CH), BF16),
        scratch_shapes=[pltpu.VMEM((ts + 2 * HALO, CONV_CH), F32)],
        compiler_params=_cparams(("parallel", "parallel")),
        name="convmod",
    )(apre, apre, apre, ga, cw, cb, lg, lb)


def _swa_kernel(sink_ref, q_ref, kp_ref, kc_ref, kn_ref, vp_ref, vc_ref, vn_ref, gb_ref, o_ref):
    n = pl.program_id(1)
    last = pl.num_programs(1) - 1
    kcat = jnp.concatenate([kp_ref[0], kc_ref[0], kn_ref[0]], axis=0)
    vcat = jnp.concatenate([vp_ref[0], vc_ref[0], vn_ref[0]], axis=0)
    rows = lax.broadcasted_iota(jnp.int32, (2 * BLOCK, 3 * BLOCK), 0)
    qi = jnp.where(rows >= BLOCK, rows - BLOCK, rows)
    ci = lax.broadcasted_iota(jnp.int32, (2 * BLOCK, 3 * BLOCK), 1)
    rel = ci - BLOCK - qi
    first_key = jnp.where(n > 0, 0, BLOCK)
    end_key = jnp.where(n < last, 3 * BLOCK, 2 * BLOCK)
    mask = (rel >= -BLOCK) & (rel <= BLOCK) & (ci >= first_key) & (ci < end_key)
    lane = lax.broadcasted_iota(jnp.int32, (BLOCK, LANES), 1)
    low = lane < HEAD_DIM
    row1 = lax.broadcasted_iota(jnp.int32, (2 * BLOCK, 1), 0)
    for j in range(SWA_HEADS // 2):
        qv = q_ref[0, :, j * LANES:(j + 1) * LANES]
        zero = jnp.zeros_like(qv)
        qq = jnp.concatenate([jnp.where(low, qv, zero), jnp.where(low, zero, qv)], axis=0)
        s = lax.dot_general(qq, kcat, (((1,), (1,)), ((), ())), preferred_element_type=F32)
        s = jnp.where(mask, s, NEG_BIG)
        sink = jnp.where(row1 < BLOCK, sink_ref[j], sink_ref[j + SWA_HEADS // 2])
        m = jnp.maximum(jnp.max(s, axis=-1, keepdims=True), sink)
        p = jnp.exp(s - m)
        denom = jnp.sum(p, axis=-1, keepdims=True) + jnp.exp(sink - m)
        pv = jnp.dot(p.astype(BF16), vcat, preferred_element_type=F32) / denom
        o = jnp.where(low, pv[:BLOCK], pv[BLOCK:])
        o_ref[0, :, j * LANES:(j + 1) * LANES] = (
            o * gb_ref[0, :, j * LANES:(j + 1) * LANES].astype(F32)).astype(BF16)


def _swa(sink, q, k, v, gb):
    bsz, seq, _ = q.shape
    nb = seq // BLOCK
    cur = lambda b, n: (b, n, 0)
    prv = lambda b, n: (b, jnp.maximum(n - 1, 0), 0)
    nxt = lambda b, n: (b, jnp.minimum(n + 1, nb - 1), 0)
    kv = lambda im: pl.BlockSpec((1, BLOCK, LANES), im)
    wide = pl.BlockSpec((1, BLOCK, 512), cur)
    return pl.pallas_call(
        _swa_kernel,
        grid=(bsz, nb),
        in_specs=[pl.BlockSpec(memory_space=pltpu.SMEM), wide,
                  kv(prv), kv(cur), kv(nxt), kv(prv), kv(cur), kv(nxt), wide],
        out_specs=wide,
        out_shape=jax.ShapeDtypeStruct((bsz, seq, 512), BF16),
        compiler_params=_cparams(("parallel", "parallel")),
        name="swa",
    )(sink, q, k, k, k, v, v, v, gb)


def _mid_kernel(x_ref, a_ref, o_ref, wo_ref, g_ref, w_ref, cos_ref, sa_ref, sb_ref,
                x1_ref, q_ref, k_ref, v_ref, cg_ref, zg_ref, xbc_ref, dt_ref):
    x1 = (x_ref[...] + jnp.dot(a_ref[...], wo_ref[0:512, :], preferred_element_type=F32)
          + jnp.dot(o_ref[...], wo_ref[512:1024, :], preferred_element_type=F32))
    x1_ref[...] = x1
    hb = _rmsnorm(x1, g_ref[...]).astype(BF16)

    def mm(a, b):
        return jnp.dot(hb, w_ref[:, a:b], preferred_element_type=F32)

    cos, sa, sb = cos_ref[...], sa_ref[...], sb_ref[...]
    q_ref[...] = (_rope_cols(mm(0, 512), cos, sa, sb) * (HEAD_DIM ** -0.5)).astype(BF16)
    k_ref[...] = _rope_cols(mm(512, 1024), cos, sa, sb).astype(BF16)
    v = mm(1024, 1536).astype(BF16)
    ones = jnp.ones((v.shape[0], LANES), BF16)
    for h in range(DIFF_HEADS):
        v_ref[:, 2 * h * LANES:(2 * h + 1) * LANES] = v[:, h * LANES:(h + 1) * LANES]
        v_ref[:, (2 * h + 1) * LANES:(2 * h + 2) * LANES] = ones
    cg_ref[...] = _silu(mm(1536, 2048)).astype(BF16)
    zg_ref[...] = _silu(mm(2048, 2560)).astype(BF16)
    xbc_ref[...] = mm(2560, 3584).astype(BF16)
    dt_ref[...] = mm(3584, IN_ODD_PAD)


def _mid(x2, a2, o2, wo, g, w, cos, sa, sb, seq, tm):
    t = x2.shape[0]
    ns = seq // tm
    row = lambda i: (i, 0)
    pos = lambda i: (i % ns, 0)
    full = lambda i: (0, 0)
    outs = ((D_MODEL, F32), (512, BF16), (512, BF16), (1024, BF16), (512, BF16), (512, BF16),
            (SSM_CONV_DIM, BF16), (LANES, F32))
    return pl.pallas_call(
        _mid_kernel,
        grid=(t // tm,),
        in_specs=[pl.BlockSpec((tm, D_MODEL), row), pl.BlockSpec((tm, 512), row), pl.BlockSpec((tm, 512), row),
                  pl.BlockSpec((D_MODEL, D_MODEL), full), pl.BlockSpec((1, D_MODEL), full),
                  pl.BlockSpec((D_MODEL, IN_ODD_PAD), full),
                  pl.BlockSpec((tm, LANES), pos), pl.BlockSpec((tm, LANES), pos), pl.BlockSpec((tm, LANES), pos)],
        out_specs=[pl.BlockSpec((tm, wd), row) for wd, _ in outs],
        out_shape=[jax.ShapeDtypeStruct((t, wd), dt) for wd, dt in outs],
        compiler_params=_cparams(("parallel",)),
        name="outproj0_inproj1",
    )(x2, a2, o2, wo, g, w, cos, sa, sb)


def _diff_kernel(lq1_ref, lk1_ref, lq2_ref, lk2_ref, g_ref, q_ref, k_ref, v_ref, cg_ref, o_ref,
                 m_ref, acc_ref, *, kc, lam_init):
    tq = q_ref.shape[1]
    seq = k_ref.shape[1]
    q = q_ref[0]
    lane = lax.broadcasted_iota(jnp.int32, (tq, LANES), 1)
    zero = jnp.zeros_like(q)
    qs = (jnp.where(lane < HEAD_DIM, q, zero), jnp.where(lane < HEAD_DIM, zero, q))
    m_ref[...] = jnp.full(m_ref.shape, NEG_BIG, F32)
    acc_ref[...] = jnp.zeros(acc_ref.shape, F32)

    def body(c, carry):
        start = pl.multiple_of(c * kc, kc)
        kch = k_ref[0, pl.ds(start, kc), :]
        vch = v_ref[0, pl.ds(start, kc), :]
        for t in range(2):
            s = lax.dot_general(qs[t], kch, (((1,), (1,)), ((), ())), preferred_element_type=F32)
            m_old = m_ref[t]
            m_new = jnp.maximum(m_old, jnp.max(s, axis=-1, keepdims=True))
            p = jnp.exp(s - m_new[:, 0:1])
            m_ref[t] = m_new
            acc_ref[t] = acc_ref[t] * jnp.exp(m_old - m_new)[:, 0:1] + jnp.dot(
                p.astype(BF16), vch, preferred_element_type=F32)
        return carry

    lax.fori_loop(0, seq // kc, body, 0)

    lam = (jnp.exp(jnp.sum(lq1_ref[...] * lk1_ref[...], axis=-1, keepdims=True))
           - jnp.exp(jnp.sum(lq2_ref[...] * lk2_ref[...], axis=-1, keepdims=True)) + lam_init)
    o0 = acc_ref[0, :, 0:LANES] / acc_ref[0, :, LANES:LANES + 1]
    o1 = acc_ref[1, :, 0:LANES] / acc_ref[1, :, LANES:LANES + 1]
    o = o0 - lam * o1
    o = o * lax.rsqrt(jnp.mean(o * o, axis=-1, keepdims=True) + EPS) * g_ref[...] * (1.0 - lam_init)
    o_ref[0] = (o * cg_ref[0].astype(F32)).astype(BF16)


def _diff_attn(lq1, lk1, lq2, lk2, g, q, k, v, cg, lam_init, tq=256, kc=512):
    bsz, seq, _ = q.shape
    small = pl.BlockSpec((1, HEAD_DIM), lambda b, h, i: (0, 0))
    tile = pl.BlockSpec((1, tq, LANES), lambda b, h, i: (b, i, h))
    return pl.pallas_call(
        functools.partial(_diff_kernel, kc=kc, lam_init=lam_init),
        grid=(bsz, DIFF_HEADS, seq // tq),
        in_specs=[small, small, small, small, pl.BlockSpec((1, LANES), lambda b, h, i: (0, 0)), tile,
                  pl.BlockSpec((1, seq, LANES), lambda b, h, i: (b, 0, h)),
                  pl.BlockSpec((1, seq, 2 * LANES), lambda b, h, i: (b, 0, h)), tile],
        out_specs=tile,
        out_shape=jax.ShapeDtypeStruct((bsz, seq, 512), BF16),
        scratch_shapes=[pltpu.VMEM((2, tq, LANES), F32), pltpu.VMEM((2, tq, 2 * LANES), F32)],
        compiler_params=_cparams(("parallel", "parallel", "parallel")),
        name="diff_attn",
    )(lq1, lk1, lq2, lk2, g, q, k, v, cg)


def _split3(x):
    hi = x.astype(BF16)
    r = x - hi.astype(F32)
    mid = r.astype(BF16)
    lo = (r - mid.astype(F32)).astype(BF16)
    return hi, mid, lo


def _ssd_kernel(xc_ref, xp_ref, xn_ref, dt_ref, cw_ref, cb_ref, dtb_ref, a_ref, dsk_ref, y_ref,
                pad_ref, carry_ref, *, reverse, col0, add_skip):
    L = BLOCK
    c = pl.program_id(1)

    @pl.when(c == 0)
    def _():
        carry_ref[...] = jnp.zeros(carry_ref.shape, F32)

    if reverse:
        i = pl.num_programs(1) - 1 - c
    else:
        i = c
    last = pl.num_programs(1) - 1
    pad_ref[0:HALO, :] = jnp.where(i > 0, xp_ref[0].astype(F32), 0.0)
    pad_ref[HALO:HALO + L, :] = xc_ref[0].astype(F32)
    pad_ref[HALO + L:2 * HALO + L, :] = jnp.where(i < last, xn_ref[0].astype(F32), 0.0)
    first = HALO - (SSM_CONV - 1) // 2
    acc = jnp.broadcast_to(cb_ref[...], (L, SSM_CONV_DIM))
    for w in range(SSM_CONV):
        acc = acc + pad_ref[first + w:first + w + L, :] * cw_ref[w:w + 1, :]
    xbc = _silu(acc)
    xs = xbc[:, 0:SSM_INNER]
    bm = [xbc[:, SSM_INNER + g * SSM_STATE:SSM_INNER + (g + 1) * SSM_STATE] for g in range(2)]
    cm = [xbc[:, SSM_INNER + (2 + g) * SSM_STATE:SSM_INNER + (3 + g) * SSM_STATE] for g in range(2)]

    z = dt_ref[0] + dtb_ref[...]
    dt = jnp.maximum(z, 0.0) + jnp.log(1.0 + jnp.exp(-jnp.abs(z)))
    a = dt * a_ref[...]
    r_i = lax.broadcasted_iota(jnp.int32, (L, L), 0)
    c_i = lax.broadcasted_iota(jnp.int32, (L, L), 1)
    causal = (r_i <= c_i) if reverse else (r_i >= c_i)
    tri = jnp.where(causal, 1.0, 0.0).astype(BF16)
    acum = sum(jnp.dot(tri, piece, preferred_element_type=F32) for piece in _split3(a))
    acum_t = acum.T
    edge = 0 if reverse else L - 1
    a_end = acum[edge:edge + 1, :]
    eac = jnp.exp(acum)
    dte = jnp.exp(a_end - acum)
    cdec = jnp.exp(a_end)

    lane = lax.broadcasted_iota(jnp.int32, (L, LANES), 1)
    low = lane < HEAD_DIM
    lane1 = lax.broadcasted_iota(jnp.int32, (1, LANES), 1)

    def pair_cols(mat, j):
        ca = col0 + 2 * j
        return jnp.where(low, mat[:, ca:ca + 1], mat[:, ca + 1:ca + 2])

    cbs = [lax.dot_general(cm[g].astype(BF16), bm[g].astype(BF16), (((1,), (1,)), ((), ())),
                           preferred_element_type=F32) for g in range(2)]
    bts = [bm[g].T.astype(BF16) for g in range(2)]
    cms = [cm[g].astype(BF16) for g in range(2)]
    for j in range(SSM_HEADS // 2):
        g = j // 2
        xs_j = xs[:, j * LANES:(j + 1) * LANES]
        xdt = xs_j * pair_cols(dt, j)
        xdt_b = xdt.astype(BF16)
        zero = jnp.zeros_like(xdt_b)
        ms = []
        for hh in range(2):
            col = col0 + 2 * j + hh
            seg = acum[:, col:col + 1] - acum_t[col:col + 1, :]
            dec = jnp.where(causal, jnp.exp(jnp.where(causal, seg, 0.0)), 0.0)
            ms.append((cbs[g] * dec).astype(BF16))
        lhs = jnp.concatenate(ms, axis=1)
        rhs = jnp.concatenate([jnp.where(low, xdt_b, zero), jnp.where(low, zero, xdt_b)], axis=0)
        y = jnp.dot(lhs, rhs, preferred_element_type=F32)
        prev = carry_ref[j]
        y = y + jnp.dot(cms[g], prev.astype(BF16), preferred_element_type=F32) * pair_cols(eac, j)
        st = jnp.dot(bts[g], (xdt * pair_cols(dte, j)).astype(BF16), preferred_element_type=F32)
        ca = col0 + 2 * j
        cd = jnp.where(lane1 < HEAD_DIM, cdec[:, ca:ca + 1], cdec[:, ca + 1:ca + 2])
        carry_ref[j] = prev * cd + st
        if add_skip:
            y = y + xs_j * dsk_ref[:, j * LANES:(j + 1) * LANES]
        y_ref[0, :, j * LANES:(j + 1) * LANES] = y


def _ssd(xbc, dt, cw, cb, dtb, a, dsk, reverse):
    bsz, seq, _ = xbc.shape
    nc = seq // BLOCK
    hb = BLOCK // HALO
    nh = seq // HALO
    if reverse:
        blk = lambda c: nc - 1 - c
    else:
        blk = lambda c: c
    vec = lambda wd: pl.BlockSpec((1, wd), lambda b, c: (0, 0))
    return pl.pallas_call(
        functools.partial(_ssd_kernel, reverse=reverse, col0=SSM_HEADS if reverse else 0, add_skip=not reverse),
        grid=(bsz, nc),
        in_specs=[pl.BlockSpec((1, BLOCK, SSM_CONV_DIM), lambda b, c: (b, blk(c), 0)),
                  pl.BlockSpec((1, HALO, SSM_CONV_DIM), lambda b, c: (b, jnp.maximum(blk(c) * hb - 1, 0), 0)),
                  pl.BlockSpec((1, HALO, SSM_CONV_DIM), lambda b, c: (b, jnp.minimum((blk(c) + 1) * hb, nh - 1), 0)),
                  pl.BlockSpec((1, BLOCK, LANES), lambda b, c: (b, blk(c), 0)),
                  pl.BlockSpec((SSM_CONV, SSM_CONV_DIM), lambda b, c: (0, 0)), vec(SSM_CONV_DIM),
                  vec(LANES), vec(LANES), vec(SSM_INNER)],
        out_specs=pl.BlockSpec((1, BLOCK, SSM_INNER), lambda b, c: (b, blk(c), 0)),
        out_shape=jax.ShapeDtypeStruct((bsz, seq, SSM_INNER), F32),
        scratch_shapes=[pltpu.VMEM((BLOCK + 2 * HALO, SSM_CONV_DIM), F32),
                        pltpu.VMEM((SSM_HEADS // 2, SSM_STATE, LANES), F32)],
        compiler_params=_cparams(("parallel", "arbitrary")),
        name="ssd_bwd" if reverse else "ssd_fwd",
    )(xbc, xbc, xbc, dt, cw, cb, dtb, a, dsk)


def _final_kernel(x1_ref, c_ref, yf_ref, yb_ref, zg_ref, sg_ref, wo_ref, fg_ref, o_ref):
    y = (yf_ref[...] + yb_ref[...]) * zg_ref[...].astype(F32)
    d = _rmsnorm(y, sg_ref[...]).astype(BF16)
    x2 = (x1_ref[...] + jnp.dot(c_ref[...], wo_ref[0:512, :], preferred_element_type=F32)
          + jnp.dot(d, wo_ref[512:1024, :], preferred_element_type=F32))
    o_ref[...] = _rmsnorm(x2, fg_ref[...])


def _final(x1, c2, yf, yb, zg, sg, wo, fg, tm):
    t = x1.shape[0]
    row = lambda i: (i, 0)
    full = lambda i: (0, 0)
    half = pl.BlockSpec((tm, 512), row)
    return pl.pallas_call(
        _final_kernel,
        grid=(t // tm,),
        in_specs=[pl.BlockSpec((tm, D_MODEL), row), half, half, half, half,
                  pl.BlockSpec((1, 512), full), pl.BlockSpec((D_MODEL, D_MODEL), full),
                  pl.BlockSpec((1, D_MODEL), full)],
        out_specs=pl.BlockSpec((tm, D_MODEL), row),
        out_shape=jax.ShapeDtypeStruct((t, D_MODEL), F32),
        compiler_params=_cparams(("parallel",)),
        name="outproj1_final",
    )(x1, c2, yf, yb, zg, sg, wo, fg)


def _rope_tables(seq):
    inv = 1.0 / (ROPE_THETA ** (jnp.arange(0, HEAD_DIM, 2, dtype=F32) / HEAD_DIM))
    f = jnp.arange(seq, dtype=F32)[:, None] * inv[None, :]
    emb = jnp.concatenate([f, f, f, f], axis=-1)
    cos, sin = jnp.cos(emb), jnp.sin(emb)
    first_half = (jnp.arange(LANES) % HEAD_DIM) < HEAD_DIM // 2
    return cos, jnp.where(first_half, -sin, 0.0), jnp.where(first_half, 0.0, sin)


def _pair_perm():
    idx = []
    for j in range(SWA_HEADS // 2):
        idx += list(range(j * HEAD_DIM, (j + 1) * HEAD_DIM))
        idx += list(range((j + 4) * HEAD_DIM, (j + 5) * HEAD_DIM))
    return jnp.asarray(idx, jnp.int32)


def _pad_lanes(v, offset):
    return jnp.zeros((1, LANES), F32).at[0, offset:offset + v.shape[0]].set(v.astype(F32))


def _prepare(norm_g, w_in0, conv_w, conv_b, conv_ln_g, conv_ln_b, sink, w_out0, w_in1, lambda_q1, lambda_k1,
             lambda_q2, lambda_k2, diff_norm_g, ssm_conv_w, ssm_conv_b, dt_bias_f, dt_bias_b, a_log_f, a_log_b,
             d_skip, ssm_norm_g, w_out1, final_norm_g):
    perm = _pair_perm()
    w0 = w_in0[0]
    w0 = w0.at[:, 1536:2048].set(w0[:, 1536:2048][:, perm]).at[:, 2304:2816].set(w0[:, 2304:2816][:, perm])
    wo0 = w_out0[0].at[512:1024, :].set(w_out0[0][512:1024, :][perm, :])
    w1 = jnp.pad(w_in1[0], ((0, 0), (0, IN_ODD_PAD - IN_ODD)))
    row = lambda v: v.astype(F32).reshape(1, -1)
    return dict(
        g0=row(norm_g[0]), g1=row(norm_g[1]), w0=w0.astype(BF16), wo0=wo0.astype(BF16), w1=w1.astype(BF16),
        wo1=w_out1[0].astype(BF16), cw=conv_w[0].astype(F32), cb=row(conv_b[0]), lg=row(conv_ln_g[0]),
        lb=row(conv_ln_b[0]), sink=sink[0].astype(F32),
        lq1=row(lambda_q1[0]), lk1=row(lambda_k1[0]), lq2=row(lambda_q2[0]), lk2=row(lambda_k2[0]),
        dg=row(diff_norm_g[0]), scw=ssm_conv_w[0].astype(F32), scb=row(ssm_conv_b[0]),
        dtb_f=_pad_lanes(dt_bias_f[0], 0), dtb_b=_pad_lanes(dt_bias_b[0], SSM_HEADS),
        a_f=_pad_lanes(-jnp.exp(a_log_f[0].astype(F32)), 0), a_b=_pad_lanes(-jnp.exp(a_log_b[0].astype(F32)), SSM_HEADS),
        dsk=row(jnp.repeat(d_skip[0].astype(F32), HEAD_DIM)), sg=row(ssm_norm_g[0]), fg=row(final_norm_g))


def _trunk(x, p, tm=512):
    bsz, seq, _ = x.shape
    assert seq % tm == 0 and seq % 512 == 0
    t = bsz * seq
    lam_init = 0.8 - 0.6 * math.exp(-0.3 * 1)
    cos, sa, sb = _rope_tables(seq)
    x2 = x.reshape(t, D_MODEL)
    apre, ga, q, k, v, gb = _inproj0(x2, p["g0"], p["w0"], cos, sa, sb, seq, tm)
    r3 = lambda arr: arr.reshape(bsz, seq, arr.shape[-1])
    a = _convmod(r3(apre), r3(ga), p["cw"], p["cb"], p["lg"], p["lb"])
    o = _swa(p["sink"], r3(q), r3(k), r3(v), r3(gb))
    x1, q1, k1, v1, cg, zg, xbc, dt = _mid(x2, a.reshape(t, 512), o.reshape(t, 512), p["wo0"], p["g1"], p["w1"],
                                           cos, sa, sb, seq, tm)
    c = _diff_attn(p["lq1"], p["lk1"], p["lq2"], p["lk2"], p["dg"], r3(q1), r3(k1), r3(v1), r3(cg), lam_init)
    yf = _ssd(r3(xbc), r3(dt), p["scw"], p["scb"], p["dtb_f"], p["a_f"], p["dsk"], reverse=False)
    yb = _ssd(r3(xbc), r3(dt), p["scw"], p["scb"], p["dtb_b"], p["a_b"], p["dsk"], reverse=True)
    out = _final(x1, c.reshape(t, 512), yf.reshape(t, 512), yb.reshape(t, 512), zg, p["sg"], p["wo1"], p["fg"], tm)
    return out.reshape(bsz, seq, D_MODEL)


def kernel(x_prompt, x_sample, norm_g, w_in0, conv_w, conv_b, conv_ln_g, conv_ln_b, sink, w_out0, w_in1, lambda_q1, lambda_k1, lambda_q2, lambda_k2, diff_norm_g, ssm_conv_w, ssm_conv_b, dt_bias_f, dt_bias_b, a_log_f, a_log_b, d_skip, ssm_norm_g, w_out1, final_norm_g):
    p = _prepare(norm_g, w_in0, conv_w, conv_b, conv_ln_g, conv_ln_b, sink, w_out0, w_in1, lambda_q1, lambda_k1,
                 lambda_q2, lambda_k2, diff_norm_g, ssm_conv_w, ssm_conv_b, dt_bias_f, dt_bias_b, a_log_f, a_log_b,
                 d_skip, ssm_norm_g, w_out1, final_norm_g)
    return (_trunk(x_prompt, p), _trunk(x_sample, p))
```

```python
import functools
import math

import jax
import jax.numpy as jnp
from jax import lax
from jax.experimental import pallas as pl
from jax.experimental.pallas import tpu as pltpu

F32 = jnp.float32
BF16 = jnp.bfloat16

D_MODEL = 1024
EPS = 1e-6
ROPE_THETA = 10000.0
HEAD_DIM = 64
LANES = 128
SUBLANES = 8
HALO = 16
CONV_CH = 512
CONV_WIDTH = 31
SWA_HEADS = 8
BLOCK = 128
DIFF_HEADS = 4
SSM_INNER = 512
SSM_HEADS = 8
SSM_STATE = 128
SSM_CONV = 5
SSM_CONV_DIM = 1024
IN_EVEN = 2816
IN_ODD = 3600
IN_ODD_PAD = 3712
NEG_BIG = -1e30
LOG2E = math.log2(math.e)
VMEM_LIMIT = 56 * 1024 * 1024


def _cparams(sem):
    return pltpu.CompilerParams(dimension_semantics=sem, vmem_limit_bytes=VMEM_LIMIT)


def _sigmoid(x):
    return 1.0 / (1.0 + jnp.exp(-x))


def _silu(x):
    return x * _sigmoid(x)


def _rmsnorm(x, g):
    return x * lax.rsqrt(jnp.mean(x * x, axis=-1, keepdims=True) + EPS) * g


def _rope_cols(x, cos, sa, sb):
    outs = []
    for j in range(x.shape[1] // LANES):
        xj = x[:, j * LANES:(j + 1) * LANES]
        outs.append(xj * cos + pltpu.roll(xj, LANES - 32, 1) * sa + pltpu.roll(xj, 32, 1) * sb)
    return outs[0] if len(outs) == 1 else jnp.concatenate(outs, axis=1)


def _inproj0_kernel(x_ref, g_ref, w_ref, cos_ref, sa_ref, sb_ref,
                    apre_ref, ga_ref, q_ref, k_ref, v_ref, gb_ref):
    hb = _rmsnorm(x_ref[...], g_ref[...]).astype(BF16)

    def mm(a, b):
        return jnp.dot(hb, w_ref[:, a:b], preferred_element_type=F32)

    cos, sa, sb = cos_ref[...], sa_ref[...], sb_ref[...]
    apre_ref[...] = (mm(0, 512) * _sigmoid(mm(512, 1024))).astype(BF16)
    ga_ref[...] = _silu(mm(1024, 1536)).astype(BF16)
    q_ref[...] = (_rope_cols(mm(1536, 2048), cos, sa, sb) * (HEAD_DIM ** -0.5)).astype(BF16)
    k_ref[...] = _rope_cols(mm(2048, 2176), cos, sa, sb).astype(BF16)
    v_ref[...] = mm(2176, 2304).astype(BF16)
    gb_ref[...] = _silu(mm(2304, 2816)).astype(BF16)


def _inproj0(x2, g, w, cos, sa, sb, seq, tm):
    t = x2.shape[0]
    ns = seq // tm
    row = lambda i: (i, 0)
    pos = lambda i: (i % ns, 0)
    full = lambda i: (0, 0)
    widths = (512, 512, 512, 128, 128, 512)
    return pl.pallas_call(
        _inproj0_kernel,
        grid=(t // tm,),
        in_specs=[pl.BlockSpec((tm, D_MODEL), row), pl.BlockSpec((1, D_MODEL), full),
                  pl.BlockSpec((D_MODEL, IN_EVEN), full),
                  pl.BlockSpec((tm, LANES), pos), pl.BlockSpec((tm, LANES), pos), pl.BlockSpec((tm, LANES), pos)],
        out_specs=[pl.BlockSpec((tm, wd), row) for wd in widths],
        out_shape=[jax.ShapeDtypeStruct((t, wd), BF16) for wd in widths],
        compiler_params=_cparams(("parallel",)),
        name="inproj0",
    )(x2, g, w, cos, sa, sb)


def _fill_halo_pad(pad_ref, xc, xp, xn, ts):
    i = pl.program_id(1)
    last = pl.num_programs(1) - 1
    pad_ref[0:HALO, :] = jnp.where(i > 0, xp.astype(F32), 0.0)
    pad_ref[HALO:HALO + ts, :] = xc.astype(F32)
    pad_ref[HALO + ts:2 * HALO + ts, :] = jnp.where(i < last, xn.astype(F32), 0.0)


def _convmod_kernel(xc_ref, xp_ref, xn_ref, ga_ref, cw_ref, cb_ref, lg_ref, lb_ref, o_ref, pad_ref, rol_ref,
                    *, ts, rs):
    _fill_halo_pad(pad_ref, xc_ref[0], xp_ref[0], xn_ref[0], ts)
    rows = ts + 2 * HALO
    window = pad_ref[...]
    for j in range(1, SUBLANES):
        rol_ref[j - 1] = pltpu.roll(window, rows - j, 0)
    first = HALO - (CONV_WIDTH - 1) // 2
    for r0 in range(0, ts, rs):
        acc = jnp.broadcast_to(cb_ref[...], (rs, CONV_CH))
        for w in range(CONV_WIDTH):
            a, j = divmod(first + w, SUBLANES)
            lo = r0 + a * SUBLANES
            tap = pad_ref[lo:lo + rs, :] if j == 0 else rol_ref[j - 1, lo:lo + rs, :]
            acc = acc + tap * jnp.concatenate([cw_ref[w]] * (rs // SUBLANES), axis=0)
        mu = jnp.mean(acc, axis=-1, keepdims=True)
        xc = acc - mu
        var = jnp.mean(xc * xc, axis=-1, keepdims=True)
        y = xc * lax.rsqrt(var + EPS) * lg_ref[...] + lb_ref[...]
        o_ref[0, r0:r0 + rs, :] = (_silu(y) * ga_ref[0, r0:r0 + rs, :].astype(F32)).astype(BF16)


def _halo_specs(ts, seq, width):
    hb = ts // HALO
    nh = seq // HALO
    return [pl.BlockSpec((1, ts, width), lambda b, i: (b, i, 0)),
            pl.BlockSpec((1, HALO, width), lambda b, i: (b, jnp.maximum(i * hb - 1, 0), 0)),
            pl.BlockSpec((1, HALO, width), lambda b, i: (b, jnp.minimum((i + 1) * hb, nh - 1), 0))]


def _convmod(apre, ga, cw, cb, lg, lb, ts=256, rs=32):
    bsz, seq, _ = apre.shape
    vec = pl.BlockSpec((1, CONV_CH), lambda b, i: (0, 0))
    return pl.pallas_call(
        functools.partial(_convmod_kernel, ts=ts, rs=rs),
        grid=(bsz, seq // ts),
        in_specs=_halo_specs(ts, seq, CONV_CH) + [
            pl.BlockSpec((1, ts, CONV_CH), lambda b, i: (b, i, 0)),
            pl.BlockSpec((CONV_WIDTH, SUBLANES, CONV_CH), lambda b, i: (0, 0, 0)), vec, vec, vec],
        out_specs=pl.BlockSpec((1, ts, CONV_CH), lambda b, i: (b, i, 0)),
        out_shape=jax.ShapeDtypeStruct((bsz, seq, CONV_CH), BF16),
        scratch_shapes=[pltpu.VMEM((ts + 2 * HALO, CONV_CH), F32),
                        pltpu.VMEM((SUBLANES - 1, ts + 2 * HALO, CONV_CH), F32)],
        compiler_params=_cparams(("parallel", "parallel")),
        name="convmod",
    )(apre, apre, apre, ga, cw, cb, lg, lb)


def _swa_kernel(sink_ref, q_ref, kp_ref, kc_ref, kn_ref, vp_ref, vc_ref, vn_ref, gb_ref, o_ref):
    n = pl.program_id(1)
    last = pl.num_programs(1) - 1
    kcat = jnp.concatenate([kp_ref[0], kc_ref[0], kn_ref[0]], axis=0)
    vcat = jnp.concatenate([vp_ref[0], vc_ref[0], vn_ref[0]], axis=0)
    rows = lax.broadcasted_iota(jnp.int32, (2 * BLOCK, 3 * BLOCK), 0)
    qi = jnp.where(rows >= BLOCK, rows - BLOCK, rows)
    ci = lax.broadcasted_iota(jnp.int32, (2 * BLOCK, 3 * BLOCK), 1)
    rel = ci - BLOCK - qi
    first_key = jnp.where(n > 0, 0, BLOCK)
    end_key = jnp.where(n < last, 3 * BLOCK, 2 * BLOCK)
    mask = (rel >= -BLOCK) & (rel <= BLOCK) & (ci >= first_key) & (ci < end_key)
    lane = lax.broadcasted_iota(jnp.int32, (BLOCK, LANES), 1)
    low = lane < HEAD_DIM
    row1 = lax.broadcasted_iota(jnp.int32, (2 * BLOCK, 1), 0)
    for j in range(SWA_HEADS // 2):
        qv = q_ref[0, :, j * LANES:(j + 1) * LANES]
        zero = jnp.zeros_like(qv)
        qq = jnp.concatenate([jnp.where(low, qv, zero), jnp.where(low, zero, qv)], axis=0)
        s = lax.dot_general(qq, kcat, (((1,), (1,)), ((), ())), preferred_element_type=F32)
        s = jnp.where(mask, s, NEG_BIG)
        sink = jnp.where(row1 < BLOCK, sink_ref[j], sink_ref[j + SWA_HEADS // 2])
        m = jnp.maximum(jnp.max(s, axis=-1, keepdims=True), sink)
        p = jnp.exp(s - m)
        denom = jnp.sum(p, axis=-1, keepdims=True) + jnp.exp(sink - m)
        pv = jnp.dot(p.astype(BF16), vcat, preferred_element_type=F32) / denom
        o = jnp.where(low, pv[:BLOCK], pv[BLOCK:])
        o_ref[0, :, j * LANES:(j + 1) * LANES] = (
            o * gb_ref[0, :, j * LANES:(j + 1) * LANES].astype(F32)).astype(BF16)


def _swa(sink, q, k, v, gb):
    bsz, seq, _ = q.shape
    nb = seq // BLOCK
    cur = lambda b, n: (b, n, 0)
    prv = lambda b, n: (b, jnp.maximum(n - 1, 0), 0)
    nxt = lambda b, n: (b, jnp.minimum(n + 1, nb - 1), 0)
    kv = lambda im: pl.BlockSpec((1, BLOCK, LANES), im)
    wide = pl.BlockSpec((1, BLOCK, 512), cur)
    return pl.pallas_call(
        _swa_kernel,
        grid=(bsz, nb),
        in_specs=[pl.BlockSpec(memory_space=pltpu.SMEM), wide,
                  kv(prv), kv(cur), kv(nxt), kv(prv), kv(cur), kv(nxt), wide],
        out_specs=wide,
        out_shape=jax.ShapeDtypeStruct((bsz, seq, 512), BF16),
        compiler_params=_cparams(("parallel", "parallel")),
        name="swa",
    )(sink, q, k, k, k, v, v, v, gb)


def _mid_kernel(x_ref, a_ref, o_ref, wo_ref, g_ref, w_ref, cos_ref, sa_ref, sb_ref,
                x1_ref, q_ref, k_ref, v_ref, cg_ref, zg_ref, xbc_ref, dt_ref):
    x1 = (x_ref[...] + jnp.dot(a_ref[...], wo_ref[0:512, :], preferred_element_type=F32)
          + jnp.dot(o_ref[...], wo_ref[512:1024, :], preferred_element_type=F32))
    x1_ref[...] = x1
    hb = _rmsnorm(x1, g_ref[...]).astype(BF16)

    def mm(a, b):
        return jnp.dot(hb, w_ref[:, a:b], preferred_element_type=F32)

    cos, sa, sb = cos_ref[...], sa_ref[...], sb_ref[...]
    q_ref[...] = (_rope_cols(mm(0, 512), cos, sa, sb) * (HEAD_DIM ** -0.5 * LOG2E)).astype(BF16)
    k_ref[...] = _rope_cols(mm(512, 1024), cos, sa, sb).astype(BF16)
    v = mm(1024, 1536).astype(BF16)
    ones = jnp.ones((v.shape[0], LANES), BF16)
    for h in range(DIFF_HEADS):
        v_ref[:, 2 * h * LANES:(2 * h + 1) * LANES] = v[:, h * LANES:(h + 1) * LANES]
        v_ref[:, (2 * h + 1) * LANES:(2 * h + 2) * LANES] = ones
    cg_ref[...] = _silu(mm(1536, 2048)).astype(BF16)
    zg_ref[...] = _silu(mm(2048, 2560)).astype(BF16)
    xbc_ref[...] = mm(2560, 3584).astype(BF16)
    dt_ref[...] = mm(3584, IN_ODD_PAD)


def _mid(x2, a2, o2, wo, g, w, cos, sa, sb, seq, tm):
    t = x2.shape[0]
    ns = seq // tm
    row = lambda i: (i, 0)
    pos = lambda i: (i % ns, 0)
    full = lambda i: (0, 0)
    outs = ((D_MODEL, F32), (512, BF16), (512, BF16), (1024, BF16), (512, BF16), (512, BF16),
            (SSM_CONV_DIM, BF16), (LANES, F32))
    return pl.pallas_call(
        _mid_kernel,
        grid=(t // tm,),
        in_specs=[pl.BlockSpec((tm, D_MODEL), row), pl.BlockSpec((tm, 512), row), pl.BlockSpec((tm, 512), row),
                  pl.BlockSpec((D_MODEL, D_MODEL), full), pl.BlockSpec((1, D_MODEL), full),
                  pl.BlockSpec((D_MODEL, IN_ODD_PAD), full),
                  pl.BlockSpec((tm, LANES), pos), pl.BlockSpec((tm, LANES), pos), pl.BlockSpec((tm, LANES), pos)],
        out_specs=[pl.BlockSpec((tm, wd), row) for wd, _ in outs],
        out_shape=[jax.ShapeDtypeStruct((t, wd), dt) for wd, dt in outs],
        compiler_params=_cparams(("parallel",)),
        name="outproj0_inproj1",
    )(x2, a2, o2, wo, g, w, cos, sa, sb)


def _diff_kernel(lq1_ref, lk1_ref, lq2_ref, lk2_ref, g_ref, q_ref, qn_ref, k_ref, v_ref, cg_ref, o_ref,
                 s0_ref, s1_ref, mx0_ref, mx1_ref, m_ref, acc_ref, *, kc, lam_init):
    tq = q_ref.shape[1]
    nch = k_ref.shape[1] // kc
    lane = lax.broadcasted_iota(jnp.int32, (tq, LANES), 1)

    def stack_maps(q):
        zero = jnp.zeros_like(q)
        return jnp.concatenate([jnp.where(lane < HEAD_DIM, q, zero), jnp.where(lane < HEAD_DIM, zero, q)], axis=0)

    qq = stack_maps(q_ref[0])
    qq_next = stack_maps(qn_ref[0])
    slots = ((s0_ref, mx0_ref), (s1_ref, mx1_ref))

    def scores(qs, c, slot):
        s_ref, mx_ref = slots[slot]
        kch = k_ref[0, pl.ds(pl.multiple_of(c * kc, kc), kc), :]
        s = lax.dot_general(qs, kch, (((1,), (1,)), ((), ())), preferred_element_type=F32)
        s_ref[...] = s
        mx_ref[...] = jnp.broadcast_to(jnp.max(s, axis=-1, keepdims=True), mx_ref.shape)

    def update(c, slot):
        s_ref, mx_ref = slots[slot]
        vch = v_ref[0, pl.ds(pl.multiple_of(c * kc, kc), kc), :]
        m_old = m_ref[...]
        m_new = jnp.maximum(m_old, mx_ref[...])
        alpha = jnp.exp2(m_old - m_new)
        p = jnp.exp2(s_ref[...] - jnp.concatenate([m_new] * (kc // LANES), axis=1)).astype(BF16)
        acc_ref[...] = acc_ref[...] * jnp.concatenate([alpha, alpha], axis=1) + jnp.dot(
            p, vch, preferred_element_type=F32)
        m_ref[...] = m_new

    m_ref[...] = jnp.full(m_ref.shape, NEG_BIG, F32)
    acc_ref[...] = jnp.zeros(acc_ref.shape, F32)

    @pl.when(pl.program_id(2) == 0)
    def _():
        scores(qq, 0, 0)

    def body(t, carry):
        scores(qq, 2 * t + 1, 1)
        update(2 * t, 0)
        wrap = t == nch // 2 - 1
        scores(jnp.where(wrap, qq_next, qq), jnp.where(wrap, 0, 2 * t + 2), 0)
        update(2 * t + 1, 1)
        return carry

    lax.fori_loop(0, nch // 2, body, 0)

    lam = (jnp.exp(jnp.sum(lq1_ref[...] * lk1_ref[...], axis=-1, keepdims=True))
           - jnp.exp(jnp.sum(lq2_ref[...] * lk2_ref[...], axis=-1, keepdims=True)) + lam_init)
    o0 = acc_ref[0:tq, 0:LANES] / acc_ref[0:tq, LANES:2 * LANES]
    o1 = acc_ref[tq:2 * tq, 0:LANES] / acc_ref[tq:2 * tq, LANES:2 * LANES]
    o = o0 - lam * o1
    o = o * lax.rsqrt(jnp.mean(o * o, axis=-1, keepdims=True) + EPS) * g_ref[...] * (1.0 - lam_init)
    o_ref[0] = (o * cg_ref[0].astype(F32)).astype(BF16)


def _diff_attn(lq1, lk1, lq2, lk2, g, q, k, v, cg, lam_init, tq=256):
    bsz, seq, _ = q.shape
    kc = next(c for c in (1024, 512, 256) if seq % (2 * c) == 0)
    nq = seq // tq
    small = pl.BlockSpec((1, HEAD_DIM), lambda b, h, i: (0, 0))
    tile = pl.BlockSpec((1, tq, LANES), lambda b, h, i: (b, i, h))
    next_tile = pl.BlockSpec((1, tq, LANES), lambda b, h, i: (b, jnp.minimum(i + 1, nq - 1), h))
    return pl.pallas_call(
        functools.partial(_diff_kernel, kc=kc, lam_init=lam_init),
        grid=(bsz, DIFF_HEADS, nq),
        in_specs=[small, small, small, small, pl.BlockSpec((1, LANES), lambda b, h, i: (0, 0)), tile, next_tile,
                  pl.BlockSpec((1, seq, LANES), lambda b, h, i: (b, 0, h)),
                  pl.BlockSpec((1, seq, 2 * LANES), lambda b, h, i: (b, 0, h)), tile],
        out_specs=tile,
        out_shape=jax.ShapeDtypeStruct((bsz, seq, 512), BF16),
        scratch_shapes=[pltpu.VMEM((2 * tq, kc), F32), pltpu.VMEM((2 * tq, kc), F32),
                        pltpu.VMEM((2 * tq, LANES), F32), pltpu.VMEM((2 * tq, LANES), F32),
                        pltpu.VMEM((2 * tq, LANES), F32), pltpu.VMEM((2 * tq, 2 * LANES), F32)],
        compiler_params=_cparams(("parallel", "parallel", "arbitrary")),
        name="diff_attn",
    )(lq1, lk1, lq2, lk2, g, q, q, k, v, cg)


def _split3(x):
    hi = x.astype(BF16)
    r = x - hi.astype(F32)
    mid = r.astype(BF16)
    lo = (r - mid.astype(F32)).astype(BF16)
    return hi, mid, lo


def _ssd_fwd_kernel(xc_ref, xp_ref, xn_ref, dt_ref, cw_ref, cb_ref, dtb_ref, a_ref, dsk_ref, y_ref, xconv_ref,
                    carry_ref, xf_ref):
    L = BLOCK
    c = pl.program_id(1)
    last = pl.num_programs(1) - 1

    @pl.when(c == 0)
    def _():
        carry_ref[...] = jnp.zeros(carry_ref.shape, F32)

    xp, xn = xp_ref[0], xn_ref[0]
    halo_p = jnp.where(c > 0, xp, jnp.zeros_like(xp))
    halo_n = jnp.where(c < last, xn, jnp.zeros_like(xn))
    r_i = lax.broadcasted_iota(jnp.int32, (L, L + 2 * HALO), 0)
    c_i = lax.broadcasted_iota(jnp.int32, (L, L + 2 * HALO), 1)
    first = HALO - (SSM_CONV - 1) // 2
    shifts = [None if first + w == HALO else jnp.where(c_i == r_i + first + w, 1.0, 0.0).astype(BF16)
              for w in range(SSM_CONV)]
    wd = 2 * LANES
    for nb in range(SSM_CONV_DIM // wd):
        cols = slice(nb * wd, (nb + 1) * wd)
        centre = xc_ref[0, :, cols]
        window = jnp.concatenate([halo_p[:, cols], centre, halo_n[:, cols]], axis=0)
        acc = jnp.broadcast_to(cb_ref[:, cols], (L, wd))
        for w in range(SSM_CONV):
            tap = centre.astype(F32) if shifts[w] is None else jnp.dot(shifts[w], window, preferred_element_type=F32)
            acc = acc + tap * cw_ref[w:w + 1, cols]
        xbc = _silu(acc)
        xconv_ref[0, :, cols] = xbc.astype(BF16)
        xf_ref[:, cols] = xbc
    _ssd_scan_chunk(lambda a, b: xf_ref[:, a:b], dt_ref, dtb_ref, a_ref, dsk_ref, y_ref, carry_ref,
                    reverse=False, col0=0)


def _ssd_bwd_kernel(xconv_ref, dt_ref, dtb_ref, a_ref, y_ref, carry_ref):
    @pl.when(pl.program_id(1) == 0)
    def _():
        carry_ref[...] = jnp.zeros(carry_ref.shape, F32)

    _ssd_scan_chunk(lambda a, b: xconv_ref[0, :, a:b].astype(F32), dt_ref, dtb_ref, a_ref, None, y_ref, carry_ref,
                    reverse=True, col0=SSM_HEADS)


def _ssd_scan_chunk(xcols, dt_ref, dtb_ref, a_ref, dsk_ref, y_ref, carry_ref, *, reverse, col0):
    L = BLOCK
    bm = [xcols(SSM_INNER + g * SSM_STATE, SSM_INNER + (g + 1) * SSM_STATE) for g in range(2)]
    cm = [xcols(SSM_INNER + (2 + g) * SSM_STATE, SSM_INNER + (3 + g) * SSM_STATE) for g in range(2)]

    z = dt_ref[0] + dtb_ref[...]
    dt = jnp.maximum(z, 0.0) + jnp.log(1.0 + jnp.exp(-jnp.abs(z)))
    a = dt * a_ref[...]
    r_i = lax.broadcasted_iota(jnp.int32, (L, L), 0)
    c_i = lax.broadcasted_iota(jnp.int32, (L, L), 1)
    causal = (r_i <= c_i) if reverse else (r_i >= c_i)
    tri = jnp.where(causal, 1.0, 0.0).astype(BF16)
    acum = sum(jnp.dot(tri, piece, preferred_element_type=F32) for piece in _split3(a))
    acum_t = acum.T
    edge = 0 if reverse else L - 1
    a_end = acum[edge:edge + 1, :]
    eac = jnp.exp(acum)
    dte = jnp.exp(a_end - acum)
    cdec = jnp.exp(a_end)

    lane = lax.broadcasted_iota(jnp.int32, (L, LANES), 1)
    low = lane < HEAD_DIM
    lane1 = lax.broadcasted_iota(jnp.int32, (1, LANES), 1)

    def pair_cols(mat, j):
        ca = col0 + 2 * j
        return jnp.where(low, mat[:, ca:ca + 1], mat[:, ca + 1:ca + 2])

    cbs = [lax.dot_general(cm[g].astype(BF16), bm[g].astype(BF16), (((1,), (1,)), ((), ())),
                           preferred_element_type=F32) for g in range(2)]
    bts = [bm[g].T.astype(BF16) for g in range(2)]
    cms = [cm[g].astype(BF16) for g in range(2)]
    for j in range(SSM_HEADS // 2):
        g = j // 2
        xs_j = xcols(j * LANES, (j + 1) * LANES)
        xdt = xs_j * pair_cols(dt, j)
        xdt_b = xdt.astype(BF16)
        zero = jnp.zeros_like(xdt_b)
        ms = []
        for hh in range(2):
            col = col0 + 2 * j + hh
            seg = acum[:, col:col + 1] - acum_t[col:col + 1, :]
            dec = jnp.where(causal, jnp.exp(jnp.where(causal, seg, 0.0)), 0.0)
            ms.append((cbs[g] * dec).astype(BF16))
        lhs = jnp.concatenate(ms, axis=1)
        rhs = jnp.concatenate([jnp.where(low, xdt_b, zero), jnp.where(low, zero, xdt_b)], axis=0)
        y = jnp.dot(lhs, rhs, preferred_element_type=F32)
        prev = carry_ref[j]
        y = y + jnp.dot(cms[g], prev.astype(BF16), preferred_element_type=F32) * pair_cols(eac, j)
        st = jnp.dot(bts[g], (xdt * pair_cols(dte, j)).astype(BF16), preferred_element_type=F32)
        ca = col0 + 2 * j
        cd = jnp.where(lane1 < HEAD_DIM, cdec[:, ca:ca + 1], cdec[:, ca + 1:ca + 2])
        carry_ref[j] = prev * cd + st
        if dsk_ref is not None:
            y = y + xs_j * dsk_ref[:, j * LANES:(j + 1) * LANES]
        y_ref[0, :, j * LANES:(j + 1) * LANES] = y


_SSD_CARRY = pltpu.VMEM((SSM_HEADS // 2, SSM_STATE, LANES), F32)


def _ssd_fwd(xbc, dt, cw, cb, dtb, a, dsk):
    bsz, seq, _ = xbc.shape
    vec = lambda wd: pl.BlockSpec((1, wd), lambda b, c: (0, 0))
    chunk = lambda wd: pl.BlockSpec((1, BLOCK, wd), lambda b, c: (b, c, 0))
    return pl.pallas_call(
        _ssd_fwd_kernel,
        grid=(bsz, seq // BLOCK),
        in_specs=_halo_specs(BLOCK, seq, SSM_CONV_DIM) + [
            chunk(LANES), pl.BlockSpec((SSM_CONV, SSM_CONV_DIM), lambda b, c: (0, 0)), vec(SSM_CONV_DIM),
            vec(LANES), vec(LANES), vec(SSM_INNER)],
        out_specs=[chunk(SSM_INNER), chunk(SSM_CONV_DIM)],
        out_shape=[jax.ShapeDtypeStruct((bsz, seq, SSM_INNER), F32),
                   jax.ShapeDtypeStruct((bsz, seq, SSM_CONV_DIM), BF16)],
        scratch_shapes=[_SSD_CARRY, pltpu.VMEM((BLOCK, SSM_CONV_DIM), F32)],
        compiler_params=_cparams(("parallel", "arbitrary")),
        name="ssd_fwd",
    )(xbc, xbc, xbc, dt, cw, cb, dtb, a, dsk)


def _ssd_bwd(xconv, dt, dtb, a):
    bsz, seq, _ = xconv.shape
    nc = seq // BLOCK
    vec = lambda wd: pl.BlockSpec((1, wd), lambda b, c: (0, 0))
    chunk = lambda wd: pl.BlockSpec((1, BLOCK, wd), lambda b, c: (b, nc - 1 - c, 0))
    return pl.pallas_call(
        _ssd_bwd_kernel,
        grid=(bsz, nc),
        in_specs=[chunk(SSM_CONV_DIM), chunk(LANES), vec(LANES), vec(LANES)],
        out_specs=chunk(SSM_INNER),
        out_shape=jax.ShapeDtypeStruct((bsz, seq, SSM_INNER), F32),
        scratch_shapes=[_SSD_CARRY],
        compiler_params=_cparams(("parallel", "arbitrary")),
        name="ssd_bwd",
    )(xconv, dt, dtb, a)


def _final_kernel(x1_ref, c_ref, yf_ref, yb_ref, zg_ref, sg_ref, wo_ref, fg_ref, o_ref):
    y = (yf_ref[...] + yb_ref[...]) * zg_ref[...].astype(F32)
    d = _rmsnorm(y, sg_ref[...]).astype(BF16)
    x2 = (x1_ref[...] + jnp.dot(c_ref[...], wo_ref[0:512, :], preferred_element_type=F32)
          + jnp.dot(d, wo_ref[512:1024, :], preferred_element_type=F32))
    o_ref[...] = _rmsnorm(x2, fg_ref[...])


def _final(x1, c2, yf, yb, zg, sg, wo, fg, tm):
    t = x1.shape[0]
    row = lambda i: (i, 0)
    full = lambda i: (0, 0)
    half = pl.BlockSpec((tm, 512), row)
    return pl.pallas_call(
        _final_kernel,
        grid=(t // tm,),
        in_specs=[pl.BlockSpec((tm, D_MODEL), row), half, half, half, half,
                  pl.BlockSpec((1, 512), full), pl.BlockSpec((D_MODEL, D_MODEL), full),
                  pl.BlockSpec((1, D_MODEL), full)],
        out_specs=pl.BlockSpec((tm, D_MODEL), row),
        out_shape=jax.ShapeDtypeStruct((t, D_MODEL), F32),
        compiler_params=_cparams(("parallel",)),
        name="outproj1_final",
    )(x1, c2, yf, yb, zg, sg, wo, fg)


def _rope_tables(seq):
    inv = 1.0 / (ROPE_THETA ** (jnp.arange(0, HEAD_DIM, 2, dtype=F32) / HEAD_DIM))
    f = jnp.arange(seq, dtype=F32)[:, None] * inv[None, :]
    emb = jnp.concatenate([f, f, f, f], axis=-1)
    cos, sin = jnp.cos(emb), jnp.sin(emb)
    first_half = (jnp.arange(LANES) % HEAD_DIM) < HEAD_DIM // 2
    return cos, jnp.where(first_half, -sin, 0.0), jnp.where(first_half, 0.0, sin)


def _pair_perm():
    idx = []
    for j in range(SWA_HEADS // 2):
        idx += list(range(j * HEAD_DIM, (j + 1) * HEAD_DIM))
        idx += list(range((j + 4) * HEAD_DIM, (j + 5) * HEAD_DIM))
    return jnp.asarray(idx, jnp.int32)


def _pad_lanes(v, offset):
    return jnp.zeros((1, LANES), F32).at[0, offset:offset + v.shape[0]].set(v.astype(F32))


def _prepare(norm_g, w_in0, conv_w, conv_b, conv_ln_g, conv_ln_b, sink, w_out0, w_in1, lambda_q1, lambda_k1,
             lambda_q2, lambda_k2, diff_norm_g, ssm_conv_w, ssm_conv_b, dt_bias_f, dt_bias_b, a_log_f, a_log_b,
             d_skip, ssm_norm_g, w_out1, final_norm_g):
    perm = _pair_perm()
    w0 = w_in0[0]
    w0 = w0.at[:, 1536:2048].set(w0[:, 1536:2048][:, perm]).at[:, 2304:2816].set(w0[:, 2304:2816][:, perm])
    wo0 = w_out0[0].at[512:1024, :].set(w_out0[0][512:1024, :][perm, :])
    w1 = jnp.pad(w_in1[0], ((0, 0), (0, IN_ODD_PAD - IN_ODD)))
    row = lambda v: v.astype(F32).reshape(1, -1)
    return dict(
        g0=row(norm_g[0]), g1=row(norm_g[1]), w0=w0.astype(BF16), wo0=wo0.astype(BF16), w1=w1.astype(BF16),
        wo1=w_out1[0].astype(BF16), cb=row(conv_b[0]), lg=row(conv_ln_g[0]),
        cw=jnp.broadcast_to(conv_w[0].astype(F32)[:, None, :], (CONV_WIDTH, SUBLANES, CONV_CH)),
        lb=row(conv_ln_b[0]), sink=sink[0].astype(F32),
        lq1=row(lambda_q1[0]), lk1=row(lambda_k1[0]), lq2=row(lambda_q2[0]), lk2=row(lambda_k2[0]),
        dg=row(diff_norm_g[0]), scw=ssm_conv_w[0].astype(F32), scb=row(ssm_conv_b[0]),
        dtb_f=_pad_lanes(dt_bias_f[0], 0), dtb_b=_pad_lanes(dt_bias_b[0], SSM_HEADS),
        a_f=_pad_lanes(-jnp.exp(a_log_f[0].astype(F32)), 0), a_b=_pad_lanes(-jnp.exp(a_log_b[0].astype(F32)), SSM_HEADS),
        dsk=row(jnp.repeat(d_skip[0].astype(F32), HEAD_DIM)), sg=row(ssm_norm_g[0]), fg=row(final_norm_g))


def _trunk(x, p, tm=512):
    bsz, seq, _ = x.shape
    assert seq % tm == 0 and seq % 512 == 0
    t = bsz * seq
    lam_init = 0.8 - 0.6 * math.exp(-0.3 * 1)
    cos, sa, sb = _rope_tables(seq)
    x2 = x.reshape(t, D_MODEL)
    apre, ga, q, k, v, gb = _inproj0(x2, p["g0"], p["w0"], cos, sa, sb, seq, tm)
    r3 = lambda arr: arr.reshape(bsz, seq, arr.shape[-1])
    a = _convmod(r3(apre), r3(ga), p["cw"], p["cb"], p["lg"], p["lb"])
    o = _swa(p["sink"], r3(q), r3(k), r3(v), r3(gb))
    x1, q1, k1, v1, cg, zg, xbc, dt = _mid(x2, a.reshape(t, 512), o.reshape(t, 512), p["wo0"], p["g1"], p["w1"],
                                           cos, sa, sb, seq, tm)
    c = _diff_attn(p["lq1"], p["lk1"], p["lq2"], p["lk2"], p["dg"], r3(q1), r3(k1), r3(v1), r3(cg), lam_init)
    yf, xconv = _ssd_fwd(r3(xbc), r3(dt), p["scw"], p["scb"], p["dtb_f"], p["a_f"], p["dsk"])
    yb = _ssd_bwd(xconv, r3(dt), p["dtb_b"], p["a_b"])
    out = _final(x1, c.reshape(t, 512), yf.reshape(t, 512), yb.reshape(t, 512), zg, p["sg"], p["wo1"], p["fg"], tm)
    return out.reshape(bsz, seq, D_MODEL)


def kernel(x_prompt, x_sample, norm_g, w_in0, conv_w, conv_b, conv_ln_g, conv_ln_b, sink, w_out0, w_in1, lambda_q1, lambda_k1, lambda_q2, lambda_k2, diff_norm_g, ssm_conv_w, ssm_conv_b, dt_bias_f, dt_bias_b, a_log_f, a_log_b, d_skip, ssm_norm_g, w_out1, final_norm_g):
    p = _prepare(norm_g, w_in0, conv_w, conv_b, conv_ln_g, conv_ln_b, sink, w_out0, w_in1, lambda_q1, lambda_k1,
                 lambda_q2, lambda_k2, diff_norm_g, ssm_conv_w, ssm_conv_b, dt_bias_f, dt_bias_b, a_log_f, a_log_b,
                 d_skip, ssm_norm_g, w_out1, final_norm_g)
    return (_trunk(x_prompt, p), _trunk(x_sample, p))
```

```python
import functools
import math

import jax
import jax.numpy as jnp
from jax import lax
from jax.experimental import pallas as pl
from jax.experimental.pallas import tpu as pltpu

F32 = jnp.float32
BF16 = jnp.bfloat16

D_MODEL = 1024
EPS = 1e-6
ROPE_THETA = 10000.0
HEAD_DIM = 64
LANES = 128
SUBLANES = 8
HALO = 16
CONV_CH = 512
CONV_WIDTH = 31
SWA_HEADS = 8
BLOCK = 128
DIFF_HEADS = 4
SSM_INNER = 512
SSM_HEADS = 8
SSM_STATE = 128
SSM_CONV = 5
SSM_CONV_DIM = 1024
IN_EVEN = 2816
IN_ODD = 3600
IN_ODD_PAD = 3712
NEG_BIG = -1e30
LOG2E = math.log2(math.e)
VMEM_LIMIT = 56 * 1024 * 1024


def _cparams(sem):
    return pltpu.CompilerParams(dimension_semantics=sem, vmem_limit_bytes=VMEM_LIMIT)


def _sigmoid(x):
    return 1.0 / (1.0 + jnp.exp(-x))


def _silu(x):
    return x * _sigmoid(x)


def _rmsnorm(x, g):
    return x * lax.rsqrt(jnp.mean(x * x, axis=-1, keepdims=True) + EPS) * g


def _rope_cols(x, cos, sa, sb):
    outs = []
    for j in range(x.shape[1] // LANES):
        xj = x[:, j * LANES:(j + 1) * LANES]
        outs.append(xj * cos + pltpu.roll(xj, LANES - 32, 1) * sa + pltpu.roll(xj, 32, 1) * sb)
    return outs[0] if len(outs) == 1 else jnp.concatenate(outs, axis=1)


def _inproj0_kernel(x_ref, g_ref, w_ref, cos_ref, sa_ref, sb_ref,
                    apre_ref, ga_ref, q_ref, k_ref, v_ref, gb_ref):
    hb = _rmsnorm(x_ref[...], g_ref[...]).astype(BF16)

    def mm(a, b):
        return jnp.dot(hb, w_ref[:, a:b], preferred_element_type=F32)

    cos, sa, sb = cos_ref[...], sa_ref[...], sb_ref[...]
    apre_ref[...] = (mm(0, 512) * _sigmoid(mm(512, 1024))).astype(BF16)
    ga_ref[...] = _silu(mm(1024, 1536)).astype(BF16)
    gb_ref[...] = _silu(mm(2304, 2816)).astype(BF16)
    q_ref[...] = (_rope_cols(mm(1536, 2048), cos, sa, sb) * (HEAD_DIM ** -0.5 * LOG2E)).astype(BF16)
    k_ref[...] = _rope_cols(mm(2048, 2176), cos, sa, sb).astype(BF16)
    v_ref[...] = mm(2176, 2304).astype(BF16)


def _inproj0(x2, g, w, cos, sa, sb, seq, tm):
    t = x2.shape[0]
    ns = seq // tm
    row = lambda i: (i, 0)
    pos = lambda i: (i % ns, 0)
    full = lambda i: (0, 0)
    widths = (512, 512, 512, 128, 128, 512)
    return pl.pallas_call(
        _inproj0_kernel,
        grid=(t // tm,),
        in_specs=[pl.BlockSpec((tm, D_MODEL), row), pl.BlockSpec((1, D_MODEL), full),
                  pl.BlockSpec((D_MODEL, IN_EVEN), full),
                  pl.BlockSpec((tm, LANES), pos), pl.BlockSpec((tm, LANES), pos), pl.BlockSpec((tm, LANES), pos)],
        out_specs=[pl.BlockSpec((tm, wd), row) for wd in widths],
        out_shape=[jax.ShapeDtypeStruct((t, wd), BF16) for wd in widths],
        compiler_params=_cparams(("parallel",)),
        name="inproj0",
    )(x2, g, w, cos, sa, sb)


def _fill_halo_pad(pad_ref, xc, xp, xn, ts):
    i = pl.program_id(1)
    last = pl.num_programs(1) - 1
    pad_ref[0:HALO, :] = jnp.where(i > 0, xp.astype(F32), 0.0)
    pad_ref[HALO:HALO + ts, :] = xc.astype(F32)
    pad_ref[HALO + ts:2 * HALO + ts, :] = jnp.where(i < last, xn.astype(F32), 0.0)


def _convmod_kernel(xc_ref, xp_ref, xn_ref, ga_ref, cw_ref, cb_ref, lg_ref, lb_ref, o_ref, pad_ref, rol_ref,
                    *, ts, rs):
    _fill_halo_pad(pad_ref, xc_ref[0], xp_ref[0], xn_ref[0], ts)
    rows = ts + 2 * HALO
    window = pad_ref[...]
    for j in range(1, SUBLANES):
        rol_ref[j - 1] = pltpu.roll(window, rows - j, 0)
    first = HALO - (CONV_WIDTH - 1) // 2
    for r0 in range(0, ts, rs):
        acc = jnp.broadcast_to(cb_ref[...], (rs, CONV_CH))
        for w in range(CONV_WIDTH):
            a, j = divmod(first + w, SUBLANES)
            lo = r0 + a * SUBLANES
            tap = pad_ref[lo:lo + rs, :] if j == 0 else rol_ref[j - 1, lo:lo + rs, :]
            acc = acc + tap * jnp.concatenate([cw_ref[w]] * (rs // SUBLANES), axis=0)
        mu = jnp.mean(acc, axis=-1, keepdims=True)
        xc = acc - mu
        var = jnp.mean(xc * xc, axis=-1, keepdims=True)
        y = xc * lax.rsqrt(var + EPS) * lg_ref[...] + lb_ref[...]
        o_ref[0, r0:r0 + rs, :] = (_silu(y) * ga_ref[0, r0:r0 + rs, :].astype(F32)).astype(BF16)


def _halo_specs(ts, seq, width):
    hb = ts // HALO
    nh = seq // HALO
    return [pl.BlockSpec((1, ts, width), lambda b, i: (b, i, 0)),
            pl.BlockSpec((1, HALO, width), lambda b, i: (b, jnp.maximum(i * hb - 1, 0), 0)),
            pl.BlockSpec((1, HALO, width), lambda b, i: (b, jnp.minimum((i + 1) * hb, nh - 1), 0))]


def _convmod(apre, ga, cw, cb, lg, lb, ts=512, rs=32):
    bsz, seq, _ = apre.shape
    vec = pl.BlockSpec((1, CONV_CH), lambda b, i: (0, 0))
    return pl.pallas_call(
        functools.partial(_convmod_kernel, ts=ts, rs=rs),
        grid=(bsz, seq // ts),
        in_specs=_halo_specs(ts, seq, CONV_CH) + [
            pl.BlockSpec((1, ts, CONV_CH), lambda b, i: (b, i, 0)),
            pl.BlockSpec((CONV_WIDTH, SUBLANES, CONV_CH), lambda b, i: (0, 0, 0)), vec, vec, vec],
        out_specs=pl.BlockSpec((1, ts, CONV_CH), lambda b, i: (b, i, 0)),
        out_shape=jax.ShapeDtypeStruct((bsz, seq, CONV_CH), BF16),
        scratch_shapes=[pltpu.VMEM((ts + 2 * HALO, CONV_CH), F32),
                        pltpu.VMEM((SUBLANES - 1, ts + 2 * HALO, CONV_CH), F32)],
        compiler_params=_cparams(("parallel", "parallel")),
        name="convmod",
    )(apre, apre, apre, ga, cw, cb, lg, lb)


def _swa_kernel(sink_ref, q_ref, kp_ref, kc_ref, kn_ref, vp_ref, vc_ref, vn_ref, gb_ref, o_ref, *, nblk):
    n = pl.program_id(1)
    last = pl.num_programs(1) - 1
    kcat = jnp.concatenate([kp_ref[0], kc_ref[0], kn_ref[0]], axis=0)
    vcat = jnp.concatenate([vp_ref[0], vc_ref[0], vn_ref[0]], axis=0)
    vaug = jnp.concatenate([vcat, jnp.ones_like(vcat)], axis=1)
    rows = lax.broadcasted_iota(jnp.int32, (2 * BLOCK, 3 * BLOCK), 0)
    qi = jnp.where(rows >= BLOCK, rows - BLOCK, rows)
    ci = lax.broadcasted_iota(jnp.int32, (2 * BLOCK, 3 * BLOCK), 1)
    rel = ci - BLOCK - qi
    band = (rel >= -BLOCK) & (rel <= BLOCK)
    first_key = jnp.where(n > 0, 0, BLOCK)
    end_key = jnp.where(n < last, 3 * BLOCK, 2 * BLOCK)
    lane = lax.broadcasted_iota(jnp.int32, (BLOCK, LANES), 1)
    low = lane < HEAD_DIM
    row1 = lax.broadcasted_iota(jnp.int32, (2 * BLOCK, 1), 0)
    for b in range(nblk):
        mask = band
        if b == 0:
            mask = mask & (ci >= first_key)
        if b == nblk - 1:
            mask = mask & (ci < end_key)
        kwin = kcat[b * BLOCK:(b + 3) * BLOCK]
        vwin = vaug[b * BLOCK:(b + 3) * BLOCK]
        r0 = b * BLOCK
        for j in range(SWA_HEADS // 2):
            cols = slice(j * LANES, (j + 1) * LANES)
            qv = q_ref[0, r0:r0 + BLOCK, cols]
            zero = jnp.zeros_like(qv)
            qq = jnp.concatenate([jnp.where(low, qv, zero), jnp.where(low, zero, qv)], axis=0)
            s = lax.dot_general(qq, kwin, (((1,), (1,)), ((), ())), preferred_element_type=F32)
            s = jnp.where(mask, s, NEG_BIG)
            sink = jnp.where(row1 < BLOCK, sink_ref[j], sink_ref[j + SWA_HEADS // 2]) * LOG2E
            m = jnp.maximum(jnp.max(s, axis=-1, keepdims=True), sink)
            pv = jnp.dot(jnp.exp2(s - m).astype(BF16), vwin, preferred_element_type=F32)
            pv = pv[:, 0:LANES] / (pv[:, LANES:2 * LANES] + jnp.exp2(sink - m))
            o = jnp.where(low, pv[:BLOCK], pv[BLOCK:])
            o_ref[0, r0:r0 + BLOCK, cols] = (o * gb_ref[0, r0:r0 + BLOCK, cols].astype(F32)).astype(BF16)


def _swa(sink, q, k, v, gb, nblk=4):
    bsz, seq, _ = q.shape
    rows = nblk * BLOCK
    nb = seq // BLOCK
    cur = lambda b, n: (b, n, 0)
    prv = lambda b, n: (b, jnp.maximum(n * nblk - 1, 0), 0)
    nxt = lambda b, n: (b, jnp.minimum((n + 1) * nblk, nb - 1), 0)
    edge = lambda im: pl.BlockSpec((1, BLOCK, LANES), im)
    mid = pl.BlockSpec((1, rows, LANES), cur)
    wide = pl.BlockSpec((1, rows, 512), cur)
    return pl.pallas_call(
        functools.partial(_swa_kernel, nblk=nblk),
        grid=(bsz, seq // rows),
        in_specs=[pl.BlockSpec(memory_space=pltpu.SMEM), wide,
                  edge(prv), mid, edge(nxt), edge(prv), mid, edge(nxt), wide],
        out_specs=wide,
        out_shape=jax.ShapeDtypeStruct((bsz, seq, 512), BF16),
        compiler_params=_cparams(("parallel", "parallel")),
        name="swa",
    )(sink, q, k, k, k, v, v, v, gb)


def _mid_kernel(x_ref, a_ref, o_ref, wo_ref, g_ref, w_ref, cos_ref, sa_ref, sb_ref,
                x1_ref, q_ref, k_ref, v_ref, cg_ref, zg_ref, xbc_ref, dt_ref):
    x1 = (x_ref[...] + jnp.dot(a_ref[...], wo_ref[0:512, :], preferred_element_type=F32)
          + jnp.dot(o_ref[...], wo_ref[512:1024, :], preferred_element_type=F32))
    x1_ref[...] = x1
    hb = _rmsnorm(x1, g_ref[...]).astype(BF16)

    def mm(a, b):
        return jnp.dot(hb, w_ref[:, a:b], preferred_element_type=F32)

    cos, sa, sb = cos_ref[...], sa_ref[...], sb_ref[...]
    q_ref[...] = (_rope_cols(mm(0, 512), cos, sa, sb) * (HEAD_DIM ** -0.5 * LOG2E)).astype(BF16)
    k_ref[...] = _rope_cols(mm(512, 1024), cos, sa, sb).astype(BF16)
    v = mm(1024, 1536).astype(BF16)
    ones = jnp.ones((v.shape[0], LANES), BF16)
    for h in range(DIFF_HEADS):
        v_ref[:, 2 * h * LANES:(2 * h + 1) * LANES] = v[:, h * LANES:(h + 1) * LANES]
        v_ref[:, (2 * h + 1) * LANES:(2 * h + 2) * LANES] = ones
    cg_ref[...] = _silu(mm(1536, 2048)).astype(BF16)
    zg_ref[...] = _silu(mm(2048, 2560)).astype(BF16)
    xbc_ref[...] = mm(2560, 3584).astype(BF16)
    dt_ref[...] = mm(3584, IN_ODD_PAD)


def _mid(x2, a2, o2, wo, g, w, cos, sa, sb, seq, tm):
    t = x2.shape[0]
    ns = seq // tm
    row = lambda i: (i, 0)
    pos = lambda i: (i % ns, 0)
    full = lambda i: (0, 0)
    outs = ((D_MODEL, F32), (512, BF16), (512, BF16), (1024, BF16), (512, BF16), (512, BF16),
            (SSM_CONV_DIM, BF16), (LANES, F32))
    return pl.pallas_call(
        _mid_kernel,
        grid=(t // tm,),
        in_specs=[pl.BlockSpec((tm, D_MODEL), row), pl.BlockSpec((tm, 512), row), pl.BlockSpec((tm, 512), row),
                  pl.BlockSpec((D_MODEL, D_MODEL), full), pl.BlockSpec((1, D_MODEL), full),
                  pl.BlockSpec((D_MODEL, IN_ODD_PAD), full),
                  pl.BlockSpec((tm, LANES), pos), pl.BlockSpec((tm, LANES), pos), pl.BlockSpec((tm, LANES), pos)],
        out_specs=[pl.BlockSpec((tm, wd), row) for wd, _ in outs],
        out_shape=[jax.ShapeDtypeStruct((t, wd), dt) for wd, dt in outs],
        compiler_params=_cparams(("parallel",)),
        name="outproj0_inproj1",
    )(x2, a2, o2, wo, g, w, cos, sa, sb)


def _diff_kernel(lq1_ref, lk1_ref, lq2_ref, lk2_ref, g_ref, q_ref, qn_ref, k_ref, v_ref, cg_ref, o_ref,
                 s0_ref, s1_ref, mx0_ref, mx1_ref, m_ref, acc_ref, *, kc, unroll, lam_init):
    tq = q_ref.shape[1]
    nch = k_ref.shape[1] // kc
    lane = lax.broadcasted_iota(jnp.int32, (tq, LANES), 1)

    def stack_maps(q):
        zero = jnp.zeros_like(q)
        return jnp.concatenate([jnp.where(lane < HEAD_DIM, q, zero), jnp.where(lane < HEAD_DIM, zero, q)], axis=0)

    qq = stack_maps(q_ref[0])
    qq_next = stack_maps(qn_ref[0])
    slots = ((s0_ref, mx0_ref), (s1_ref, mx1_ref))

    def scores(qs, c, slot):
        s_ref, mx_ref = slots[slot]
        kch = k_ref[0, pl.ds(pl.multiple_of(c * kc, kc), kc), :]
        s = lax.dot_general(qs, kch, (((1,), (1,)), ((), ())), preferred_element_type=F32)
        s_ref[...] = s
        mx_ref[...] = jnp.broadcast_to(jnp.max(s, axis=-1, keepdims=True), mx_ref.shape)

    def update(c, slot):
        s_ref, mx_ref = slots[slot]
        vch = v_ref[0, pl.ds(pl.multiple_of(c * kc, kc), kc), :]
        m_old = m_ref[...]
        m_new = jnp.maximum(m_old, mx_ref[...])
        alpha = jnp.exp2(m_old - m_new)
        p = jnp.exp2(s_ref[...] - jnp.concatenate([m_new] * (kc // LANES), axis=1)).astype(BF16)
        acc_ref[...] = acc_ref[...] * jnp.concatenate([alpha, alpha], axis=1) + jnp.dot(
            p, vch, preferred_element_type=F32)
        m_ref[...] = m_new

    m_ref[...] = jnp.full(m_ref.shape, NEG_BIG, F32)
    acc_ref[...] = jnp.zeros(acc_ref.shape, F32)

    @pl.when(pl.program_id(2) == 0)
    def _():
        scores(qq, 0, 0)

    ntrips = nch // unroll

    def body(t, carry):
        for u in range(unroll):
            c = unroll * t + u
            if u < unroll - 1:
                scores(qq, c + 1, (u + 1) % 2)
            elif ntrips == 1:
                scores(qq_next, 0, 0)
            else:
                wrap = t == ntrips - 1
                scores(jnp.where(wrap, qq_next, qq), jnp.where(wrap, 0, c + 1), 0)
            update(c, u % 2)
        return carry

    if ntrips == 1:
        body(0, 0)
    else:
        lax.fori_loop(0, ntrips, body, 0)

    lam = (jnp.exp(jnp.sum(lq1_ref[...] * lk1_ref[...], axis=-1, keepdims=True))
           - jnp.exp(jnp.sum(lq2_ref[...] * lk2_ref[...], axis=-1, keepdims=True)) + lam_init)
    o0 = acc_ref[0:tq, 0:LANES] / acc_ref[0:tq, LANES:2 * LANES]
    o1 = acc_ref[tq:2 * tq, 0:LANES] / acc_ref[tq:2 * tq, LANES:2 * LANES]
    o = o0 - lam * o1
    o = o * lax.rsqrt(jnp.mean(o * o, axis=-1, keepdims=True) + EPS) * g_ref[...] * (1.0 - lam_init)
    o_ref[0] = (o * cg_ref[0].astype(F32)).astype(BF16)


def _diff_attn(lq1, lk1, lq2, lk2, g, q, k, v, cg, lam_init, tq=256):
    bsz, seq, _ = q.shape
    kc = next(c for c in (1024, 512, 256) if seq % (2 * c) == 0)
    unroll = 4 if seq % (4 * kc) == 0 else 2
    nq = seq // tq
    small = pl.BlockSpec((1, HEAD_DIM), lambda b, h, i: (0, 0))
    tile = pl.BlockSpec((1, tq, LANES), lambda b, h, i: (b, i, h))
    next_tile = pl.BlockSpec((1, tq, LANES), lambda b, h, i: (b, jnp.minimum(i + 1, nq - 1), h))
    return pl.pallas_call(
        functools.partial(_diff_kernel, kc=kc, unroll=unroll, lam_init=lam_init),
        grid=(bsz, DIFF_HEADS, nq),
        in_specs=[small, small, small, small, pl.BlockSpec((1, LANES), lambda b, h, i: (0, 0)), tile, next_tile,
                  pl.BlockSpec((1, seq, LANES), lambda b, h, i: (b, 0, h)),
                  pl.BlockSpec((1, seq, 2 * LANES), lambda b, h, i: (b, 0, h)), tile],
        out_specs=tile,
        out_shape=jax.ShapeDtypeStruct((bsz, seq, 512), BF16),
        scratch_shapes=[pltpu.VMEM((2 * tq, kc), F32), pltpu.VMEM((2 * tq, kc), F32),
                        pltpu.VMEM((2 * tq, LANES), F32), pltpu.VMEM((2 * tq, LANES), F32),
                        pltpu.VMEM((2 * tq, LANES), F32), pltpu.VMEM((2 * tq, 2 * LANES), F32)],
        compiler_params=_cparams(("parallel", "parallel", "arbitrary")),
        name="diff_attn",
    )(lq1, lk1, lq2, lk2, g, q, q, k, v, cg)


def _split3(x):
    hi = x.astype(BF16)
    r = x - hi.astype(F32)
    mid = r.astype(BF16)
    lo = (r - mid.astype(F32)).astype(BF16)
    return hi, mid, lo


def _ssd_fwd_kernel(xc_ref, xp_ref, xn_ref, dt_ref, cw_ref, cb_ref, dtb_ref, a_ref, dsk_ref, y_ref, xconv_ref,
                    carry_ref, xf_ref):
    L = BLOCK
    nchunk = xc_ref.shape[1] // L
    c = pl.program_id(1)
    last = pl.num_programs(1) - 1

    @pl.when(c == 0)
    def _():
        carry_ref[...] = jnp.zeros(carry_ref.shape, F32)

    xp, xn = xp_ref[0], xn_ref[0]
    halo_p = jnp.where(c > 0, xp, jnp.zeros_like(xp))
    halo_n = jnp.where(c < last, xn, jnp.zeros_like(xn))
    r_i = lax.broadcasted_iota(jnp.int32, (L, L + 2 * HALO), 0)
    c_i = lax.broadcasted_iota(jnp.int32, (L, L + 2 * HALO), 1)
    first = HALO - (SSM_CONV - 1) // 2
    shifts = [None if first + w == HALO else jnp.where(c_i == r_i + first + w, 1.0, 0.0).astype(BF16)
              for w in range(SSM_CONV)]
    wd = 2 * LANES
    for nb in range(SSM_CONV_DIM // wd):
        cols = slice(nb * wd, (nb + 1) * wd)
        padded = jnp.concatenate([halo_p[:, cols], xc_ref[0, :, cols], halo_n[:, cols]], axis=0)
        for ci in range(nchunk):
            r0 = ci * L
            window = padded[r0:r0 + L + 2 * HALO]
            acc = jnp.broadcast_to(cb_ref[:, cols], (L, wd))
            for w in range(SSM_CONV):
                tap = (xc_ref[0, r0:r0 + L, cols].astype(F32) if shifts[w] is None
                       else jnp.dot(shifts[w], window, preferred_element_type=F32))
                acc = acc + tap * cw_ref[w:w + 1, cols]
            xbc = _silu(acc)
            xconv_ref[0, r0:r0 + L, cols] = xbc.astype(BF16)
            xf_ref[r0:r0 + L, cols] = xbc
    for ci in range(nchunk):
        r0 = ci * L
        _ssd_scan_chunk(lambda a, b: xf_ref[r0:r0 + L, a:b], r0, dt_ref, dtb_ref, a_ref, dsk_ref, y_ref, carry_ref,
                        reverse=False, col0=0)


def _ssd_bwd_kernel(xconv_ref, dt_ref, dtb_ref, a_ref, y_ref, carry_ref):
    L = BLOCK

    @pl.when(pl.program_id(1) == 0)
    def _():
        carry_ref[...] = jnp.zeros(carry_ref.shape, F32)

    for ci in reversed(range(xconv_ref.shape[1] // L)):
        r0 = ci * L
        _ssd_scan_chunk(lambda a, b: xconv_ref[0, r0:r0 + L, a:b].astype(F32), r0, dt_ref, dtb_ref, a_ref, None,
                        y_ref, carry_ref, reverse=True, col0=SSM_HEADS)


def _ssd_scan_chunk(xcols, r0, dt_ref, dtb_ref, a_ref, dsk_ref, y_ref, carry_ref, *, reverse, col0):
    L = BLOCK
    bm = [xcols(SSM_INNER + g * SSM_STATE, SSM_INNER + (g + 1) * SSM_STATE) for g in range(2)]
    cm = [xcols(SSM_INNER + (2 + g) * SSM_STATE, SSM_INNER + (3 + g) * SSM_STATE) for g in range(2)]

    z = dt_ref[0, r0:r0 + L, :] + dtb_ref[...]
    dt = jnp.maximum(z, 0.0) + jnp.log(1.0 + jnp.exp(-jnp.abs(z)))
    a = dt * a_ref[...]
    r_i = lax.broadcasted_iota(jnp.int32, (L, L), 0)
    c_i = lax.broadcasted_iota(jnp.int32, (L, L), 1)
    causal = (r_i <= c_i) if reverse else (r_i >= c_i)
    tri = jnp.where(causal, 1.0, 0.0).astype(BF16)
    acum = sum(jnp.dot(tri, piece, preferred_element_type=F32) for piece in _split3(a))
    acum_t = acum.T
    edge = 0 if reverse else L - 1
    a_end = acum[edge:edge + 1, :]

    lane = lax.broadcasted_iota(jnp.int32, (L, LANES), 1)
    low = lane < HEAD_DIM
    lane1 = lax.broadcasted_iota(jnp.int32, (1, LANES), 1)

    cbs = [lax.dot_general(cm[g].astype(BF16), bm[g].astype(BF16), (((1,), (1,)), ((), ())),
                           preferred_element_type=F32) for g in range(2)]
    bts = [bm[g].T.astype(BF16) for g in range(2)]
    cms = [cm[g].astype(BF16) for g in range(2)]
    for j in range(SSM_HEADS // 2):
        g = j // 2
        ca = col0 + 2 * j
        acb = [jnp.broadcast_to(acum[:, ca + hh:ca + hh + 1], (L, LANES)) for hh in range(2)]
        ac_pair = jnp.where(low, acb[0], acb[1])
        dt_pair = jnp.where(low, dt[:, ca:ca + 1], dt[:, ca + 1:ca + 2])
        end_pair = jnp.where(lane1 < HEAD_DIM, a_end[:, ca:ca + 1], a_end[:, ca + 1:ca + 2])
        xs_j = xcols(j * LANES, (j + 1) * LANES)
        xdt = xs_j * dt_pair
        xdt_b = xdt.astype(BF16)
        zero = jnp.zeros_like(xdt_b)
        ms = []
        for hh in range(2):
            seg = acb[hh] - acum_t[ca + hh:ca + hh + 1, :]
            dec = jnp.where(causal, jnp.exp(jnp.where(causal, seg, 0.0)), 0.0)
            ms.append((cbs[g] * dec).astype(BF16))
        lhs = jnp.concatenate(ms, axis=1)
        rhs = jnp.concatenate([jnp.where(low, xdt_b, zero), jnp.where(low, zero, xdt_b)], axis=0)
        y = jnp.dot(lhs, rhs, preferred_element_type=F32)
        prev = carry_ref[j]
        y = y + jnp.dot(cms[g], prev.astype(BF16), preferred_element_type=F32) * jnp.exp(ac_pair)
        st = jnp.dot(bts[g], (xdt * jnp.exp(end_pair - ac_pair)).astype(BF16), preferred_element_type=F32)
        carry_ref[j] = prev * jnp.exp(end_pair) + st
        if dsk_ref is not None:
            y = y + xs_j * dsk_ref[:, j * LANES:(j + 1) * LANES]
        y_ref[0, r0:r0 + L, j * LANES:(j + 1) * LANES] = y


_SSD_CARRY = pltpu.VMEM((SSM_HEADS // 2, SSM_STATE, LANES), F32)
_SSD_ROWS = 4 * BLOCK


def _ssd_fwd(xbc, dt, cw, cb, dtb, a, dsk):
    bsz, seq, _ = xbc.shape
    vec = lambda wd: pl.BlockSpec((1, wd), lambda b, c: (0, 0))
    chunk = lambda wd: pl.BlockSpec((1, _SSD_ROWS, wd), lambda b, c: (b, c, 0))
    return pl.pallas_call(
        _ssd_fwd_kernel,
        grid=(bsz, seq // _SSD_ROWS),
        in_specs=_halo_specs(_SSD_ROWS, seq, SSM_CONV_DIM) + [
            chunk(LANES), pl.BlockSpec((SSM_CONV, SSM_CONV_DIM), lambda b, c: (0, 0)), vec(SSM_CONV_DIM),
            vec(LANES), vec(LANES), vec(SSM_INNER)],
        out_specs=[chunk(SSM_INNER), chunk(SSM_CONV_DIM)],
        out_shape=[jax.ShapeDtypeStruct((bsz, seq, SSM_INNER), F32),
                   jax.ShapeDtypeStruct((bsz, seq, SSM_CONV_DIM), BF16)],
        scratch_shapes=[_SSD_CARRY, pltpu.VMEM((_SSD_ROWS, SSM_CONV_DIM), F32)],
        compiler_params=_cparams(("parallel", "arbitrary")),
        name="ssd_fwd",
    )(xbc, xbc, xbc, dt, cw, cb, dtb, a, dsk)


def _ssd_bwd(xconv, dt, dtb, a):
    bsz, seq, _ = xconv.shape
    nc = seq // _SSD_ROWS
    vec = lambda wd: pl.BlockSpec((1, wd), lambda b, c: (0, 0))
    chunk = lambda wd: pl.BlockSpec((1, _SSD_ROWS, wd), lambda b, c: (b, nc - 1 - c, 0))
    return pl.pallas_call(
        _ssd_bwd_kernel,
        grid=(bsz, nc),
        in_specs=[chunk(SSM_CONV_DIM), chunk(LANES), vec(LANES), vec(LANES)],
        out_specs=chunk(SSM_INNER),
        out_shape=jax.ShapeDtypeStruct((bsz, seq, SSM_INNER), F32),
        scratch_shapes=[_SSD_CARRY],
        compiler_params=_cparams(("parallel", "arbitrary")),
        name="ssd_bwd",
    )(xconv, dt, dtb, a)


def _final_kernel(x1_ref, c_ref, yf_ref, yb_ref, zg_ref, sg_ref, wo_ref, fg_ref, o_ref, *, parts):
    rows = x1_ref.shape[0] // parts
    for part in range(parts):
        r = slice(part * rows, (part + 1) * rows)
        y = (yf_ref[r, :] + yb_ref[r, :]) * zg_ref[r, :].astype(F32)
        d = _rmsnorm(y, sg_ref[...]).astype(BF16)
        x2 = (x1_ref[r, :] + jnp.dot(c_ref[r, :], wo_ref[0:512, :], preferred_element_type=F32)
              + jnp.dot(d, wo_ref[512:1024, :], preferred_element_type=F32))
        o_ref[r, :] = _rmsnorm(x2, fg_ref[...])


def _final(x1, c2, yf, yb, zg, sg, wo, fg, tm):
    t = x1.shape[0]
    row = lambda i: (i, 0)
    full = lambda i: (0, 0)
    half = pl.BlockSpec((tm, 512), row)
    return pl.pallas_call(
        functools.partial(_final_kernel, parts=2),
        grid=(t // tm,),
        in_specs=[pl.BlockSpec((tm, D_MODEL), row), half, half, half, half,
                  pl.BlockSpec((1, 512), full), pl.BlockSpec((D_MODEL, D_MODEL), full),
                  pl.BlockSpec((1, D_MODEL), full)],
        out_specs=pl.BlockSpec((tm, D_MODEL), row),
        out_shape=jax.ShapeDtypeStruct((t, D_MODEL), F32),
        compiler_params=_cparams(("parallel",)),
        name="outproj1_final",
    )(x1, c2, yf, yb, zg, sg, wo, fg)


def _rope_tables(seq):
    inv = 1.0 / (ROPE_THETA ** (jnp.arange(0, HEAD_DIM, 2, dtype=F32) / HEAD_DIM))
    f = jnp.arange(seq, dtype=F32)[:, None] * inv[None, :]
    emb = jnp.concatenate([f, f, f, f], axis=-1)
    cos, sin = jnp.cos(emb), jnp.sin(emb)
    first_half = (jnp.arange(LANES) % HEAD_DIM) < HEAD_DIM // 2
    return cos, jnp.where(first_half, -sin, 0.0), jnp.where(first_half, 0.0, sin)


def _pair_perm():
    idx = []
    for j in range(SWA_HEADS // 2):
        idx += list(range(j * HEAD_DIM, (j + 1) * HEAD_DIM))
        idx += list(range((j + 4) * HEAD_DIM, (j + 5) * HEAD_DIM))
    return jnp.asarray(idx, jnp.int32)


def _pad_lanes(v, offset):
    return jnp.zeros((1, LANES), F32).at[0, offset:offset + v.shape[0]].set(v.astype(F32))


def _prepare(norm_g, w_in0, conv_w, conv_b, conv_ln_g, conv_ln_b, sink, w_out0, w_in1, lambda_q1, lambda_k1,
             lambda_q2, lambda_k2, diff_norm_g, ssm_conv_w, ssm_conv_b, dt_bias_f, dt_bias_b, a_log_f, a_log_b,
             d_skip, ssm_norm_g, w_out1, final_norm_g):
    perm = _pair_perm()
    w0 = w_in0[0]
    w0 = w0.at[:, 1536:2048].set(w0[:, 1536:2048][:, perm]).at[:, 2304:2816].set(w0[:, 2304:2816][:, perm])
    wo0 = w_out0[0].at[512:1024, :].set(w_out0[0][512:1024, :][perm, :])
    w1 = jnp.pad(w_in1[0], ((0, 0), (0, IN_ODD_PAD - IN_ODD)))
    row = lambda v: v.astype(F32).reshape(1, -1)
    return dict(
        g0=row(norm_g[0]), g1=row(norm_g[1]), w0=w0.astype(BF16), wo0=wo0.astype(BF16), w1=w1.astype(BF16),
        wo1=w_out1[0].astype(BF16), cb=row(conv_b[0]), lg=row(conv_ln_g[0]),
        cw=jnp.broadcast_to(conv_w[0].astype(F32)[:, None, :], (CONV_WIDTH, SUBLANES, CONV_CH)),
        lb=row(conv_ln_b[0]), sink=sink[0].astype(F32),
        lq1=row(lambda_q1[0]), lk1=row(lambda_k1[0]), lq2=row(lambda_q2[0]), lk2=row(lambda_k2[0]),
        dg=row(diff_norm_g[0]), scw=ssm_conv_w[0].astype(F32), scb=row(ssm_conv_b[0]),
        dtb_f=_pad_lanes(dt_bias_f[0], 0), dtb_b=_pad_lanes(dt_bias_b[0], SSM_HEADS),
        a_f=_pad_lanes(-jnp.exp(a_log_f[0].astype(F32)), 0), a_b=_pad_lanes(-jnp.exp(a_log_b[0].astype(F32)), SSM_HEADS),
        dsk=row(jnp.repeat(d_skip[0].astype(F32), HEAD_DIM)), sg=row(ssm_norm_g[0]), fg=row(final_norm_g))


def _trunk(x, p, tm=512):
    bsz, seq, _ = x.shape
    assert seq % tm == 0 and seq % 512 == 0
    t = bsz * seq
    lam_init = 0.8 - 0.6 * math.exp(-0.3 * 1)
    cos, sa, sb = _rope_tables(seq)
    x2 = x.reshape(t, D_MODEL)
    apre, ga, q, k, v, gb = _inproj0(x2, p["g0"], p["w0"], cos, sa, sb, seq, tm)
    r3 = lambda arr: arr.reshape(bsz, seq, arr.shape[-1])
    a = _convmod(r3(apre), r3(ga), p["cw"], p["cb"], p["lg"], p["lb"])
    o = _swa(p["sink"], r3(q), r3(k), r3(v), r3(gb))
    x1, q1, k1, v1, cg, zg, xbc, dt = _mid(x2, a.reshape(t, 512), o.reshape(t, 512), p["wo0"], p["g1"], p["w1"],
                                           cos, sa, sb, seq, tm)
    c = _diff_attn(p["lq1"], p["lk1"], p["lq2"], p["lk2"], p["dg"], r3(q1), r3(k1), r3(v1), r3(cg), lam_init)
    yf, xconv = _ssd_fwd(r3(xbc), r3(dt), p["scw"], p["scb"], p["dtb_f"], p["a_f"], p["dsk"])
    yb = _ssd_bwd(xconv, r3(dt), p["dtb_b"], p["a_b"])
    out = _final(x1, c.reshape(t, 512), yf.reshape(t, 512), yb.reshape(t, 512), zg, p["sg"], p["wo1"], p["fg"], tm)
    return out.reshape(bsz, seq, D_MODEL)


def kernel(x_prompt, x_sample, norm_g, w_in0, conv_w, conv_b, conv_ln_g, conv_ln_b, sink, w_out0, w_in1, lambda_q1, lambda_k1, lambda_q2, lambda_k2, diff_norm_g, ssm_conv_w, ssm_conv_b, dt_bias_f, dt_bias_b, a_log_f, a_log_b, d_skip, ssm_norm_g, w_out1, final_norm_g):
    p = _prepare(norm_g, w_in0, conv_w, conv_b, conv_ln_g, conv_ln_b, sink, w_out0, w_in1, lambda_q1, lambda_k1,
                 lambda_q2, lambda_k2, diff_norm_g, ssm_conv_w, ssm_conv_b, dt_bias_f, dt_bias_b, a_log_f, a_log_b,
                 d_skip, ssm_norm_g, w_out1, final_norm_g)
    return (_trunk(x_prompt, p), _trunk(x_sample, p))
```

```python
import functools
import math

import jax
import jax.numpy as jnp
from jax import lax
from jax.experimental import pallas as pl
from jax.experimental.pallas import tpu as pltpu

F32 = jnp.float32
BF16 = jnp.bfloat16

D_MODEL = 1024
EPS = 1e-6
ROPE_THETA = 10000.0
HEAD_DIM = 64
LANES = 128
SUBLANES = 8
HALO = 16
CONV_CH = 512
CONV_WIDTH = 31
SWA_HEADS = 8
BLOCK = 128
DIFF_HEADS = 4
SSM_INNER = 512
SSM_HEADS = 8
SSM_STATE = 128
SSM_CONV = 5
SSM_CONV_DIM = 1024
IN_EVEN = 2816
IN_ODD = 3600
IN_ODD_PAD = 3712
NEG_BIG = -1e30
LOG2E = math.log2(math.e)
VMEM_LIMIT = 56 * 1024 * 1024


def _cparams(sem):
    return pltpu.CompilerParams(dimension_semantics=sem, vmem_limit_bytes=VMEM_LIMIT)


def _sigmoid(x):
    return 1.0 / (1.0 + jnp.exp2(x * (-LOG2E)))


def _silu(x):
    return x * _sigmoid(x)


def _rmsnorm(x, g):
    return x * lax.rsqrt(jnp.mean(x * x, axis=-1, keepdims=True) + EPS) * g


def _rope_cols(x, cos, sa, sb):
    outs = []
    for j in range(x.shape[1] // LANES):
        xj = x[:, j * LANES:(j + 1) * LANES]
        outs.append(xj * cos + pltpu.roll(xj, LANES - 32, 1) * sa + pltpu.roll(xj, 32, 1) * sb)
    return outs[0] if len(outs) == 1 else jnp.concatenate(outs, axis=1)


def _inproj0_kernel(x_ref, g_ref, w_ref, cos_ref, sa_ref, sb_ref,
                    apre_ref, ga_ref, q_ref, k_ref, v_ref, gb_ref):
    hb = _rmsnorm(x_ref[...], g_ref[...]).astype(BF16)

    def mm(a, b):
        return jnp.dot(hb, w_ref[:, a:b], preferred_element_type=F32)

    cos, sa, sb = cos_ref[...], sa_ref[...], sb_ref[...]
    apre_ref[...] = (mm(0, 512) * _sigmoid(mm(512, 1024))).astype(BF16)
    ga_ref[...] = _silu(mm(1024, 1536)).astype(BF16)
    gb_ref[...] = _silu(mm(2304, 2816)).astype(BF16)
    q_ref[...] = (_rope_cols(mm(1536, 2048), cos, sa, sb) * (HEAD_DIM ** -0.5 * LOG2E)).astype(BF16)
    k_ref[...] = _rope_cols(mm(2048, 2176), cos, sa, sb).astype(BF16)
    v_ref[...] = mm(2176, 2304).astype(BF16)


def _inproj0(x2, g, w, cos, sa, sb, seq, tm):
    t = x2.shape[0]
    ns = seq // tm
    row = lambda i: (i, 0)
    pos = lambda i: (i % ns, 0)
    full = lambda i: (0, 0)
    widths = (512, 512, 512, 128, 128, 512)
    return pl.pallas_call(
        _inproj0_kernel,
        grid=(t // tm,),
        in_specs=[pl.BlockSpec((tm, D_MODEL), row), pl.BlockSpec((1, D_MODEL), full),
                  pl.BlockSpec((D_MODEL, IN_EVEN), full),
                  pl.BlockSpec((tm, LANES), pos), pl.BlockSpec((tm, LANES), pos), pl.BlockSpec((tm, LANES), pos)],
        out_specs=[pl.BlockSpec((tm, wd), row) for wd in widths],
        out_shape=[jax.ShapeDtypeStruct((t, wd), BF16) for wd in widths],
        compiler_params=_cparams(("parallel",)),
        name="inproj0",
    )(x2, g, w, cos, sa, sb)


def _convmod_kernel(xc_ref, xp_ref, xn_ref, ga_ref, cw_ref, cb_ref, lg_ref, lb_ref, o_ref, rol_ref, *, ts, rs):
    i = pl.program_id(1)
    last = pl.num_programs(1) - 1
    xp, xn = xp_ref[0], xn_ref[0]
    padded = jnp.concatenate([jnp.where(i > 0, xp, jnp.zeros_like(xp)), xc_ref[0],
                              jnp.where(i < last, xn, jnp.zeros_like(xn))], axis=0)
    rows = ts + 2 * HALO
    rol_ref[0] = padded.astype(F32)
    for rb in range(0, rows, BLOCK):
        nr = min(BLOCK, rows - rb)
        nk = min(BLOCK + HALO, rows - rb)
        r_i = lax.broadcasted_iota(jnp.int32, (nr, nk), 0)
        c_i = lax.broadcasted_iota(jnp.int32, (nr, nk), 1)
        window = padded[rb:rb + nk]
        for j in range(1, SUBLANES):
            shift = jnp.where(c_i == r_i + j, 1.0, 0.0).astype(BF16)
            rol_ref[j, rb:rb + nr] = jnp.dot(shift, window, preferred_element_type=F32)
    first = HALO - (CONV_WIDTH - 1) // 2
    for r0 in range(0, ts, rs):
        acc = jnp.broadcast_to(cb_ref[...], (rs, CONV_CH))
        for w in range(CONV_WIDTH):
            a, j = divmod(first + w, SUBLANES)
            lo = r0 + a * SUBLANES
            acc = acc + rol_ref[j, lo:lo + rs, :] * jnp.concatenate([cw_ref[w]] * (rs // SUBLANES), axis=0)
        mu = jnp.mean(acc, axis=-1, keepdims=True)
        xc = acc - mu
        var = jnp.mean(xc * xc, axis=-1, keepdims=True)
        y = xc * lax.rsqrt(var + EPS) * lg_ref[...] + lb_ref[...]
        o_ref[0, r0:r0 + rs, :] = (_silu(y) * ga_ref[0, r0:r0 + rs, :].astype(F32)).astype(BF16)


def _halo_specs(ts, seq, width):
    hb = ts // HALO
    nh = seq // HALO
    return [pl.BlockSpec((1, ts, width), lambda b, i: (b, i, 0)),
            pl.BlockSpec((1, HALO, width), lambda b, i: (b, jnp.maximum(i * hb - 1, 0), 0)),
            pl.BlockSpec((1, HALO, width), lambda b, i: (b, jnp.minimum((i + 1) * hb, nh - 1), 0))]


def _convmod(apre, ga, cw, cb, lg, lb, ts=512, rs=32):
    bsz, seq, _ = apre.shape
    vec = pl.BlockSpec((1, CONV_CH), lambda b, i: (0, 0))
    return pl.pallas_call(
        functools.partial(_convmod_kernel, ts=ts, rs=rs),
        grid=(bsz, seq // ts),
        in_specs=_halo_specs(ts, seq, CONV_CH) + [
            pl.BlockSpec((1, ts, CONV_CH), lambda b, i: (b, i, 0)),
            pl.BlockSpec((CONV_WIDTH, SUBLANES, CONV_CH), lambda b, i: (0, 0, 0)), vec, vec, vec],
        out_specs=pl.BlockSpec((1, ts, CONV_CH), lambda b, i: (b, i, 0)),
        out_shape=jax.ShapeDtypeStruct((bsz, seq, CONV_CH), BF16),
        scratch_shapes=[pltpu.VMEM((SUBLANES, ts + 2 * HALO, CONV_CH), F32)],
        compiler_params=_cparams(("parallel", "parallel")),
        name="convmod",
    )(apre, apre, apre, ga, cw, cb, lg, lb)


def _swa_kernel(sink_ref, q_ref, kp_ref, kc_ref, kn_ref, vp_ref, vc_ref, vn_ref, gb_ref, o_ref, *, nblk):
    n = pl.program_id(1)
    last = pl.num_programs(1) - 1
    kcat = jnp.concatenate([kp_ref[0], kc_ref[0], kn_ref[0]], axis=0)
    vcat = jnp.concatenate([vp_ref[0], vc_ref[0], vn_ref[0]], axis=0)
    vaug = jnp.concatenate([vcat, jnp.ones_like(vcat)], axis=1)
    rows = lax.broadcasted_iota(jnp.int32, (2 * BLOCK, 3 * BLOCK), 0)
    qi = jnp.where(rows >= BLOCK, rows - BLOCK, rows)
    ci = lax.broadcasted_iota(jnp.int32, (2 * BLOCK, 3 * BLOCK), 1)
    rel = ci - BLOCK - qi
    band = (rel >= -BLOCK) & (rel <= BLOCK)
    first_key = jnp.where(n > 0, 0, BLOCK)
    end_key = jnp.where(n < last, 3 * BLOCK, 2 * BLOCK)
    lane = lax.broadcasted_iota(jnp.int32, (BLOCK, LANES), 1)
    low = lane < HEAD_DIM
    row1 = lax.broadcasted_iota(jnp.int32, (2 * BLOCK, 1), 0)
    for b in range(nblk):
        mask = band
        if b == 0:
            mask = mask & (ci >= first_key)
        if b == nblk - 1:
            mask = mask & (ci < end_key)
        kwin = kcat[b * BLOCK:(b + 3) * BLOCK]
        vwin = vaug[b * BLOCK:(b + 3) * BLOCK]
        r0 = b * BLOCK
        for j in range(SWA_HEADS // 2):
            cols = slice(j * LANES, (j + 1) * LANES)
            qv = q_ref[0, r0:r0 + BLOCK, cols]
            zero = jnp.zeros_like(qv)
            qq = jnp.concatenate([jnp.where(low, qv, zero), jnp.where(low, zero, qv)], axis=0)
            s = lax.dot_general(qq, kwin, (((1,), (1,)), ((), ())), preferred_element_type=F32)
            s = jnp.where(mask, s, NEG_BIG)
            sink = jnp.where(row1 < BLOCK, sink_ref[j], sink_ref[j + SWA_HEADS // 2]) * LOG2E
            m = jnp.maximum(jnp.max(s, axis=-1, keepdims=True), sink)
            pv = jnp.dot(jnp.exp2(s - m).astype(BF16), vwin, preferred_element_type=F32)
            pv = pv[:, 0:LANES] / (pv[:, LANES:2 * LANES] + jnp.exp2(sink - m))
            o = jnp.where(low, pv[:BLOCK], pv[BLOCK:])
            o_ref[0, r0:r0 + BLOCK, cols] = (o * gb_ref[0, r0:r0 + BLOCK, cols].astype(F32)).astype(BF16)


def _swa(sink, q, k, v, gb, nblk=4):
    bsz, seq, _ = q.shape
    rows = nblk * BLOCK
    nb = seq // BLOCK
    cur = lambda b, n: (b, n, 0)
    prv = lambda b, n: (b, jnp.maximum(n * nblk - 1, 0), 0)
    nxt = lambda b, n: (b, jnp.minimum((n + 1) * nblk, nb - 1), 0)
    edge = lambda im: pl.BlockSpec((1, BLOCK, LANES), im)
    mid = pl.BlockSpec((1, rows, LANES), cur)
    wide = pl.BlockSpec((1, rows, 512), cur)
    return pl.pallas_call(
        functools.partial(_swa_kernel, nblk=nblk),
        grid=(bsz, seq // rows),
        in_specs=[pl.BlockSpec(memory_space=pltpu.SMEM), wide,
                  edge(prv), mid, edge(nxt), edge(prv), mid, edge(nxt), wide],
        out_specs=wide,
        out_shape=jax.ShapeDtypeStruct((bsz, seq, 512), BF16),
        compiler_params=_cparams(("parallel", "parallel")),
        name="swa",
    )(sink, q, k, k, k, v, v, v, gb)


def _mid_kernel(x_ref, a_ref, o_ref, wo_ref, g_ref, w_ref, cos_ref, sa_ref, sb_ref,
                x1_ref, q_ref, k_ref, v_ref, cg_ref, zg_ref, xbc_ref, dt_ref):
    x1 = (x_ref[...] + jnp.dot(a_ref[...], wo_ref[0:512, :], preferred_element_type=F32)
          + jnp.dot(o_ref[...], wo_ref[512:1024, :], preferred_element_type=F32))
    x1_ref[...] = x1
    hb = _rmsnorm(x1, g_ref[...]).astype(BF16)

    def mm(a, b):
        return jnp.dot(hb, w_ref[:, a:b], preferred_element_type=F32)

    cos, sa, sb = cos_ref[...], sa_ref[...], sb_ref[...]
    q_ref[...] = (_rope_cols(mm(0, 512), cos, sa, sb) * (HEAD_DIM ** -0.5 * LOG2E)).astype(BF16)
    k_ref[...] = _rope_cols(mm(512, 1024), cos, sa, sb).astype(BF16)
    v = mm(1024, 1536).astype(BF16)
    ones = jnp.ones((v.shape[0], LANES), BF16)
    for h in range(DIFF_HEADS):
        v_ref[:, 2 * h * LANES:(2 * h + 1) * LANES] = v[:, h * LANES:(h + 1) * LANES]
        v_ref[:, (2 * h + 1) * LANES:(2 * h + 2) * LANES] = ones
    cg_ref[...] = _silu(mm(1536, 2048)).astype(BF16)
    zg_ref[...] = _silu(mm(2048, 2560)).astype(BF16)
    xbc_ref[...] = mm(2560, 3584).astype(BF16)
    dt_ref[...] = mm(3584, IN_ODD_PAD)


def _mid(x2, a2, o2, wo, g, w, cos, sa, sb, seq, tm):
    t = x2.shape[0]
    ns = seq // tm
    row = lambda i: (i, 0)
    pos = lambda i: (i % ns, 0)
    full = lambda i: (0, 0)
    outs = ((D_MODEL, F32), (512, BF16), (512, BF16), (1024, BF16), (512, BF16), (512, BF16),
            (SSM_CONV_DIM, BF16), (LANES, F32))
    return pl.pallas_call(
        _mid_kernel,
        grid=(t // tm,),
        in_specs=[pl.BlockSpec((tm, D_MODEL), row), pl.BlockSpec((tm, 512), row), pl.BlockSpec((tm, 512), row),
                  pl.BlockSpec((D_MODEL, D_MODEL), full), pl.BlockSpec((1, D_MODEL), full),
                  pl.BlockSpec((D_MODEL, IN_ODD_PAD), full),
                  pl.BlockSpec((tm, LANES), pos), pl.BlockSpec((tm, LANES), pos), pl.BlockSpec((tm, LANES), pos)],
        out_specs=[pl.BlockSpec((tm, wd), row) for wd, _ in outs],
        out_shape=[jax.ShapeDtypeStruct((t, wd), dt) for wd, dt in outs],
        compiler_params=_cparams(("parallel",)),
        name="outproj0_inproj1",
    )(x2, a2, o2, wo, g, w, cos, sa, sb)


def _diff_kernel(lq1_ref, lk1_ref, lq2_ref, lk2_ref, g_ref, q_ref, qn_ref, k_ref, v_ref, cg_ref, o_ref,
                 s0_ref, s1_ref, mx0_ref, mx1_ref, m_ref, acc_ref, *, kc, unroll, lam_init):
    tq = q_ref.shape[1]
    nch = k_ref.shape[1] // kc
    lane = lax.broadcasted_iota(jnp.int32, (tq, LANES), 1)

    def stack_maps(q):
        zero = jnp.zeros_like(q)
        return jnp.concatenate([jnp.where(lane < HEAD_DIM, q, zero), jnp.where(lane < HEAD_DIM, zero, q)], axis=0)

    qq = stack_maps(q_ref[0])
    qq_next = stack_maps(qn_ref[0])
    slots = ((s0_ref, mx0_ref), (s1_ref, mx1_ref))

    def scores(qs, c, slot):
        s_ref, mx_ref = slots[slot]
        kch = k_ref[0, pl.ds(pl.multiple_of(c * kc, kc), kc), :]
        s = lax.dot_general(qs, kch, (((1,), (1,)), ((), ())), preferred_element_type=F32)
        s_ref[...] = s
        mx_ref[...] = jnp.broadcast_to(jnp.max(s, axis=-1, keepdims=True), mx_ref.shape)

    def update(c, slot):
        s_ref, mx_ref = slots[slot]
        vch = v_ref[0, pl.ds(pl.multiple_of(c * kc, kc), kc), :]
        m_old = m_ref[...]
        m_new = jnp.maximum(m_old, mx_ref[...])
        alpha = jnp.exp2(m_old - m_new)
        p = jnp.exp2(s_ref[...] - jnp.concatenate([m_new] * (kc // LANES), axis=1)).astype(BF16)
        acc_ref[...] = acc_ref[...] * jnp.concatenate([alpha, alpha], axis=1) + jnp.dot(
            p, vch, preferred_element_type=F32)
        m_ref[...] = m_new

    m_ref[...] = jnp.full(m_ref.shape, NEG_BIG, F32)
    acc_ref[...] = jnp.zeros(acc_ref.shape, F32)

    @pl.when(pl.program_id(2) == 0)
    def _():
        scores(qq, 0, 0)

    ntrips = nch // unroll

    def body(t, carry):
        for u in range(unroll):
            c = unroll * t + u
            if u < unroll - 1:
                scores(qq, c + 1, (u + 1) % 2)
            elif ntrips == 1:
                scores(qq_next, 0, 0)
            else:
                wrap = t == ntrips - 1
                scores(jnp.where(wrap, qq_next, qq), jnp.where(wrap, 0, c + 1), 0)
            update(c, u % 2)
        return carry

    if ntrips == 1:
        body(0, 0)
    else:
        lax.fori_loop(0, ntrips, body, 0)

    lam = (jnp.exp(jnp.sum(lq1_ref[...] * lk1_ref[...], axis=-1, keepdims=True))
           - jnp.exp(jnp.sum(lq2_ref[...] * lk2_ref[...], axis=-1, keepdims=True)) + lam_init)
    o0 = acc_ref[0:tq, 0:LANES] / acc_ref[0:tq, LANES:2 * LANES]
    o1 = acc_ref[tq:2 * tq, 0:LANES] / acc_ref[tq:2 * tq, LANES:2 * LANES]
    o = o0 - lam * o1
    o = o * lax.rsqrt(jnp.mean(o * o, axis=-1, keepdims=True) + EPS) * g_ref[...] * (1.0 - lam_init)
    o_ref[0] = (o * cg_ref[0].astype(F32)).astype(BF16)


def _diff_attn(lq1, lk1, lq2, lk2, g, q, k, v, cg, lam_init, tq=512):
    bsz, seq, _ = q.shape
    kc = next(c for c in (1024, 512, 256) if seq % (2 * c) == 0)
    unroll = 4 if seq % (4 * kc) == 0 else 2
    nq = seq // tq
    small = pl.BlockSpec((1, HEAD_DIM), lambda b, h, i: (0, 0))
    tile = pl.BlockSpec((1, tq, LANES), lambda b, h, i: (b, i, h))
    next_tile = pl.BlockSpec((1, tq, LANES), lambda b, h, i: (b, jnp.minimum(i + 1, nq - 1), h))
    return pl.pallas_call(
        functools.partial(_diff_kernel, kc=kc, unroll=unroll, lam_init=lam_init),
        grid=(bsz, DIFF_HEADS, nq),
        in_specs=[small, small, small, small, pl.BlockSpec((1, LANES), lambda b, h, i: (0, 0)), tile, next_tile,
                  pl.BlockSpec((1, seq, LANES), lambda b, h, i: (b, 0, h)),
                  pl.BlockSpec((1, seq, 2 * LANES), lambda b, h, i: (b, 0, h)), tile],
        out_specs=tile,
        out_shape=jax.ShapeDtypeStruct((bsz, seq, 512), BF16),
        scratch_shapes=[pltpu.VMEM((2 * tq, kc), F32), pltpu.VMEM((2 * tq, kc), F32),
                        pltpu.VMEM((2 * tq, LANES), F32), pltpu.VMEM((2 * tq, LANES), F32),
                        pltpu.VMEM((2 * tq, LANES), F32), pltpu.VMEM((2 * tq, 2 * LANES), F32)],
        compiler_params=_cparams(("parallel", "parallel", "arbitrary")),
        name="diff_attn",
    )(lq1, lk1, lq2, lk2, g, q, q, k, v, cg)


def _split3(x):
    hi = x.astype(BF16)
    r = x - hi.astype(F32)
    mid = r.astype(BF16)
    lo = (r - mid.astype(F32)).astype(BF16)
    return hi, mid, lo


def _ssd_fwd_kernel(xc_ref, xp_ref, xn_ref, dt_ref, cw_ref, cb_ref, dtb_ref, a_ref, dsk_ref, y_ref, xconv_ref,
                    carry_ref, xf_ref):
    L = BLOCK
    nchunk = xc_ref.shape[1] // L
    c = pl.program_id(1)
    last = pl.num_programs(1) - 1

    @pl.when(c == 0)
    def _():
        carry_ref[...] = jnp.zeros(carry_ref.shape, F32)

    xp, xn = xp_ref[0], xn_ref[0]
    halo_p = jnp.where(c > 0, xp, jnp.zeros_like(xp))
    halo_n = jnp.where(c < last, xn, jnp.zeros_like(xn))
    r_i = lax.broadcasted_iota(jnp.int32, (L, L + 2 * HALO), 0)
    c_i = lax.broadcasted_iota(jnp.int32, (L, L + 2 * HALO), 1)
    first = HALO - (SSM_CONV - 1) // 2
    shifts = [None if first + w == HALO else jnp.where(c_i == r_i + first + w, 1.0, 0.0).astype(BF16)
              for w in range(SSM_CONV)]
    wd = 2 * LANES
    for nb in range(SSM_CONV_DIM // wd):
        cols = slice(nb * wd, (nb + 1) * wd)
        padded = jnp.concatenate([halo_p[:, cols], xc_ref[0, :, cols], halo_n[:, cols]], axis=0)
        for ci in range(nchunk):
            r0 = ci * L
            window = padded[r0:r0 + L + 2 * HALO]
            acc = jnp.broadcast_to(cb_ref[:, cols], (L, wd))
            for w in range(SSM_CONV):
                tap = (xc_ref[0, r0:r0 + L, cols].astype(F32) if shifts[w] is None
                       else jnp.dot(shifts[w], window, preferred_element_type=F32))
                acc = acc + tap * cw_ref[w:w + 1, cols]
            xbc = _silu(acc)
            xconv_ref[0, r0:r0 + L, cols] = xbc.astype(BF16)
            xf_ref[r0:r0 + L, cols] = xbc

    def xcols(k, a, b):
        return xf_ref[k * L:(k + 1) * L, a:b]

    def emit(k, j, y):
        cols = slice(j * LANES, (j + 1) * LANES)
        y_ref[0, k * L:(k + 1) * L, cols] = y + xcols(k, j * LANES, (j + 1) * LANES) * dsk_ref[:, cols]

    _ssd_scan_block(xcols, list(range(nchunk)), dt_ref, dtb_ref, a_ref, carry_ref, emit, reverse=False, col0=0)


def _ssd_bwd_kernel(xconv_ref, dt_ref, dtb_ref, a_ref, yf_ref, zg_ref, sg_ref, d_ref, carry_ref, yb_ref):
    L = BLOCK
    nchunk = xconv_ref.shape[1] // L

    @pl.when(pl.program_id(1) == 0)
    def _():
        carry_ref[...] = jnp.zeros(carry_ref.shape, F32)

    def xcols(k, a, b):
        return xconv_ref[0, k * L:(k + 1) * L, a:b].astype(F32)

    def emit(k, j, y):
        yb_ref[k * L:(k + 1) * L, j * LANES:(j + 1) * LANES] = y

    _ssd_scan_block(xcols, list(reversed(range(nchunk))), dt_ref, dtb_ref, a_ref, carry_ref, emit,
                    reverse=True, col0=SSM_HEADS)
    y = (yf_ref[0] + yb_ref[...]) * zg_ref[0].astype(F32)
    d_ref[0] = _rmsnorm(y, sg_ref[...]).astype(BF16)


def _ssd_scan_block(xcols, order, dt_ref, dtb_ref, a_ref, carry_ref, emit, *, reverse, col0):
    L = BLOCK
    nchunk = len(order)
    chunk = lambda m, k: m[k * L:(k + 1) * L]

    z = dt_ref[0] + dtb_ref[...]
    dt = jnp.maximum(z, 0.0) + jnp.log(1.0 + jnp.exp(-jnp.abs(z)))
    a = dt * a_ref[...]
    r_i = lax.broadcasted_iota(jnp.int32, (L, L), 0)
    c_i = lax.broadcasted_iota(jnp.int32, (L, L), 1)
    causal = (r_i <= c_i) if reverse else (r_i >= c_i)
    tri = jnp.where(causal, 1.0, 0.0).astype(BF16)
    pieces = _split3(a * LOG2E)
    acum = [sum(jnp.dot(tri, chunk(pc, k), preferred_element_type=F32) for pc in pieces) for k in range(nchunk)]
    acum_t = [m.T for m in acum]
    edge = 0 if reverse else L - 1
    a_end = [m[edge:edge + 1, :] for m in acum]

    lane = lax.broadcasted_iota(jnp.int32, (L, LANES), 1)
    low = lane < HEAD_DIM
    lane1 = lax.broadcasted_iota(jnp.int32, (1, LANES), 1)

    cbs, bts, cms = [], [], []
    for k in range(nchunk):
        bm = [xcols(k, SSM_INNER + g * SSM_STATE, SSM_INNER + (g + 1) * SSM_STATE) for g in range(2)]
        cm = [xcols(k, SSM_INNER + (2 + g) * SSM_STATE, SSM_INNER + (3 + g) * SSM_STATE).astype(BF16)
              for g in range(2)]
        cbs.append([lax.dot_general(cm[g], bm[g].astype(BF16), (((1,), (1,)), ((), ())),
                                    preferred_element_type=F32) for g in range(2)])
        bts.append([bm[g].T.astype(BF16) for g in range(2)])
        cms.append(cm)
    for j in range(SSM_HEADS // 2):
        g = j // 2
        ca = col0 + 2 * j
        ydiag, states, decay_out, decay_in = [], [], [], []
        for k in range(nchunk):
            acb = [jnp.broadcast_to(acum[k][:, ca + hh:ca + hh + 1], (L, LANES)) for hh in range(2)]
            ac_pair = jnp.where(low, acb[0], acb[1])
            dtk = chunk(dt, k)
            dt_pair = jnp.where(low, dtk[:, ca:ca + 1], dtk[:, ca + 1:ca + 2])
            end_pair = jnp.where(lane1 < HEAD_DIM, a_end[k][:, ca:ca + 1], a_end[k][:, ca + 1:ca + 2])
            xdt = xcols(k, j * LANES, (j + 1) * LANES) * dt_pair
            xdt_b = xdt.astype(BF16)
            zero = jnp.zeros_like(xdt_b)
            ms = []
            for hh in range(2):
                seg = acb[hh] - acum_t[k][ca + hh:ca + hh + 1, :]
                dec = jnp.exp2(jnp.where(causal, seg, NEG_BIG))
                ms.append((cbs[k][g] * dec).astype(BF16))
            lhs = jnp.concatenate(ms, axis=1)
            rhs = jnp.concatenate([jnp.where(low, xdt_b, zero), jnp.where(low, zero, xdt_b)], axis=0)
            ydiag.append(jnp.dot(lhs, rhs, preferred_element_type=F32))
            states.append(jnp.dot(bts[k][g], (xdt * jnp.exp2(end_pair - ac_pair)).astype(BF16),
                                  preferred_element_type=F32))
            decay_out.append(jnp.exp2(end_pair))
            decay_in.append(jnp.exp2(ac_pair))
        carry = carry_ref[j]
        for k in order:
            emit(k, j, ydiag[k] + jnp.dot(cms[k][g], carry.astype(BF16), preferred_element_type=F32) * decay_in[k])
            carry = carry * decay_out[k] + states[k]
        carry_ref[j] = carry


_SSD_CARRY = pltpu.VMEM((SSM_HEADS // 2, SSM_STATE, LANES), F32)
_SSD_ROWS = 4 * BLOCK


def _ssd_fwd(xbc, dt, cw, cb, dtb, a, dsk):
    bsz, seq, _ = xbc.shape
    vec = lambda wd: pl.BlockSpec((1, wd), lambda b, c: (0, 0))
    chunk = lambda wd: pl.BlockSpec((1, _SSD_ROWS, wd), lambda b, c: (b, c, 0))
    return pl.pallas_call(
        _ssd_fwd_kernel,
        grid=(bsz, seq // _SSD_ROWS),
        in_specs=_halo_specs(_SSD_ROWS, seq, SSM_CONV_DIM) + [
            chunk(LANES), pl.BlockSpec((SSM_CONV, SSM_CONV_DIM), lambda b, c: (0, 0)), vec(SSM_CONV_DIM),
            vec(LANES), vec(LANES), vec(SSM_INNER)],
        out_specs=[chunk(SSM_INNER), chunk(SSM_CONV_DIM)],
        out_shape=[jax.ShapeDtypeStruct((bsz, seq, SSM_INNER), F32),
                   jax.ShapeDtypeStruct((bsz, seq, SSM_CONV_DIM), BF16)],
        scratch_shapes=[_SSD_CARRY, pltpu.VMEM((_SSD_ROWS, SSM_CONV_DIM), F32)],
        compiler_params=_cparams(("parallel", "arbitrary")),
        name="ssd_fwd",
    )(xbc, xbc, xbc, dt, cw, cb, dtb, a, dsk)


def _ssd_bwd(xconv, dt, dtb, a, yf, zg, sg):
    bsz, seq, _ = xconv.shape
    nc = seq // _SSD_ROWS
    vec = lambda wd: pl.BlockSpec((1, wd), lambda b, c: (0, 0))
    chunk = lambda wd: pl.BlockSpec((1, _SSD_ROWS, wd), lambda b, c: (b, nc - 1 - c, 0))
    return pl.pallas_call(
        _ssd_bwd_kernel,
        grid=(bsz, nc),
        in_specs=[chunk(SSM_CONV_DIM), chunk(LANES), vec(LANES), vec(LANES), chunk(SSM_INNER), chunk(SSM_INNER),
                  vec(SSM_INNER)],
        out_specs=chunk(SSM_INNER),
        out_shape=jax.ShapeDtypeStruct((bsz, seq, SSM_INNER), BF16),
        scratch_shapes=[_SSD_CARRY, pltpu.VMEM((_SSD_ROWS, SSM_INNER), F32)],
        compiler_params=_cparams(("parallel", "arbitrary")),
        name="ssd_bwd",
    )(xconv, dt, dtb, a, yf, zg, sg)


def _final_kernel(x1_ref, c_ref, d_ref, wo_ref, fg_ref, o_ref, *, parts):
    rows = x1_ref.shape[0] // parts
    for part in range(parts):
        r = slice(part * rows, (part + 1) * rows)
        x2 = (x1_ref[r, :] + jnp.dot(c_ref[r, :], wo_ref[0:512, :], preferred_element_type=F32)
              + jnp.dot(d_ref[r, :], wo_ref[512:1024, :], preferred_element_type=F32))
        o_ref[r, :] = _rmsnorm(x2, fg_ref[...])


def _final(x1, c2, d2, wo, fg, tm):
    t = x1.shape[0]
    row = lambda i: (i, 0)
    full = lambda i: (0, 0)
    half = pl.BlockSpec((tm, 512), row)
    return pl.pallas_call(
        functools.partial(_final_kernel, parts=2),
        grid=(t // tm,),
        in_specs=[pl.BlockSpec((tm, D_MODEL), row), half, half, pl.BlockSpec((D_MODEL, D_MODEL), full),
                  pl.BlockSpec((1, D_MODEL), full)],
        out_specs=pl.BlockSpec((tm, D_MODEL), row),
        out_shape=jax.ShapeDtypeStruct((t, D_MODEL), F32),
        compiler_params=_cparams(("parallel",)),
        name="outproj1_final",
    )(x1, c2, d2, wo, fg)


def _rope_tables(seq):
    inv = 1.0 / (ROPE_THETA ** (jnp.arange(0, HEAD_DIM, 2, dtype=F32) / HEAD_DIM))
    f = jnp.arange(seq, dtype=F32)[:, None] * inv[None, :]
    emb = jnp.concatenate([f, f, f, f], axis=-1)
    cos, sin = jnp.cos(emb), jnp.sin(emb)
    first_half = (jnp.arange(LANES) % HEAD_DIM) < HEAD_DIM // 2
    return cos, jnp.where(first_half, -sin, 0.0), jnp.where(first_half, 0.0, sin)


def _pair_perm():
    idx = []
    for j in range(SWA_HEADS // 2):
        idx += list(range(j * HEAD_DIM, (j + 1) * HEAD_DIM))
        idx += list(range((j + 4) * HEAD_DIM, (j + 5) * HEAD_DIM))
    return jnp.asarray(idx, jnp.int32)


def _pad_lanes(v, offset):
    return jnp.zeros((1, LANES), F32).at[0, offset:offset + v.shape[0]].set(v.astype(F32))


def _prepare(norm_g, w_in0, conv_w, conv_b, conv_ln_g, conv_ln_b, sink, w_out0, w_in1, lambda_q1, lambda_k1,
             lambda_q2, lambda_k2, diff_norm_g, ssm_conv_w, ssm_conv_b, dt_bias_f, dt_bias_b, a_log_f, a_log_b,
             d_skip, ssm_norm_g, w_out1, final_norm_g):
    perm = _pair_perm()
    w0 = w_in0[0]
    w0 = w0.at[:, 1536:2048].set(w0[:, 1536:2048][:, perm]).at[:, 2304:2816].set(w0[:, 2304:2816][:, perm])
    wo0 = w_out0[0].at[512:1024, :].set(w_out0[0][512:1024, :][perm, :])
    w1 = jnp.pad(w_in1[0], ((0, 0), (0, IN_ODD_PAD - IN_ODD)))
    row = lambda v: v.astype(F32).reshape(1, -1)
    return dict(
        g0=row(norm_g[0]), g1=row(norm_g[1]), w0=w0.astype(BF16), wo0=wo0.astype(BF16), w1=w1.astype(BF16),
        wo1=w_out1[0].astype(BF16), cb=row(conv_b[0]), lg=row(conv_ln_g[0]),
        cw=jnp.broadcast_to(conv_w[0].astype(F32)[:, None, :], (CONV_WIDTH, SUBLANES, CONV_CH)),
        lb=row(conv_ln_b[0]), sink=sink[0].astype(F32),
        lq1=row(lambda_q1[0]), lk1=row(lambda_k1[0]), lq2=row(lambda_q2[0]), lk2=row(lambda_k2[0]),
        dg=row(diff_norm_g[0]), scw=ssm_conv_w[0].astype(F32), scb=row(ssm_conv_b[0]),
        dtb_f=_pad_lanes(dt_bias_f[0], 0), dtb_b=_pad_lanes(dt_bias_b[0], SSM_HEADS),
        a_f=_pad_lanes(-jnp.exp(a_log_f[0].astype(F32)), 0), a_b=_pad_lanes(-jnp.exp(a_log_b[0].astype(F32)), SSM_HEADS),
        dsk=row(jnp.repeat(d_skip[0].astype(F32), HEAD_DIM)), sg=row(ssm_norm_g[0]), fg=row(final_norm_g))


def _trunk(x, p, tm=512):
    bsz, seq, _ = x.shape
    assert seq % tm == 0 and seq % 512 == 0
    t = bsz * seq
    lam_init = 0.8 - 0.6 * math.exp(-0.3 * 1)
    cos, sa, sb = _rope_tables(seq)
    x2 = x.reshape(t, D_MODEL)
    apre, ga, q, k, v, gb = _inproj0(x2, p["g0"], p["w0"], cos, sa, sb, seq, tm)
    r3 = lambda arr: arr.reshape(bsz, seq, arr.shape[-1])
    a = _convmod(r3(apre), r3(ga), p["cw"], p["cb"], p["lg"], p["lb"])
    o = _swa(p["sink"], r3(q), r3(k), r3(v), r3(gb))
    x1, q1, k1, v1, cg, zg, xbc, dt = _mid(x2, a.reshape(t, 512), o.reshape(t, 512), p["wo0"], p["g1"], p["w1"],
                                           cos, sa, sb, seq, tm)
    c = _diff_attn(p["lq1"], p["lk1"], p["lq2"], p["lk2"], p["dg"], r3(q1), r3(k1), r3(v1), r3(cg), lam_init)
    yf, xconv = _ssd_fwd(r3(xbc), r3(dt), p["scw"], p["scb"], p["dtb_f"], p["a_f"], p["dsk"])
    d = _ssd_bwd(xconv, r3(dt), p["dtb_b"], p["a_b"], yf, r3(zg), p["sg"])
    out = _final(x1, c.reshape(t, 512), d.reshape(t, 512), p["wo1"], p["fg"], tm)
    return out.reshape(bsz, seq, D_MODEL)


def kernel(x_prompt, x_sample, norm_g, w_in0, conv_w, conv_b, conv_ln_g, conv_ln_b, sink, w_out0, w_in1, lambda_q1, lambda_k1, lambda_q2, lambda_k2, diff_norm_g, ssm_conv_w, ssm_conv_b, dt_bias_f, dt_bias_b, a_log_f, a_log_b, d_skip, ssm_norm_g, w_out1, final_norm_g):
    p = _prepare(norm_g, w_in0, conv_w, conv_b, conv_ln_g, conv_ln_b, sink, w_out0, w_in1, lambda_q1, lambda_k1,
                 lambda_q2, lambda_k2, diff_norm_g, ssm_conv_w, ssm_conv_b, dt_bias_f, dt_bias_b, a_log_f, a_log_b,
                 d_skip, ssm_norm_g, w_out1, final_norm_g)
    return (_trunk(x_prompt, p), _trunk(x_sample, p))
```

```python
import functools
import math

import jax
import jax.numpy as jnp
from jax import lax
from jax.experimental import pallas as pl
from jax.experimental.pallas import tpu as pltpu

F32 = jnp.float32
BF16 = jnp.bfloat16

D_MODEL = 1024
EPS = 1e-6
ROPE_THETA = 10000.0
HEAD_DIM = 64
LANES = 128
SUBLANES = 8
HALO = 16
CONV_CH = 512
CONV_WIDTH = 31
SWA_HEADS = 8
BLOCK = 128
DIFF_HEADS = 4
SSM_INNER = 512
SSM_HEADS = 8
SSM_STATE = 128
SSM_CONV = 5
SSM_CONV_DIM = 1024
IN_EVEN = 2816
IN_ODD = 3600
IN_ODD_PAD = 3712
NEG_BIG = -1e30
LOG2E = math.log2(math.e)
VMEM_LIMIT = 56 * 1024 * 1024


def _cparams(sem):
    return pltpu.CompilerParams(dimension_semantics=sem, vmem_limit_bytes=VMEM_LIMIT)


def _sigmoid(x):
    return 1.0 / (1.0 + jnp.exp2(x * (-LOG2E)))


def _silu(x):
    return x * _sigmoid(x)


def _rmsnorm(x, g):
    return x * lax.rsqrt(jnp.mean(x * x, axis=-1, keepdims=True) + EPS) * g


def _rope_cols(x, cos, sa, sb):
    outs = []
    for j in range(x.shape[1] // LANES):
        xj = x[:, j * LANES:(j + 1) * LANES]
        outs.append(xj * cos + pltpu.roll(xj, LANES - 32, 1) * sa + pltpu.roll(xj, 32, 1) * sb)
    return outs[0] if len(outs) == 1 else jnp.concatenate(outs, axis=1)


def _inproj0_kernel(x_ref, g_ref, w_ref, cos_ref, sa_ref, sb_ref,
                    apre_ref, ga_ref, q_ref, k_ref, v_ref, gb_ref):
    parts = 2
    rows = x_ref.shape[0] // parts
    groups = [slice(part * rows, (part + 1) * rows) for part in range(parts)]
    hbs = [_rmsnorm(x_ref[r, :], g_ref[...]).astype(BF16) for r in groups]
    for r, hb in zip(groups, hbs):
        def mm(a, b):
            return jnp.dot(hb, w_ref[:, a:b], preferred_element_type=F32)

        cos, sa, sb = cos_ref[r, :], sa_ref[r, :], sb_ref[r, :]
        apre_ref[r, :] = (mm(0, 512) * _sigmoid(mm(512, 1024))).astype(BF16)
        ga_ref[r, :] = _silu(mm(1024, 1536)).astype(BF16)
        gb_ref[r, :] = _silu(mm(2304, 2816)).astype(BF16)
        q_ref[r, :] = (_rope_cols(mm(1536, 2048), cos, sa, sb) * (HEAD_DIM ** -0.5 * LOG2E)).astype(BF16)
        k_ref[r, :] = _rope_cols(mm(2048, 2176), cos, sa, sb).astype(BF16)
        v_ref[r, :] = mm(2176, 2304).astype(BF16)


def _inproj0(x2, g, w, cos, sa, sb, seq, tm):
    t = x2.shape[0]
    ns = seq // tm
    row = lambda i: (i, 0)
    pos = lambda i: (i % ns, 0)
    full = lambda i: (0, 0)
    widths = (512, 512, 512, 128, 128, 512)
    return pl.pallas_call(
        _inproj0_kernel,
        grid=(t // tm,),
        in_specs=[pl.BlockSpec((tm, D_MODEL), row), pl.BlockSpec((1, D_MODEL), full),
                  pl.BlockSpec((D_MODEL, IN_EVEN), full),
                  pl.BlockSpec((tm, LANES), pos), pl.BlockSpec((tm, LANES), pos), pl.BlockSpec((tm, LANES), pos)],
        out_specs=[pl.BlockSpec((tm, wd), row) for wd in widths],
        out_shape=[jax.ShapeDtypeStruct((t, wd), BF16) for wd in widths],
        compiler_params=_cparams(("parallel",)),
        name="inproj0",
    )(x2, g, w, cos, sa, sb)


def _convmod_kernel(xc_ref, xp_ref, xn_ref, ga_ref, cw_ref, cb_ref, lg_ref, lb_ref, o_ref, rol_ref, *, ts, rs):
    i = pl.program_id(1)
    last = pl.num_programs(1) - 1
    xp, xn = xp_ref[0], xn_ref[0]
    padded = jnp.concatenate([jnp.where(i > 0, xp, jnp.zeros_like(xp)), xc_ref[0],
                              jnp.where(i < last, xn, jnp.zeros_like(xn))], axis=0)
    rows = ts + 2 * HALO
    rol_ref[0] = padded.astype(F32)
    for rb in range(0, rows, BLOCK):
        nr = min(BLOCK, rows - rb)
        nk = min(BLOCK + HALO, rows - rb)
        r_i = lax.broadcasted_iota(jnp.int32, (nr, nk), 0)
        c_i = lax.broadcasted_iota(jnp.int32, (nr, nk), 1)
        window = padded[rb:rb + nk]
        for j in range(1, SUBLANES):
            shift = jnp.where(c_i == r_i + j, 1.0, 0.0).astype(BF16)
            rol_ref[j, rb:rb + nr] = jnp.dot(shift, window, preferred_element_type=F32)
    first = HALO - (CONV_WIDTH - 1) // 2
    for r0 in range(0, ts, rs):
        acc = jnp.broadcast_to(cb_ref[...], (rs, CONV_CH))
        for w in range(CONV_WIDTH):
            a, j = divmod(first + w, SUBLANES)
            lo = r0 + a * SUBLANES
            acc = acc + rol_ref[j, lo:lo + rs, :] * jnp.concatenate([cw_ref[w]] * (rs // SUBLANES), axis=0)
        mu = jnp.mean(acc, axis=-1, keepdims=True)
        xc = acc - mu
        var = jnp.mean(xc * xc, axis=-1, keepdims=True)
        y = xc * lax.rsqrt(var + EPS) * lg_ref[...] + lb_ref[...]
        o_ref[0, r0:r0 + rs, :] = (_silu(y) * ga_ref[0, r0:r0 + rs, :].astype(F32)).astype(BF16)


def _halo_specs(ts, seq, width):
    hb = ts // HALO
    nh = seq // HALO
    return [pl.BlockSpec((1, ts, width), lambda b, i: (b, i, 0)),
            pl.BlockSpec((1, HALO, width), lambda b, i: (b, jnp.maximum(i * hb - 1, 0), 0)),
            pl.BlockSpec((1, HALO, width), lambda b, i: (b, jnp.minimum((i + 1) * hb, nh - 1), 0))]


def _convmod(apre, ga, cw, cb, lg, lb, ts=512, rs=32):
    bsz, seq, _ = apre.shape
    vec = pl.BlockSpec((1, CONV_CH), lambda b, i: (0, 0))
    return pl.pallas_call(
        functools.partial(_convmod_kernel, ts=ts, rs=rs),
        grid=(bsz, seq // ts),
        in_specs=_halo_specs(ts, seq, CONV_CH) + [
            pl.BlockSpec((1, ts, CONV_CH), lambda b, i: (b, i, 0)),
            pl.BlockSpec((CONV_WIDTH, SUBLANES, CONV_CH), lambda b, i: (0, 0, 0)), vec, vec, vec],
        out_specs=pl.BlockSpec((1, ts, CONV_CH), lambda b, i: (b, i, 0)),
        out_shape=jax.ShapeDtypeStruct((bsz, seq, CONV_CH), BF16),
        scratch_shapes=[pltpu.VMEM((SUBLANES, ts + 2 * HALO, CONV_CH), F32)],
        compiler_params=_cparams(("parallel", "parallel")),
        name="convmod",
    )(apre, apre, apre, ga, cw, cb, lg, lb)


def _swa_kernel(sink_ref, q_ref, kp_ref, kc_ref, kn_ref, vp_ref, vc_ref, vn_ref, gb_ref, o_ref, *, nblk, group):
    n = pl.program_id(1)
    last = pl.num_programs(1) - 1
    kcat = jnp.concatenate([kp_ref[0], kc_ref[0], kn_ref[0]], axis=0)
    vcat = jnp.concatenate([vp_ref[0], vc_ref[0], vn_ref[0]], axis=0)
    vaug = jnp.concatenate([vcat, jnp.ones_like(vcat)], axis=1)
    rows = lax.broadcasted_iota(jnp.int32, (2 * BLOCK, 3 * BLOCK), 0)
    qi = jnp.where(rows >= BLOCK, rows - BLOCK, rows)
    ci = lax.broadcasted_iota(jnp.int32, (2 * BLOCK, 3 * BLOCK), 1)
    rel = ci - BLOCK - qi
    band = (rel >= -BLOCK) & (rel <= BLOCK)
    first_key = jnp.where(n > 0, 0, BLOCK)
    end_key = jnp.where(n < last, 3 * BLOCK, 2 * BLOCK)
    lane = lax.broadcasted_iota(jnp.int32, (BLOCK, LANES), 1)
    low = lane < HEAD_DIM
    row1 = lax.broadcasted_iota(jnp.int32, (2 * BLOCK, 1), 0)
    npair = SWA_HEADS // 2
    sinks = [jnp.where(row1 < BLOCK, sink_ref[j], sink_ref[j + npair]) * LOG2E for j in range(npair)]
    masks = []
    for b in range(nblk):
        mask = band
        if b == 0:
            mask = mask & (ci >= first_key)
        if b == nblk - 1:
            mask = mask & (ci < end_key)
        masks.append(mask)
    for b0 in range(0, nblk, group):
        probs = [(b, j) for b in range(b0, min(b0 + group, nblk)) for j in range(npair)]
        rsl = {b: slice(b * BLOCK, (b + 1) * BLOCK) for b, _ in probs}
        csl = [slice(j * LANES, (j + 1) * LANES) for j in range(npair)]
        qqs = []
        for b, j in probs:
            qv = q_ref[0, rsl[b], csl[j]]
            zero = jnp.zeros_like(qv)
            qqs.append(jnp.concatenate([jnp.where(low, qv, zero), jnp.where(low, zero, qv)], axis=0))
        ss = [jnp.where(masks[b], lax.dot_general(qq, kcat[b * BLOCK:(b + 3) * BLOCK], (((1,), (1,)), ((), ())),
                                                  preferred_element_type=F32), NEG_BIG)
              for (b, j), qq in zip(probs, qqs)]
        ms = [jnp.maximum(jnp.max(s, axis=-1, keepdims=True), sinks[j]) for (b, j), s in zip(probs, ss)]
        pvs = [jnp.dot(jnp.exp2(s - m).astype(BF16), vaug[b * BLOCK:(b + 3) * BLOCK], preferred_element_type=F32)
               for (b, j), s, m in zip(probs, ss, ms)]
        for (b, j), pv, m in zip(probs, pvs, ms):
            pv = pv[:, 0:LANES] / (pv[:, LANES:2 * LANES] + jnp.exp2(sinks[j] - m))
            o = jnp.where(low, pv[:BLOCK], pv[BLOCK:])
            o_ref[0, rsl[b], csl[j]] = (o * gb_ref[0, rsl[b], csl[j]].astype(F32)).astype(BF16)


def _swa(sink, q, k, v, gb, nblk=4):
    bsz, seq, _ = q.shape
    rows = nblk * BLOCK
    nb = seq // BLOCK
    cur = lambda b, n: (b, n, 0)
    prv = lambda b, n: (b, jnp.maximum(n * nblk - 1, 0), 0)
    nxt = lambda b, n: (b, jnp.minimum((n + 1) * nblk, nb - 1), 0)
    edge = lambda im: pl.BlockSpec((1, BLOCK, LANES), im)
    mid = pl.BlockSpec((1, rows, LANES), cur)
    wide = pl.BlockSpec((1, rows, 512), cur)
    return pl.pallas_call(
        functools.partial(_swa_kernel, nblk=nblk, group=1),
        grid=(bsz, seq // rows),
        in_specs=[pl.BlockSpec(memory_space=pltpu.SMEM), wide,
                  edge(prv), mid, edge(nxt), edge(prv), mid, edge(nxt), wide],
        out_specs=wide,
        out_shape=jax.ShapeDtypeStruct((bsz, seq, 512), BF16),
        compiler_params=_cparams(("parallel", "parallel")),
        name="swa",
    )(sink, q, k, k, k, v, v, v, gb)


def _mid_kernel(x_ref, a_ref, o_ref, wo_ref, g_ref, w_ref, cos_ref, sa_ref, sb_ref,
                x1_ref, q_ref, k_ref, v_ref, cg_ref, zg_ref, xbc_ref, dt_ref):
    parts = 2
    rows = x_ref.shape[0] // parts
    groups = [slice(part * rows, (part + 1) * rows) for part in range(parts)]
    hbs = []
    for r in groups:
        x1 = (x_ref[r, :] + jnp.dot(a_ref[r, :], wo_ref[0:512, :], preferred_element_type=F32)
              + jnp.dot(o_ref[r, :], wo_ref[512:1024, :], preferred_element_type=F32))
        x1_ref[r, :] = x1
        hbs.append(_rmsnorm(x1, g_ref[...]).astype(BF16))
    ones = jnp.ones((rows, LANES), BF16)
    for r, hb in zip(groups, hbs):
        def mm(a, b):
            return jnp.dot(hb, w_ref[:, a:b], preferred_element_type=F32)

        cos, sa, sb = cos_ref[r, :], sa_ref[r, :], sb_ref[r, :]
        q_ref[r, :] = (_rope_cols(mm(0, 512), cos, sa, sb) * (HEAD_DIM ** -0.5 * LOG2E)).astype(BF16)
        k_ref[r, :] = _rope_cols(mm(512, 1024), cos, sa, sb).astype(BF16)
        v = mm(1024, 1536).astype(BF16)
        for h in range(DIFF_HEADS):
            v_ref[r, 2 * h * LANES:(2 * h + 1) * LANES] = v[:, h * LANES:(h + 1) * LANES]
            v_ref[r, (2 * h + 1) * LANES:(2 * h + 2) * LANES] = ones
        dt_ref[r, :] = mm(3584, IN_ODD_PAD)
        cg_ref[r, :] = _silu(mm(1536, 2048)).astype(BF16)
        zg_ref[r, :] = _silu(mm(2048, 2560)).astype(BF16)
        xbc_ref[r, :] = mm(2560, 3584).astype(BF16)


def _mid(x2, a2, o2, wo, g, w, cos, sa, sb, seq, tm):
    t = x2.shape[0]
    ns = seq // tm
    row = lambda i: (i, 0)
    pos = lambda i: (i % ns, 0)
    full = lambda i: (0, 0)
    outs = ((D_MODEL, F32), (512, BF16), (512, BF16), (1024, BF16), (512, BF16), (512, BF16),
            (SSM_CONV_DIM, BF16), (LANES, F32))
    return pl.pallas_call(
        _mid_kernel,
        grid=(t // tm,),
        in_specs=[pl.BlockSpec((tm, D_MODEL), row), pl.BlockSpec((tm, 512), row), pl.BlockSpec((tm, 512), row),
                  pl.BlockSpec((D_MODEL, D_MODEL), full), pl.BlockSpec((1, D_MODEL), full),
                  pl.BlockSpec((D_MODEL, IN_ODD_PAD), full),
                  pl.BlockSpec((tm, LANES), pos), pl.BlockSpec((tm, LANES), pos), pl.BlockSpec((tm, LANES), pos)],
        out_specs=[pl.BlockSpec((tm, wd), row) for wd, _ in outs],
        out_shape=[jax.ShapeDtypeStruct((t, wd), dt) for wd, dt in outs],
        compiler_params=_cparams(("parallel",)),
        name="outproj0_inproj1",
    )(x2, a2, o2, wo, g, w, cos, sa, sb)


def _diff_kernel(lq1_ref, lk1_ref, lq2_ref, lk2_ref, g_ref, q_ref, qn_ref, k_ref, v_ref, cg_ref, o_ref,
                 s0_ref, s1_ref, mx0_ref, mx1_ref, m_ref, acc_ref, *, kc, unroll, lam_init):
    tq = q_ref.shape[1]
    nch = k_ref.shape[1] // kc
    lane = lax.broadcasted_iota(jnp.int32, (tq, LANES), 1)

    def stack_maps(q):
        zero = jnp.zeros_like(q)
        return jnp.concatenate([jnp.where(lane < HEAD_DIM, q, zero), jnp.where(lane < HEAD_DIM, zero, q)], axis=0)

    slots = ((s0_ref, mx0_ref), (s1_ref, mx1_ref))

    def scores(qs, c, slot):
        s_ref, mx_ref = slots[slot]
        kch = k_ref[0, pl.ds(pl.multiple_of(c * kc, kc), kc), :]
        s = lax.dot_general(qs, kch, (((1,), (1,)), ((), ())), preferred_element_type=F32)
        s_ref[...] = s
        mx_ref[...] = jnp.broadcast_to(jnp.max(s, axis=-1, keepdims=True), mx_ref.shape)

    def update(c, slot):
        s_ref, mx_ref = slots[slot]
        vch = v_ref[0, pl.ds(pl.multiple_of(c * kc, kc), kc), :]
        m_old = m_ref[...]
        m_new = jnp.maximum(m_old, mx_ref[...])
        alpha = jnp.exp2(m_old - m_new)
        p = jnp.exp2(s_ref[...] - jnp.concatenate([m_new] * (kc // LANES), axis=1)).astype(BF16)
        acc_ref[...] = acc_ref[...] * jnp.concatenate([alpha, alpha], axis=1) + jnp.dot(
            p, vch, preferred_element_type=F32)
        m_ref[...] = m_new

    @pl.when(pl.program_id(2) == 0)
    def _():
        scores(stack_maps(q_ref[0]), 0, 0)

    qq = stack_maps(q_ref[0])
    qq_next = stack_maps(qn_ref[0])
    m_ref[...] = jnp.full(m_ref.shape, NEG_BIG, F32)
    acc_ref[...] = jnp.zeros(acc_ref.shape, F32)
    ntrips = nch // unroll

    def body(t, carry):
        for u in range(unroll):
            c = unroll * t + u
            if u < unroll - 1:
                scores(qq, c + 1, (u + 1) % 2)
            elif ntrips == 1:
                scores(qq_next, 0, 0)
            else:
                wrap = t == ntrips - 1
                scores(jnp.where(wrap, qq_next, qq), jnp.where(wrap, 0, c + 1), 0)
            update(c, u % 2)
        return carry

    if ntrips == 1:
        body(0, 0)
    else:
        lax.fori_loop(0, ntrips, body, 0)

    lam = (jnp.exp(jnp.sum(lq1_ref[...] * lk1_ref[...], axis=-1, keepdims=True))
           - jnp.exp(jnp.sum(lq2_ref[...] * lk2_ref[...], axis=-1, keepdims=True)) + lam_init)
    o0 = acc_ref[0:tq, 0:LANES] / acc_ref[0:tq, LANES:2 * LANES]
    o1 = acc_ref[tq:2 * tq, 0:LANES] / acc_ref[tq:2 * tq, LANES:2 * LANES]
    o = o0 - lam * o1
    o = o * lax.rsqrt(jnp.mean(o * o, axis=-1, keepdims=True) + EPS) * g_ref[...] * (1.0 - lam_init)
    o_ref[0] = (o * cg_ref[0].astype(F32)).astype(BF16)


def _diff_attn(lq1, lk1, lq2, lk2, g, q, k, v, cg, lam_init, tq=512):
    bsz, seq, _ = q.shape
    kc = next(c for c in (1024, 512, 256) if seq % (2 * c) == 0)
    unroll = 4 if seq % (4 * kc) == 0 else 2
    nq = seq // tq
    small = pl.BlockSpec((1, HEAD_DIM), lambda b, h, i: (0, 0))
    tile = pl.BlockSpec((1, tq, LANES), lambda b, h, i: (b, i, h))
    next_tile = pl.BlockSpec((1, tq, LANES), lambda b, h, i: (b, jnp.minimum(i + 1, nq - 1), h))
    return pl.pallas_call(
        functools.partial(_diff_kernel, kc=kc, unroll=unroll, lam_init=lam_init),
        grid=(bsz, DIFF_HEADS, nq),
        in_specs=[small, small, small, small, pl.BlockSpec((1, LANES), lambda b, h, i: (0, 0)), tile, next_tile,
                  pl.BlockSpec((1, seq, LANES), lambda b, h, i: (b, 0, h)),
                  pl.BlockSpec((1, seq, 2 * LANES), lambda b, h, i: (b, 0, h)), tile],
        out_specs=tile,
        out_shape=jax.ShapeDtypeStruct((bsz, seq, 512), BF16),
        scratch_shapes=[pltpu.VMEM((2 * tq, kc), F32), pltpu.VMEM((2 * tq, kc), F32),
                        pltpu.VMEM((2 * tq, LANES), F32), pltpu.VMEM((2 * tq, LANES), F32),
                        pltpu.VMEM((2 * tq, LANES), F32), pltpu.VMEM((2 * tq, 2 * LANES), F32)],
        compiler_params=_cparams(("parallel", "parallel", "arbitrary")),
        name="diff_attn",
    )(lq1, lk1, lq2, lk2, g, q, q, k, v, cg)


def _split3(x):
    hi = x.astype(BF16)
    r = x - hi.astype(F32)
    mid = r.astype(BF16)
    lo = (r - mid.astype(F32)).astype(BF16)
    return hi, mid, lo


def _ssd_fwd_kernel(xc_ref, xp_ref, xn_ref, dt_ref, cw_ref, cb_ref, dtb_ref, a_ref, dsk_ref, y_ref, xconv_ref,
                    carry_ref, xf_ref):
    L = BLOCK
    nchunk = xc_ref.shape[1] // L
    c = pl.program_id(1)
    last = pl.num_programs(1) - 1

    @pl.when(c == 0)
    def _():
        carry_ref[...] = jnp.zeros(carry_ref.shape, F32)

    xp, xn = xp_ref[0], xn_ref[0]
    halo_p = jnp.where(c > 0, xp, jnp.zeros_like(xp))
    halo_n = jnp.where(c < last, xn, jnp.zeros_like(xn))
    r_i = lax.broadcasted_iota(jnp.int32, (L, L + 2 * HALO), 0)
    c_i = lax.broadcasted_iota(jnp.int32, (L, L + 2 * HALO), 1)
    first = HALO - (SSM_CONV - 1) // 2
    shifts = [None if first + w == HALO else jnp.where(c_i == r_i + first + w, 1.0, 0.0).astype(BF16)
              for w in range(SSM_CONV)]
    wd = 2 * LANES
    for nb in range(SSM_CONV_DIM // wd):
        cols = slice(nb * wd, (nb + 1) * wd)
        padded = jnp.concatenate([halo_p[:, cols], xc_ref[0, :, cols], halo_n[:, cols]], axis=0)
        for ci in range(nchunk):
            r0 = ci * L
            window = padded[r0:r0 + L + 2 * HALO]
            acc = jnp.broadcast_to(cb_ref[:, cols], (L, wd))
            for w in range(SSM_CONV):
                tap = (xc_ref[0, r0:r0 + L, cols].astype(F32) if shifts[w] is None
                       else jnp.dot(shifts[w], window, preferred_element_type=F32))
                acc = acc + tap * cw_ref[w:w + 1, cols]
            xbc = _silu(acc)
            xconv_ref[0, r0:r0 + L, cols] = xbc.astype(BF16)
            xf_ref[r0:r0 + L, cols] = xbc

    def xcols(k, a, b):
        return xf_ref[k * L:(k + 1) * L, a:b]

    def emit(k, j, y):
        cols = slice(j * LANES, (j + 1) * LANES)
        y_ref[0, k * L:(k + 1) * L, cols] = y + xcols(k, j * LANES, (j + 1) * LANES) * dsk_ref[:, cols]

    _ssd_scan_block(xcols, list(range(nchunk)), dt_ref, dtb_ref, a_ref, carry_ref, emit, reverse=False, col0=0)


def _ssd_bwd_kernel(xconv_ref, dt_ref, dtb_ref, a_ref, yf_ref, zg_ref, sg_ref, d_ref, carry_ref, yb_ref):
    L = BLOCK
    nchunk = xconv_ref.shape[1] // L

    @pl.when(pl.program_id(1) == 0)
    def _():
        carry_ref[...] = jnp.zeros(carry_ref.shape, F32)

    def xcols(k, a, b):
        return xconv_ref[0, k * L:(k + 1) * L, a:b].astype(F32)

    def emit(k, j, y):
        yb_ref[k * L:(k + 1) * L, j * LANES:(j + 1) * LANES] = y

    _ssd_scan_block(xcols, list(reversed(range(nchunk))), dt_ref, dtb_ref, a_ref, carry_ref, emit,
                    reverse=True, col0=SSM_HEADS)
    y = (yf_ref[0] + yb_ref[...]) * zg_ref[0].astype(F32)
    d_ref[0] = _rmsnorm(y, sg_ref[...]).astype(BF16)


def _ssd_scan_block(xcols, order, dt_ref, dtb_ref, a_ref, carry_ref, emit, *, reverse, col0):
    L = BLOCK
    nchunk = len(order)
    chunk = lambda m, k: m[k * L:(k + 1) * L]

    z = dt_ref[0] + dtb_ref[...]
    dt = jnp.maximum(z, 0.0) + jnp.log(1.0 + jnp.exp(-jnp.abs(z)))
    a = dt * a_ref[...]
    r_i = lax.broadcasted_iota(jnp.int32, (L, L), 0)
    c_i = lax.broadcasted_iota(jnp.int32, (L, L), 1)
    causal = (r_i <= c_i) if reverse else (r_i >= c_i)
    tri = jnp.where(causal, 1.0, 0.0).astype(BF16)
    pieces = _split3(a * LOG2E)
    acum = [sum(jnp.dot(tri, chunk(pc, k), preferred_element_type=F32) for pc in pieces) for k in range(nchunk)]
    acum_t = [m.T for m in acum]
    edge = 0 if reverse else L - 1
    a_end = [m[edge:edge + 1, :] for m in acum]

    lane = lax.broadcasted_iota(jnp.int32, (L, LANES), 1)
    low = lane < HEAD_DIM
    lane1 = lax.broadcasted_iota(jnp.int32, (1, LANES), 1)

    cbs, bts, cms = [], [], []
    for k in range(nchunk):
        bm = [xcols(k, SSM_INNER + g * SSM_STATE, SSM_INNER + (g + 1) * SSM_STATE) for g in range(2)]
        cm = [xcols(k, SSM_INNER + (2 + g) * SSM_STATE, SSM_INNER + (3 + g) * SSM_STATE).astype(BF16)
              for g in range(2)]
        cbs.append([lax.dot_general(cm[g], bm[g].astype(BF16), (((1,), (1,)), ((), ())),
                                    preferred_element_type=F32) for g in range(2)])
        bts.append([bm[g].T.astype(BF16) for g in range(2)])
        cms.append(cm)
    for j in range(SSM_HEADS // 2):
        g = j // 2
        ca = col0 + 2 * j
        ydiag, states, decay_out, decay_in = [], [], [], []
        for k in range(nchunk):
            acb = [jnp.broadcast_to(acum[k][:, ca + hh:ca + hh + 1], (L, LANES)) for hh in range(2)]
            ac_pair = jnp.where(low, acb[0], acb[1])
            dtk = chunk(dt, k)
            dt_pair = jnp.where(low, dtk[:, ca:ca + 1], dtk[:, ca + 1:ca + 2])
            end_pair = jnp.where(lane1 < HEAD_DIM, a_end[k][:, ca:ca + 1], a_end[k][:, ca + 1:ca + 2])
            xdt = xcols(k, j * LANES, (j + 1) * LANES) * dt_pair
            xdt_b = xdt.astype(BF16)
            zero = jnp.zeros_like(xdt_b)
            ms = []
            for hh in range(2):
                seg = acb[hh] - acum_t[k][ca + hh:ca + hh + 1, :]
                dec = jnp.exp2(jnp.where(causal, seg, NEG_BIG))
                ms.append((cbs[k][g] * dec).astype(BF16))
            lhs = jnp.concatenate(ms, axis=1)
            rhs = jnp.concatenate([jnp.where(low, xdt_b, zero), jnp.where(low, zero, xdt_b)], axis=0)
            ydiag.append(jnp.dot(lhs, rhs, preferred_element_type=F32))
            states.append(jnp.dot(bts[k][g], (xdt * jnp.exp2(end_pair - ac_pair)).astype(BF16),
                                  preferred_element_type=F32))
            decay_out.append(jnp.exp2(end_pair))
            decay_in.append(jnp.exp2(ac_pair))
        carry = carry_ref[j]
        for k in order:
            emit(k, j, ydiag[k] + jnp.dot(cms[k][g], carry.astype(BF16), preferred_element_type=F32) * decay_in[k])
            carry = carry * decay_out[k] + states[k]
        carry_ref[j] = carry


_SSD_CARRY = pltpu.VMEM((SSM_HEADS // 2, SSM_STATE, LANES), F32)
_SSD_ROWS = 4 * BLOCK


def _ssd_fwd(xbc, dt, cw, cb, dtb, a, dsk):
    bsz, seq, _ = xbc.shape
    vec = lambda wd: pl.BlockSpec((1, wd), lambda b, c: (0, 0))
    chunk = lambda wd: pl.BlockSpec((1, _SSD_ROWS, wd), lambda b, c: (b, c, 0))
    return pl.pallas_call(
        _ssd_fwd_kernel,
        grid=(bsz, seq // _SSD_ROWS),
        in_specs=_halo_specs(_SSD_ROWS, seq, SSM_CONV_DIM) + [
            chunk(LANES), pl.BlockSpec((SSM_CONV, SSM_CONV_DIM), lambda b, c: (0, 0)), vec(SSM_CONV_DIM),
            vec(LANES), vec(LANES), vec(SSM_INNER)],
        out_specs=[chunk(SSM_INNER), chunk(SSM_CONV_DIM)],
        out_shape=[jax.ShapeDtypeStruct((bsz, seq, SSM_INNER), F32),
                   jax.ShapeDtypeStruct((bsz, seq, SSM_CONV_DIM), BF16)],
        scratch_shapes=[_SSD_CARRY, pltpu.VMEM((_SSD_ROWS, SSM_CONV_DIM), F32)],
        compiler_params=_cparams(("parallel", "arbitrary")),
        name="ssd_fwd",
    )(xbc, xbc, xbc, dt, cw, cb, dtb, a, dsk)


def _ssd_bwd(xconv, dt, dtb, a, yf, zg, sg):
    bsz, seq, _ = xconv.shape
    nc = seq // _SSD_ROWS
    vec = lambda wd: pl.BlockSpec((1, wd), lambda b, c: (0, 0))
    chunk = lambda wd: pl.BlockSpec((1, _SSD_ROWS, wd), lambda b, c: (b, nc - 1 - c, 0))
    return pl.pallas_call(
        _ssd_bwd_kernel,
        grid=(bsz, nc),
        in_specs=[chunk(SSM_CONV_DIM), chunk(LANES), vec(LANES), vec(LANES), chunk(SSM_INNER), chunk(SSM_INNER),
                  vec(SSM_INNER)],
        out_specs=chunk(SSM_INNER),
        out_shape=jax.ShapeDtypeStruct((bsz, seq, SSM_INNER), BF16),
        scratch_shapes=[_SSD_CARRY, pltpu.VMEM((_SSD_ROWS, SSM_INNER), F32)],
        compiler_params=_cparams(("parallel", "arbitrary")),
        name="ssd_bwd",
    )(xconv, dt, dtb, a, yf, zg, sg)


def _final_kernel(x1_ref, c_ref, d_ref, wo_ref, fg_ref, o_ref, *, parts):
    rows = x1_ref.shape[0] // parts
    groups = [slice(part * rows, (part + 1) * rows) for part in range(parts)]
    x2s = [x1_ref[r, :] + jnp.dot(c_ref[r, :], wo_ref[0:512, :], preferred_element_type=F32)
           + jnp.dot(d_ref[r, :], wo_ref[512:1024, :], preferred_element_type=F32) for r in groups]
    for r, x2 in zip(groups, x2s):
        o_ref[r, :] = _rmsnorm(x2, fg_ref[...])


def _final(x1, c2, d2, wo, fg, tm):
    t = x1.shape[0]
    row = lambda i: (i, 0)
    full = lambda i: (0, 0)
    half = pl.BlockSpec((tm, 512), row)
    return pl.pallas_call(
        functools.partial(_final_kernel, parts=2),
        grid=(t // tm,),
        in_specs=[pl.BlockSpec((tm, D_MODEL), row), half, half, pl.BlockSpec((D_MODEL, D_MODEL), full),
                  pl.BlockSpec((1, D_MODEL), full)],
        out_specs=pl.BlockSpec((tm, D_MODEL), row),
        out_shape=jax.ShapeDtypeStruct((t, D_MODEL), F32),
        compiler_params=_cparams(("parallel",)),
        name="outproj1_final",
    )(x1, c2, d2, wo, fg)


def _rope_tables(seq):
    inv = 1.0 / (ROPE_THETA ** (jnp.arange(0, HEAD_DIM, 2, dtype=F32) / HEAD_DIM))
    f = jnp.arange(seq, dtype=F32)[:, None] * inv[None, :]
    emb = jnp.concatenate([f, f, f, f], axis=-1)
    cos, sin = jnp.cos(emb), jnp.sin(emb)
    first_half = (jnp.arange(LANES) % HEAD_DIM) < HEAD_DIM // 2
    return cos, jnp.where(first_half, -sin, 0.0), jnp.where(first_half, 0.0, sin)


def _pair_perm():
    idx = []
    for j in range(SWA_HEADS // 2):
        idx += list(range(j * HEAD_DIM, (j + 1) * HEAD_DIM))
        idx += list(range((j + 4) * HEAD_DIM, (j + 5) * HEAD_DIM))
    return jnp.asarray(idx, jnp.int32)


def _pad_lanes(v, offset):
    return jnp.zeros((1, LANES), F32).at[0, offset:offset + v.shape[0]].set(v.astype(F32))


def _prepare(norm_g, w_in0, conv_w, conv_b, conv_ln_g, conv_ln_b, sink, w_out0, w_in1, lambda_q1, lambda_k1,
             lambda_q2, lambda_k2, diff_norm_g, ssm_conv_w, ssm_conv_b, dt_bias_f, dt_bias_b, a_log_f, a_log_b,
             d_skip, ssm_norm_g, w_out1, final_norm_g):
    perm = _pair_perm()
    w0 = w_in0[0]
    w0 = w0.at[:, 1536:2048].set(w0[:, 1536:2048][:, perm]).at[:, 2304:2816].set(w0[:, 2304:2816][:, perm])
    wo0 = w_out0[0].at[512:1024, :].set(w_out0[0][512:1024, :][perm, :])
    w1 = jnp.pad(w_in1[0], ((0, 0), (0, IN_ODD_PAD - IN_ODD)))
    row = lambda v: v.astype(F32).reshape(1, -1)
    return dict(
        g0=row(norm_g[0]), g1=row(norm_g[1]), w0=w0.astype(BF16), wo0=wo0.astype(BF16), w1=w1.astype(BF16),
        wo1=w_out1[0].astype(BF16), cb=row(conv_b[0]), lg=row(conv_ln_g[0]),
        cw=jnp.broadcast_to(conv_w[0].astype(F32)[:, None, :], (CONV_WIDTH, SUBLANES, CONV_CH)),
        lb=row(conv_ln_b[0]), sink=sink[0].astype(F32),
        lq1=row(lambda_q1[0]), lk1=row(lambda_k1[0]), lq2=row(lambda_q2[0]), lk2=row(lambda_k2[0]),
        dg=row(diff_norm_g[0]), scw=ssm_conv_w[0].astype(F32), scb=row(ssm_conv_b[0]),
        dtb_f=_pad_lanes(dt_bias_f[0], 0), dtb_b=_pad_lanes(dt_bias_b[0], SSM_HEADS),
        a_f=_pad_lanes(-jnp.exp(a_log_f[0].astype(F32)), 0), a_b=_pad_lanes(-jnp.exp(a_log_b[0].astype(F32)), SSM_HEADS),
        dsk=row(jnp.repeat(d_skip[0].astype(F32), HEAD_DIM)), sg=row(ssm_norm_g[0]), fg=row(final_norm_g))


def _trunk(x, p, tm=512):
    bsz, seq, _ = x.shape
    assert seq % tm == 0 and seq % 512 == 0
    t = bsz * seq
    lam_init = 0.8 - 0.6 * math.exp(-0.3 * 1)
    cos, sa, sb = _rope_tables(seq)
    x2 = x.reshape(t, D_MODEL)
    apre, ga, q, k, v, gb = _inproj0(x2, p["g0"], p["w0"], cos, sa, sb, seq, tm)
    r3 = lambda arr: arr.reshape(bsz, seq, arr.shape[-1])
    a = _convmod(r3(apre), r3(ga), p["cw"], p["cb"], p["lg"], p["lb"])
    o = _swa(p["sink"], r3(q), r3(k), r3(v), r3(gb))
    x1, q1, k1, v1, cg, zg, xbc, dt = _mid(x2, a.reshape(t, 512), o.reshape(t, 512), p["wo0"], p["g1"], p["w1"],
                                           cos, sa, sb, seq, tm)
    c = _diff_attn(p["lq1"], p["lk1"], p["lq2"], p["lk2"], p["dg"], r3(q1), r3(k1), r3(v1), r3(cg), lam_init)
    yf, xconv = _ssd_fwd(r3(xbc), r3(dt), p["scw"], p["scb"], p["dtb_f"], p["a_f"], p["dsk"])
    d = _ssd_bwd(xconv, r3(dt), p["dtb_b"], p["a_b"], yf, r3(zg), p["sg"])
    out = _final(x1, c.reshape(t, 512), d.reshape(t, 512), p["wo1"], p["fg"], tm)
    return out.reshape(bsz, seq, D_MODEL)


def kernel(x_prompt, x_sample, norm_g, w_in0, conv_w, conv_b, conv_ln_g, conv_ln_b, sink, w_out0, w_in1, lambda_q1, lambda_k1, lambda_q2, lambda_k2, diff_norm_g, ssm_conv_w, ssm_conv_b, dt_bias_f, dt_bias_b, a_log_f, a_log_b, d_skip, ssm_norm_g, w_out1, final_norm_g):
    p = _prepare(norm_g, w_in0, conv_w, conv_b, conv_ln_g, conv_ln_b, sink, w_out0, w_in1, lambda_q1, lambda_k1,
                 lambda_q2, lambda_k2, diff_norm_g, ssm_conv_w, ssm_conv_b, dt_bias_f, dt_bias_b, a_log_f, a_log_b,
                 d_skip, ssm_norm_g, w_out1, final_norm_g)
    return (_trunk(x_prompt, p), _trunk(x_sample, p))
```

```python
import functools
import math

import jax
import jax.numpy as jnp
from jax import lax
from jax.experimental import pallas as pl
from jax.experimental.pallas import tpu as pltpu

F32 = jnp.float32
BF16 = jnp.bfloat16

D_MODEL = 1024
EPS = 1e-6
ROPE_THETA = 10000.0
HEAD_DIM = 64
LANES = 128
SUBLANES = 8
HALO = 16
CONV_CH = 512
CONV_WIDTH = 31
SWA_HEADS = 8
BLOCK = 128
DIFF_HEADS = 4
SSM_INNER = 512
SSM_HEADS = 8
SSM_STATE = 128
SSM_CONV = 5
SSM_CONV_DIM = 1024
IN_EVEN = 2816
IN_ODD = 3600
IN_ODD_PAD = 3712
NEG_BIG = -1e30
LOG2E = math.log2(math.e)
VMEM_LIMIT = 56 * 1024 * 1024


def _cparams(sem):
    return pltpu.CompilerParams(dimension_semantics=sem, vmem_limit_bytes=VMEM_LIMIT)


def _sigmoid(x):
    return 1.0 / (1.0 + jnp.exp2(x * (-LOG2E)))


def _silu(x):
    return x * _sigmoid(x)


def _rmsnorm(x, g):
    return x * lax.rsqrt(jnp.mean(x * x, axis=-1, keepdims=True) + EPS) * g


def _rope_cols(x, cos, sa, sb):
    outs = []
    for j in range(x.shape[1] // LANES):
        xj = x[:, j * LANES:(j + 1) * LANES]
        outs.append(xj * cos + pltpu.roll(xj, LANES - 32, 1) * sa + pltpu.roll(xj, 32, 1) * sb)
    return outs[0] if len(outs) == 1 else jnp.concatenate(outs, axis=1)


def _inproj0_kernel(x_ref, g_ref, w_ref, cos_ref, sa_ref, sb_ref,
                    apre_ref, ga_ref, q_ref, k_ref, v_ref, gb_ref):
    parts = 2
    rows = x_ref.shape[0] // parts
    groups = [slice(part * rows, (part + 1) * rows) for part in range(parts)]
    hbs = [_rmsnorm(x_ref[r, :], g_ref[...]).astype(BF16) for r in groups]
    for r, hb in zip(groups, hbs):
        def mm(a, b):
            return jnp.dot(hb, w_ref[:, a:b], preferred_element_type=F32)

        cos, sa, sb = cos_ref[r, :], sa_ref[r, :], sb_ref[r, :]
        apre_ref[r, :] = (mm(0, 512) * _sigmoid(mm(512, 1024))).astype(BF16)
        ga_ref[r, :] = _silu(mm(1024, 1536)).astype(BF16)
        gb_ref[r, :] = _silu(mm(2304, 2816)).astype(BF16)
        q_ref[r, :] = (_rope_cols(mm(1536, 2048), cos, sa, sb) * (HEAD_DIM ** -0.5 * LOG2E)).astype(BF16)
        k_ref[r, :] = _rope_cols(mm(2048, 2176), cos, sa, sb).astype(BF16)
        v_ref[r, :] = mm(2176, 2304).astype(BF16)


def _inproj0(x2, g, w, cos, sa, sb, seq, tm):
    t = x2.shape[0]
    ns = seq // tm
    row = lambda i: (i, 0)
    pos = lambda i: (i % ns, 0)
    full = lambda i: (0, 0)
    widths = (512, 512, 512, 128, 128, 512)
    return pl.pallas_call(
        _inproj0_kernel,
        grid=(t // tm,),
        in_specs=[pl.BlockSpec((tm, D_MODEL), row), pl.BlockSpec((1, D_MODEL), full),
                  pl.BlockSpec((D_MODEL, IN_EVEN), full),
                  pl.BlockSpec((tm, LANES), pos), pl.BlockSpec((tm, LANES), pos), pl.BlockSpec((tm, LANES), pos)],
        out_specs=[pl.BlockSpec((tm, wd), row) for wd in widths],
        out_shape=[jax.ShapeDtypeStruct((t, wd), BF16) for wd in widths],
        compiler_params=_cparams(("parallel",)),
        name="inproj0",
    )(x2, g, w, cos, sa, sb)


def _convmod_kernel(xc_ref, xp_ref, xn_ref, ga_ref, cw_ref, cb_ref, lg_ref, lb_ref, o_ref, rol_ref, *, ts, rs):
    i = pl.program_id(1)
    last = pl.num_programs(1) - 1
    xp, xn = xp_ref[0], xn_ref[0]
    padded = jnp.concatenate([jnp.where(i > 0, xp, jnp.zeros_like(xp)), xc_ref[0],
                              jnp.where(i < last, xn, jnp.zeros_like(xn))], axis=0)
    rows = ts + 2 * HALO
    rol_ref[0] = padded.astype(F32)
    for rb in range(0, rows, BLOCK):
        nr = min(BLOCK, rows - rb)
        nk = min(BLOCK + HALO, rows - rb)
        r_i = lax.broadcasted_iota(jnp.int32, (nr, nk), 0)
        c_i = lax.broadcasted_iota(jnp.int32, (nr, nk), 1)
        window = padded[rb:rb + nk]
        for j in range(1, SUBLANES):
            shift = jnp.where(c_i == r_i + j, 1.0, 0.0).astype(BF16)
            rol_ref[j, rb:rb + nr] = jnp.dot(shift, window, preferred_element_type=F32)
    first = HALO - (CONV_WIDTH - 1) // 2

    for r0 in range(0, ts, rs):
        acc = jnp.broadcast_to(cb_ref[...], (rs, CONV_CH))
        for w in range(CONV_WIDTH):
            a, j = divmod(first + w, SUBLANES)
            lo = r0 + a * SUBLANES
            acc = acc + rol_ref[j, lo:lo + rs, :] * jnp.concatenate([cw_ref[w]] * (rs // SUBLANES), axis=0)
        mu = jnp.mean(acc, axis=-1, keepdims=True)
        xc = acc - mu
        var = jnp.mean(xc * xc, axis=-1, keepdims=True)
        y = xc * lax.rsqrt(var + EPS) * lg_ref[...] + lb_ref[...]
        o_ref[0, r0:r0 + rs, :] = (_silu(y) * ga_ref[0, r0:r0 + rs, :].astype(F32)).astype(BF16)


def _halo_specs(ts, seq, width):
    hb = ts // HALO
    nh = seq // HALO
    return [pl.BlockSpec((1, ts, width), lambda b, i: (b, i, 0)),
            pl.BlockSpec((1, HALO, width), lambda b, i: (b, jnp.maximum(i * hb - 1, 0), 0)),
            pl.BlockSpec((1, HALO, width), lambda b, i: (b, jnp.minimum((i + 1) * hb, nh - 1), 0))]


def _convmod(apre, ga, cw, cb, lg, lb, ts=512, rs=32):
    bsz, seq, _ = apre.shape
    vec = pl.BlockSpec((1, CONV_CH), lambda b, i: (0, 0))
    return pl.pallas_call(
        functools.partial(_convmod_kernel, ts=ts, rs=rs),
        grid=(bsz, seq // ts),
        in_specs=_halo_specs(ts, seq, CONV_CH) + [
            pl.BlockSpec((1, ts, CONV_CH), lambda b, i: (b, i, 0)),
            pl.BlockSpec((CONV_WIDTH, SUBLANES, CONV_CH), lambda b, i: (0, 0, 0)), vec, vec, vec],
        out_specs=pl.BlockSpec((1, ts, CONV_CH), lambda b, i: (b, i, 0)),
        out_shape=jax.ShapeDtypeStruct((bsz, seq, CONV_CH), BF16),
        scratch_shapes=[pltpu.VMEM((SUBLANES, ts + 2 * HALO, CONV_CH), F32)],
        compiler_params=_cparams(("parallel", "parallel")),
        name="convmod",
    )(apre, apre, apre, ga, cw, cb, lg, lb)


def _swa_kernel(sink_ref, q_ref, kp_ref, kc_ref, kn_ref, vp_ref, vc_ref, vn_ref, gb_ref, o_ref, *, nblk):
    n = pl.program_id(1)
    last = pl.num_programs(1) - 1
    kcat = jnp.concatenate([kp_ref[0], kc_ref[0], kn_ref[0]], axis=0)
    vcat = jnp.concatenate([vp_ref[0], vc_ref[0], vn_ref[0]], axis=0)
    vaug = jnp.concatenate([vcat, jnp.ones_like(vcat)], axis=1)
    rows = lax.broadcasted_iota(jnp.int32, (2 * BLOCK, 3 * BLOCK), 0)
    qi = jnp.where(rows >= BLOCK, rows - BLOCK, rows)
    ci = lax.broadcasted_iota(jnp.int32, (2 * BLOCK, 3 * BLOCK), 1)
    rel = ci - BLOCK - qi
    band = (rel >= -BLOCK) & (rel <= BLOCK)
    first_key = jnp.where(n > 0, 0, BLOCK)
    end_key = jnp.where(n < last, 3 * BLOCK, 2 * BLOCK)
    lane = lax.broadcasted_iota(jnp.int32, (BLOCK, LANES), 1)
    low = lane < HEAD_DIM
    row1 = lax.broadcasted_iota(jnp.int32, (2 * BLOCK, 1), 0)
    npair = SWA_HEADS // 2
    sinks = [jnp.where(row1 < BLOCK, sink_ref[j], sink_ref[j + npair]) * LOG2E for j in range(npair)]
    masks = []
    for b in range(nblk):
        mask = band
        if b == 0:
            mask = mask & (ci >= first_key)
        if b == nblk - 1:
            mask = mask & (ci < end_key)
        masks.append(mask)
    csl = [slice(j * LANES, (j + 1) * LANES) for j in range(npair)]
    for b in range(nblk):
        rsl = slice(b * BLOCK, (b + 1) * BLOCK)
        kwin = kcat[b * BLOCK:(b + 3) * BLOCK]
        vwin = vaug[b * BLOCK:(b + 3) * BLOCK]
        qqs = []
        for j in range(npair):
            qv = q_ref[0, rsl, csl[j]]
            zero = jnp.zeros_like(qv)
            qqs.append(jnp.concatenate([jnp.where(low, qv, zero), jnp.where(low, zero, qv)], axis=0))
        ss = [jnp.where(masks[b], lax.dot_general(qq, kwin, (((1,), (1,)), ((), ())), preferred_element_type=F32),
                        NEG_BIG) for qq in qqs]
        ms = [jnp.maximum(jnp.max(ss[j], axis=-1, keepdims=True), sinks[j]) for j in range(npair)]
        pvs = [jnp.dot(jnp.exp2(ss[j] - ms[j]).astype(BF16), vwin, preferred_element_type=F32) for j in range(npair)]
        for j in range(npair):
            pv = pvs[j][:, 0:LANES] / (pvs[j][:, LANES:2 * LANES] + jnp.exp2(sinks[j] - ms[j]))
            o = jnp.where(low, pv[:BLOCK], pv[BLOCK:])
            o_ref[0, rsl, csl[j]] = (o * gb_ref[0, rsl, csl[j]].astype(F32)).astype(BF16)


def _swa(sink, q, k, v, gb, nblk=4):
    bsz, seq, _ = q.shape
    rows = nblk * BLOCK
    nb = seq // BLOCK
    cur = lambda b, n: (b, n, 0)
    prv = lambda b, n: (b, jnp.maximum(n * nblk - 1, 0), 0)
    nxt = lambda b, n: (b, jnp.minimum((n + 1) * nblk, nb - 1), 0)
    edge = lambda im: pl.BlockSpec((1, BLOCK, LANES), im)
    mid = pl.BlockSpec((1, rows, LANES), cur)
    wide = pl.BlockSpec((1, rows, 512), cur)
    return pl.pallas_call(
        functools.partial(_swa_kernel, nblk=nblk),
        grid=(bsz, seq // rows),
        in_specs=[pl.BlockSpec(memory_space=pltpu.SMEM), wide,
                  edge(prv), mid, edge(nxt), edge(prv), mid, edge(nxt), wide],
        out_specs=wide,
        out_shape=jax.ShapeDtypeStruct((bsz, seq, 512), BF16),
        compiler_params=_cparams(("parallel", "parallel")),
        name="swa",
    )(sink, q, k, k, k, v, v, v, gb)


def _mid_kernel(x_ref, a_ref, o_ref, wo_ref, g_ref, w_ref, cos_ref, sa_ref, sb_ref,
                x1_ref, q_ref, k_ref, v_ref, cg_ref, zg_ref, xbc_ref, dt_ref):
    parts = 2
    rows = x_ref.shape[0] // parts
    groups = [slice(part * rows, (part + 1) * rows) for part in range(parts)]
    hbs = []
    for r in groups:
        x1 = (x_ref[r, :] + jnp.dot(a_ref[r, :], wo_ref[0:512, :], preferred_element_type=F32)
              + jnp.dot(o_ref[r, :], wo_ref[512:1024, :], preferred_element_type=F32))
        x1_ref[r, :] = x1
        hbs.append(_rmsnorm(x1, g_ref[...]).astype(BF16))
    ones = jnp.ones((rows, LANES), BF16)
    for r, hb in zip(groups, hbs):
        def mm(a, b):
            return jnp.dot(hb, w_ref[:, a:b], preferred_element_type=F32)

        cos, sa, sb = cos_ref[r, :], sa_ref[r, :], sb_ref[r, :]
        q_ref[r, :] = (_rope_cols(mm(0, 512), cos, sa, sb) * (HEAD_DIM ** -0.5 * LOG2E)).astype(BF16)
        k_ref[r, :] = _rope_cols(mm(512, 1024), cos, sa, sb).astype(BF16)
        v = mm(1024, 1536).astype(BF16)
        for h in range(DIFF_HEADS):
            v_ref[r, 2 * h * LANES:(2 * h + 1) * LANES] = v[:, h * LANES:(h + 1) * LANES]
            v_ref[r, (2 * h + 1) * LANES:(2 * h + 2) * LANES] = ones
        dt_ref[r, :] = mm(3584, IN_ODD_PAD)
        cg_ref[r, :] = _silu(mm(1536, 2048)).astype(BF16)
        zg_ref[r, :] = _silu(mm(2048, 2560)).astype(BF16)
        xbc_ref[r, :] = mm(2560, 3584).astype(BF16)


def _mid(x2, a2, o2, wo, g, w, cos, sa, sb, seq, tm):
    t = x2.shape[0]
    ns = seq // tm
    row = lambda i: (i, 0)
    pos = lambda i: (i % ns, 0)
    full = lambda i: (0, 0)
    outs = ((D_MODEL, F32), (512, BF16), (512, BF16), (1024, BF16), (512, BF16), (512, BF16),
            (SSM_CONV_DIM, BF16), (LANES, F32))
    return pl.pallas_call(
        _mid_kernel,
        grid=(t // tm,),
        in_specs=[pl.BlockSpec((tm, D_MODEL), row), pl.BlockSpec((tm, 512), row), pl.BlockSpec((tm, 512), row),
                  pl.BlockSpec((D_MODEL, D_MODEL), full), pl.BlockSpec((1, D_MODEL), full),
                  pl.BlockSpec((D_MODEL, IN_ODD_PAD), full),
                  pl.BlockSpec((tm, LANES), pos), pl.BlockSpec((tm, LANES), pos), pl.BlockSpec((tm, LANES), pos)],
        out_specs=[pl.BlockSpec((tm, wd), row) for wd, _ in outs],
        out_shape=[jax.ShapeDtypeStruct((t, wd), dt) for wd, dt in outs],
        compiler_params=_cparams(("parallel",)),
        name="outproj0_inproj1",
    )(x2, a2, o2, wo, g, w, cos, sa, sb)


def _diff_kernel(lq1_ref, lk1_ref, lq2_ref, lk2_ref, g_ref, q_ref, qn_ref, k_ref, v_ref, cg_ref, o_ref,
                 s0_ref, s1_ref, mx0_ref, mx1_ref, m_ref, acc_ref, *, kc, unroll, lam_init):
    tq = q_ref.shape[1]
    nch = k_ref.shape[1] // kc
    lane = lax.broadcasted_iota(jnp.int32, (tq, LANES), 1)

    def stack_maps(q):
        zero = jnp.zeros_like(q)
        return jnp.concatenate([jnp.where(lane < HEAD_DIM, q, zero), jnp.where(lane < HEAD_DIM, zero, q)], axis=0)

    slots = ((s0_ref, mx0_ref), (s1_ref, mx1_ref))

    def scores(qs, c, slot):
        s_ref, mx_ref = slots[slot]
        kch = k_ref[0, pl.ds(pl.multiple_of(c * kc, kc), kc), :]
        s = lax.dot_general(qs, kch, (((1,), (1,)), ((), ())), preferred_element_type=F32)
        s_ref[...] = s
        mx_ref[...] = jnp.broadcast_to(jnp.max(s, axis=-1, keepdims=True), mx_ref.shape)

    def update(c, slot):
        s_ref, mx_ref = slots[slot]
        vch = v_ref[0, pl.ds(pl.multiple_of(c * kc, kc), kc), :]
        m_old = m_ref[...]
        m_new = jnp.maximum(m_old, mx_ref[...])
        alpha = jnp.exp2(m_old - m_new)
        p = jnp.exp2(s_ref[...] - jnp.concatenate([m_new] * (kc // LANES), axis=1)).astype(BF16)
        acc_ref[...] = acc_ref[...] * jnp.concatenate([alpha, alpha], axis=1) + jnp.dot(
            p, vch, preferred_element_type=F32)
        m_ref[...] = m_new

    @pl.when(pl.program_id(2) == 0)
    def _():
        scores(stack_maps(q_ref[0]), 0, 0)

    qq = stack_maps(q_ref[0])
    qq_next = stack_maps(qn_ref[0])
    m_ref[...] = jnp.full(m_ref.shape, NEG_BIG, F32)
    acc_ref[...] = jnp.zeros(acc_ref.shape, F32)
    ntrips = nch // unroll

    def body(t, carry):
        for u in range(unroll):
            c = unroll * t + u
            if u < unroll - 1:
                scores(qq, c + 1, (u + 1) % 2)
            elif ntrips == 1:
                scores(qq_next, 0, 0)
            else:
                wrap = t == ntrips - 1
                scores(jnp.where(wrap, qq_next, qq), jnp.where(wrap, 0, c + 1), 0)
            update(c, u % 2)
        return carry

    if ntrips == 1:
        body(0, 0)
    else:
        lax.fori_loop(0, ntrips, body, 0)

    lam = (jnp.exp(jnp.sum(lq1_ref[...] * lk1_ref[...], axis=-1, keepdims=True))
           - jnp.exp(jnp.sum(lq2_ref[...] * lk2_ref[...], axis=-1, keepdims=True)) + lam_init)
    o0 = acc_ref[0:tq, 0:LANES] / acc_ref[0:tq, LANES:2 * LANES]
    o1 = acc_ref[tq:2 * tq, 0:LANES] / acc_ref[tq:2 * tq, LANES:2 * LANES]
    o = o0 - lam * o1
    o = o * lax.rsqrt(jnp.mean(o * o, axis=-1, keepdims=True) + EPS) * g_ref[...] * (1.0 - lam_init)
    o_ref[0] = (o * cg_ref[0].astype(F32)).astype(BF16)


def _diff_attn(lq1, lk1, lq2, lk2, g, q, k, v, cg, lam_init, tq=512):
    bsz, seq, _ = q.shape
    kc = next(c for c in (1024, 512, 256) if seq % (2 * c) == 0)
    unroll = 4 if seq % (4 * kc) == 0 else 2
    nq = seq // tq
    small = pl.BlockSpec((1, HEAD_DIM), lambda b, h, i: (0, 0))
    tile = pl.BlockSpec((1, tq, LANES), lambda b, h, i: (b, i, h))
    next_tile = pl.BlockSpec((1, tq, LANES), lambda b, h, i: (b, jnp.minimum(i + 1, nq - 1), h))
    return pl.pallas_call(
        functools.partial(_diff_kernel, kc=kc, unroll=unroll, lam_init=lam_init),
        grid=(bsz, DIFF_HEADS, nq),
        in_specs=[small, small, small, small, pl.BlockSpec((1, LANES), lambda b, h, i: (0, 0)), tile, next_tile,
                  pl.BlockSpec((1, seq, LANES), lambda b, h, i: (b, 0, h)),
                  pl.BlockSpec((1, seq, 2 * LANES), lambda b, h, i: (b, 0, h)), tile],
        out_specs=tile,
        out_shape=jax.ShapeDtypeStruct((bsz, seq, 512), BF16),
        scratch_shapes=[pltpu.VMEM((2 * tq, kc), F32), pltpu.VMEM((2 * tq, kc), F32),
                        pltpu.VMEM((2 * tq, LANES), F32), pltpu.VMEM((2 * tq, LANES), F32),
                        pltpu.VMEM((2 * tq, LANES), F32), pltpu.VMEM((2 * tq, 2 * LANES), F32)],
        compiler_params=_cparams(("parallel", "parallel", "arbitrary")),
        name="diff_attn",
    )(lq1, lk1, lq2, lk2, g, q, q, k, v, cg)


def _split3(x):
    hi = x.astype(BF16)
    r = x - hi.astype(F32)
    mid = r.astype(BF16)
    lo = (r - mid.astype(F32)).astype(BF16)
    return hi, mid, lo


def _ssd_fwd_kernel(xc_ref, xp_ref, xn_ref, dt_ref, cw_ref, cb_ref, dtb_ref, a_ref, dsk_ref, y_ref, xconv_ref,
                    carry_ref, xf_ref):
    L = BLOCK
    nchunk = xc_ref.shape[1] // L
    c = pl.program_id(1)
    last = pl.num_programs(1) - 1

    @pl.when(c == 0)
    def _():
        carry_ref[...] = jnp.zeros(carry_ref.shape, F32)

    xp, xn = xp_ref[0], xn_ref[0]
    halo_p = jnp.where(c > 0, xp, jnp.zeros_like(xp))
    halo_n = jnp.where(c < last, xn, jnp.zeros_like(xn))
    r_i = lax.broadcasted_iota(jnp.int32, (L, L + 2 * HALO), 0)
    c_i = lax.broadcasted_iota(jnp.int32, (L, L + 2 * HALO), 1)
    first = HALO - (SSM_CONV - 1) // 2
    shifts = [None if first + w == HALO else jnp.where(c_i == r_i + first + w, 1.0, 0.0).astype(BF16)
              for w in range(SSM_CONV)]
    wd = 2 * LANES
    for nb in range(SSM_CONV_DIM // wd):
        cols = slice(nb * wd, (nb + 1) * wd)
        padded = jnp.concatenate([halo_p[:, cols], xc_ref[0, :, cols], halo_n[:, cols]], axis=0)
        for ci in range(nchunk):
            r0 = ci * L
            window = padded[r0:r0 + L + 2 * HALO]
            acc = jnp.broadcast_to(cb_ref[:, cols], (L, wd))
            for w in range(SSM_CONV):
                tap = (xc_ref[0, r0:r0 + L, cols].astype(F32) if shifts[w] is None
                       else jnp.dot(shifts[w], window, preferred_element_type=F32))
                acc = acc + tap * cw_ref[w:w + 1, cols]
            xbc = _silu(acc)
            xconv_ref[0, r0:r0 + L, cols] = xbc.astype(BF16)
            xf_ref[r0:r0 + L, cols] = xbc

    def xcols(k, a, b):
        return xf_ref[k * L:(k + 1) * L, a:b]

    def emit(k, j, y):
        cols = slice(j * LANES, (j + 1) * LANES)
        y_ref[0, k * L:(k + 1) * L, cols] = y + xcols(k, j * LANES, (j + 1) * LANES) * dsk_ref[:, cols]

    _ssd_scan_block(xcols, list(range(nchunk)), dt_ref, dtb_ref, a_ref, carry_ref, emit, reverse=False, col0=0)


def _ssd_bwd_kernel(xconv_ref, dt_ref, dtb_ref, a_ref, yf_ref, zg_ref, sg_ref, x1_ref, c_ref, wo_ref, fg_ref, o_ref,
                    carry_ref, yb_ref):
    L = BLOCK
    nchunk = xconv_ref.shape[1] // L

    @pl.when(pl.program_id(1) == 0)
    def _():
        carry_ref[...] = jnp.zeros(carry_ref.shape, F32)

    x2 = x1_ref[0] + jnp.dot(c_ref[0], wo_ref[0:512, :], preferred_element_type=F32)

    def xcols(k, a, b):
        return xconv_ref[0, k * L:(k + 1) * L, a:b].astype(F32)

    def emit(k, j, y):
        yb_ref[k * L:(k + 1) * L, j * LANES:(j + 1) * LANES] = y

    _ssd_scan_block(xcols, list(reversed(range(nchunk))), dt_ref, dtb_ref, a_ref, carry_ref, emit,
                    reverse=True, col0=SSM_HEADS)
    y = (yf_ref[0] + yb_ref[...]) * zg_ref[0].astype(F32)
    d = _rmsnorm(y, sg_ref[...]).astype(BF16)
    x2 = x2 + jnp.dot(d, wo_ref[512:1024, :], preferred_element_type=F32)
    o_ref[0] = _rmsnorm(x2, fg_ref[...])


def _ssd_scan_block(xcols, order, dt_ref, dtb_ref, a_ref, carry_ref, emit, *, reverse, col0):
    L = BLOCK
    nchunk = len(order)
    chunk = lambda m, k: m[k * L:(k + 1) * L]

    z = dt_ref[0] + dtb_ref[...]
    dt = jnp.maximum(z, 0.0) + jnp.log(1.0 + jnp.exp(-jnp.abs(z)))
    a = dt * a_ref[...]
    r_i = lax.broadcasted_iota(jnp.int32, (L, L), 0)
    c_i = lax.broadcasted_iota(jnp.int32, (L, L), 1)
    causal = (r_i <= c_i) if reverse else (r_i >= c_i)
    tri = jnp.where(causal, 1.0, 0.0).astype(BF16)
    pieces = _split3(a * LOG2E)
    acum = [sum(jnp.dot(tri, chunk(pc, k), preferred_element_type=F32) for pc in pieces) for k in range(nchunk)]
    acum_t = [m.T for m in acum]
    edge = 0 if reverse else L - 1
    a_end = [m[edge:edge + 1, :] for m in acum]

    lane = lax.broadcasted_iota(jnp.int32, (L, LANES), 1)
    low = lane < HEAD_DIM
    lane1 = lax.broadcasted_iota(jnp.int32, (1, LANES), 1)

    cbs, bts, cms = [], [], []
    for k in range(nchunk):
        bm = [xcols(k, SSM_INNER + g * SSM_STATE, SSM_INNER + (g + 1) * SSM_STATE) for g in range(2)]
        cm = [xcols(k, SSM_INNER + (2 + g) * SSM_STATE, SSM_INNER + (3 + g) * SSM_STATE).astype(BF16)
              for g in range(2)]
        cbs.append([lax.dot_general(cm[g], bm[g].astype(BF16), (((1,), (1,)), ((), ())),
                                    preferred_element_type=F32) for g in range(2)])
        bts.append([bm[g].T.astype(BF16) for g in range(2)])
        cms.append(cm)
    for j in range(SSM_HEADS // 2):
        g = j // 2
        ca = col0 + 2 * j
        ydiag, states, decay_out, decay_in = [], [], [], []
        for k in range(nchunk):
            acb = [jnp.broadcast_to(acum[k][:, ca + hh:ca + hh + 1], (L, LANES)) for hh in range(2)]
            ac_pair = jnp.where(low, acb[0], acb[1])
            dtk = chunk(dt, k)
            dt_pair = jnp.where(low, dtk[:, ca:ca + 1], dtk[:, ca + 1:ca + 2])
            end_pair = jnp.where(lane1 < HEAD_DIM, a_end[k][:, ca:ca + 1], a_end[k][:, ca + 1:ca + 2])
            xdt = xcols(k, j * LANES, (j + 1) * LANES) * dt_pair
            xdt_b = xdt.astype(BF16)
            zero = jnp.zeros_like(xdt_b)
            ms = []
            for hh in range(2):
                seg = acb[hh] - acum_t[k][ca + hh:ca + hh + 1, :]
                dec = jnp.exp2(jnp.where(causal, seg, NEG_BIG))
                ms.append((cbs[k][g] * dec).astype(BF16))
            lhs = jnp.concatenate(ms, axis=1)
            rhs = jnp.concatenate([jnp.where(low, xdt_b, zero), jnp.where(low, zero, xdt_b)], axis=0)
            ydiag.append(jnp.dot(lhs, rhs, preferred_element_type=F32))
            states.append(jnp.dot(bts[k][g], (xdt * jnp.exp2(end_pair - ac_pair)).astype(BF16),
                                  preferred_element_type=F32))
            decay_out.append(jnp.exp2(end_pair))
            decay_in.append(jnp.exp2(ac_pair))
        carry = carry_ref[j]
        for k in order:
            emit(k, j, ydiag[k] + jnp.dot(cms[k][g], carry.astype(BF16), preferred_element_type=F32) * decay_in[k])
            carry = carry * decay_out[k] + states[k]
        carry_ref[j] = carry


_SSD_CARRY = pltpu.VMEM((SSM_HEADS // 2, SSM_STATE, LANES), F32)
_SSD_ROWS = 4 * BLOCK


def _ssd_fwd(xbc, dt, cw, cb, dtb, a, dsk):
    bsz, seq, _ = xbc.shape
    vec = lambda wd: pl.BlockSpec((1, wd), lambda b, c: (0, 0))
    chunk = lambda wd: pl.BlockSpec((1, _SSD_ROWS, wd), lambda b, c: (b, c, 0))
    return pl.pallas_call(
        _ssd_fwd_kernel,
        grid=(bsz, seq // _SSD_ROWS),
        in_specs=_halo_specs(_SSD_ROWS, seq, SSM_CONV_DIM) + [
            chunk(LANES), pl.BlockSpec((SSM_CONV, SSM_CONV_DIM), lambda b, c: (0, 0)), vec(SSM_CONV_DIM),
            vec(LANES), vec(LANES), vec(SSM_INNER)],
        out_specs=[chunk(SSM_INNER), chunk(SSM_CONV_DIM)],
        out_shape=[jax.ShapeDtypeStruct((bsz, seq, SSM_INNER), F32),
                   jax.ShapeDtypeStruct((bsz, seq, SSM_CONV_DIM), BF16)],
        scratch_shapes=[_SSD_CARRY, pltpu.VMEM((_SSD_ROWS, SSM_CONV_DIM), F32)],
        compiler_params=_cparams(("parallel", "arbitrary")),
        name="ssd_fwd",
    )(xbc, xbc, xbc, dt, cw, cb, dtb, a, dsk)


def _ssd_bwd_final(xconv, dt, dtb, a, yf, zg, sg, x1, c, wo, fg):
    bsz, seq, _ = xconv.shape
    nc = seq // _SSD_ROWS
    vec = lambda wd: pl.BlockSpec((1, wd), lambda b, c: (0, 0))
    chunk = lambda wd: pl.BlockSpec((1, _SSD_ROWS, wd), lambda b, c: (b, nc - 1 - c, 0))
    return pl.pallas_call(
        _ssd_bwd_kernel,
        grid=(bsz, nc),
        in_specs=[chunk(SSM_CONV_DIM), chunk(LANES), vec(LANES), vec(LANES), chunk(SSM_INNER), chunk(SSM_INNER),
                  vec(SSM_INNER), chunk(D_MODEL), chunk(512), pl.BlockSpec((D_MODEL, D_MODEL), lambda b, c: (0, 0)),
                  vec(D_MODEL)],
        out_specs=chunk(D_MODEL),
        out_shape=jax.ShapeDtypeStruct((bsz, seq, D_MODEL), F32),
        scratch_shapes=[_SSD_CARRY, pltpu.VMEM((_SSD_ROWS, SSM_INNER), F32)],
        compiler_params=_cparams(("parallel", "arbitrary")),
        name="ssd_bwd_final",
    )(xconv, dt, dtb, a, yf, zg, sg, x1, c, wo, fg)


def _rope_tables(seq):
    inv = 1.0 / (ROPE_THETA ** (jnp.arange(0, HEAD_DIM, 2, dtype=F32) / HEAD_DIM))
    f = jnp.arange(seq, dtype=F32)[:, None] * inv[None, :]
    cos, sin = jnp.cos(f), jnp.sin(f)
    cos = jnp.concatenate([cos] * 4, axis=-1)
    sin = jnp.concatenate([sin] * 4, axis=-1)
    first_half = (jnp.arange(LANES) % HEAD_DIM) < HEAD_DIM // 2
    return cos, jnp.where(first_half, -sin, 0.0), jnp.where(first_half, 0.0, sin)


def _pair_perm():
    idx = []
    for j in range(SWA_HEADS // 2):
        idx += list(range(j * HEAD_DIM, (j + 1) * HEAD_DIM))
        idx += list(range((j + 4) * HEAD_DIM, (j + 5) * HEAD_DIM))
    return jnp.asarray(idx, jnp.int32)


def _pad_lanes(v, offset):
    return jnp.zeros((1, LANES), F32).at[0, offset:offset + v.shape[0]].set(v.astype(F32))


def _prepare(norm_g, w_in0, conv_w, conv_b, conv_ln_g, conv_ln_b, sink, w_out0, w_in1, lambda_q1, lambda_k1,
             lambda_q2, lambda_k2, diff_norm_g, ssm_conv_w, ssm_conv_b, dt_bias_f, dt_bias_b, a_log_f, a_log_b,
             d_skip, ssm_norm_g, w_out1, final_norm_g):
    perm = _pair_perm()
    w0 = w_in0[0].astype(BF16)
    w0 = jnp.concatenate([w0[:, :1536], w0[:, 1536:2048][:, perm], w0[:, 2048:2304], w0[:, 2304:2816][:, perm]],
                         axis=1)
    wo0 = w_out0[0].astype(BF16)
    wo0 = jnp.concatenate([wo0[:512], wo0[512:1024][perm]], axis=0)
    w1 = jnp.pad(w_in1[0].astype(BF16), ((0, 0), (0, IN_ODD_PAD - IN_ODD)))
    row = lambda v: v.astype(F32).reshape(1, -1)
    return dict(
        g0=row(norm_g[0]), g1=row(norm_g[1]), w0=w0, wo0=wo0, w1=w1,
        wo1=w_out1[0].astype(BF16), cb=row(conv_b[0]), lg=row(conv_ln_g[0]),
        cw=jnp.broadcast_to(conv_w[0].astype(F32)[:, None, :], (CONV_WIDTH, SUBLANES, CONV_CH)),
        lb=row(conv_ln_b[0]), sink=sink[0].astype(F32),
        lq1=row(lambda_q1[0]), lk1=row(lambda_k1[0]), lq2=row(lambda_q2[0]), lk2=row(lambda_k2[0]),
        dg=row(diff_norm_g[0]), scw=ssm_conv_w[0].astype(F32), scb=row(ssm_conv_b[0]),
        dtb_f=_pad_lanes(dt_bias_f[0], 0), dtb_b=_pad_lanes(dt_bias_b[0], SSM_HEADS),
        a_f=_pad_lanes(-jnp.exp(a_log_f[0].astype(F32)), 0), a_b=_pad_lanes(-jnp.exp(a_log_b[0].astype(F32)), SSM_HEADS),
        dsk=row(jnp.repeat(d_skip[0].astype(F32), HEAD_DIM)), sg=row(ssm_norm_g[0]), fg=row(final_norm_g))


def _trunk(x, p, tm=512):
    bsz, seq, _ = x.shape
    assert seq % tm == 0 and seq % 512 == 0
    t = bsz * seq
    lam_init = 0.8 - 0.6 * math.exp(-0.3 * 1)
    cos, sa, sb = _rope_tables(seq)
    x2 = x.reshape(t, D_MODEL)
    apre, ga, q, k, v, gb = _inproj0(x2, p["g0"], p["w0"], cos, sa, sb, seq, tm)
    r3 = lambda arr: arr.reshape(bsz, seq, arr.shape[-1])
    a = _convmod(r3(apre), r3(ga), p["cw"], p["cb"], p["lg"], p["lb"])
    o = _swa(p["sink"], r3(q), r3(k), r3(v), r3(gb))
    x1, q1, k1, v1, cg, zg, xbc, dt = _mid(x2, a.reshape(t, 512), o.reshape(t, 512), p["wo0"], p["g1"], p["w1"],
                                           cos, sa, sb, seq, tm)
    c = _diff_attn(p["lq1"], p["lk1"], p["lq2"], p["lk2"], p["dg"], r3(q1), r3(k1), r3(v1), r3(cg), lam_init)
    yf, xconv = _ssd_fwd(r3(xbc), r3(dt), p["scw"], p["scb"], p["dtb_f"], p["a_f"], p["dsk"])
    return _ssd_bwd_final(xconv, r3(dt), p["dtb_b"], p["a_b"], yf, r3(zg), p["sg"], r3(x1), c, p["wo1"], p["fg"])


def kernel(x_prompt, x_sample, norm_g, w_in0, conv_w, conv_b, conv_ln_g, conv_ln_b, sink, w_out0, w_in1, lambda_q1, lambda_k1, lambda_q2, lambda_k2, diff_norm_g, ssm_conv_w, ssm_conv_b, dt_bias_f, dt_bias_b, a_log_f, a_log_b, d_skip, ssm_norm_g, w_out1, final_norm_g):
    p = _prepare(norm_g, w_in0, conv_w, conv_b, conv_ln_g, conv_ln_b, sink, w_out0, w_in1, lambda_q1, lambda_k1,
                 lambda_q2, lambda_k2, diff_norm_g, ssm_conv_w, ssm_conv_b, dt_bias_f, dt_bias_b, a_log_f, a_log_b,
                 d_skip, ssm_norm_g, w_out1, final_norm_g)
    return (_trunk(x_prompt, p), _trunk(x_sample, p))
```

```python
import functools
import math

import jax
import jax.numpy as jnp
from jax import lax
from jax.experimental import pallas as pl
from jax.experimental.pallas import tpu as pltpu

F32 = jnp.float32
BF16 = jnp.bfloat16

D_MODEL = 1024
EPS = 1e-6
ROPE_THETA = 10000.0
HEAD_DIM = 64
LANES = 128
SUBLANES = 8
HALO = 16
CONV_CH = 512
CONV_WIDTH = 31
SWA_HEADS = 8
BLOCK = 128
DIFF_HEADS = 4
SSM_INNER = 512
SSM_HEADS = 8
SSM_STATE = 128
SSM_CONV = 5
SSM_CONV_DIM = 1024
HALF = D_MODEL // 2


def _col_ranges(**widths):
    out, start = {}, 0
    for name, width in widths.items():
        out[name] = (start, start + width)
        start += width
    return out


EVEN = _col_ranges(a_val=CONV_CH, a_glu=CONV_CH, a_gate=CONV_CH, q=HALF, k=LANES, v=LANES, b_gate=HALF)
ODD = _col_ranges(q=HALF, k=HALF, v=HALF, c_gate=HALF, z=SSM_INNER, xbc=SSM_CONV_DIM, dt=LANES)
IN_EVEN = EVEN["b_gate"][1]
IN_ODD = ODD["dt"][0] + 2 * SSM_HEADS
IN_ODD_PAD = ODD["dt"][1]
NEG_BIG = -1e30
LOG2E = math.log2(math.e)
VMEM_LIMIT = 56 * 1024 * 1024


def _cparams(sem):
    return pltpu.CompilerParams(dimension_semantics=sem, vmem_limit_bytes=VMEM_LIMIT)


def _sigmoid(x):
    return 1.0 / (1.0 + jnp.exp2(x * (-LOG2E)))


def _silu(x):
    return x * _sigmoid(x)


def _rmsnorm(x, g):
    return x * lax.rsqrt(jnp.mean(x * x, axis=-1, keepdims=True) + EPS) * g


def _rope_cols(x, cos, sa, sb):
    outs = []
    for j in range(x.shape[1] // LANES):
        xj = x[:, j * LANES:(j + 1) * LANES]
        outs.append(xj * cos + pltpu.roll(xj, LANES - 32, 1) * sa + pltpu.roll(xj, 32, 1) * sb)
    return outs[0] if len(outs) == 1 else jnp.concatenate(outs, axis=1)


def _inproj0_kernel(x_ref, g_ref, w_ref, cos_ref, sa_ref, sb_ref,
                    apre_ref, ga_ref, q_ref, k_ref, v_ref, gb_ref):
    parts = 2
    rows = x_ref.shape[0] // parts
    groups = [slice(part * rows, (part + 1) * rows) for part in range(parts)]
    hbs = [_rmsnorm(x_ref[r, :], g_ref[...]).astype(BF16) for r in groups]
    for r, hb in zip(groups, hbs):
        def mm(a, b):
            return jnp.dot(hb, w_ref[:, a:b], preferred_element_type=F32)

        cos, sa, sb = cos_ref[r, :], sa_ref[r, :], sb_ref[r, :]
        apre_ref[r, :] = (mm(*EVEN["a_val"]) * _sigmoid(mm(*EVEN["a_glu"]))).astype(BF16)
        ga_ref[r, :] = _silu(mm(*EVEN["a_gate"])).astype(BF16)
        gb_ref[r, :] = _silu(mm(*EVEN["b_gate"])).astype(BF16)
        q_ref[r, :] = (_rope_cols(mm(*EVEN["q"]), cos, sa, sb) * (HEAD_DIM ** -0.5 * LOG2E)).astype(BF16)
        kv = mm(EVEN["k"][0], EVEN["v"][1])
        k_ref[r, :] = _rope_cols(kv[:, 0:LANES], cos, sa, sb).astype(BF16)
        v_ref[r, :] = kv[:, LANES:2 * LANES].astype(BF16)


def _inproj0(x2, g, w, cos, sa, sb, seq, tm):
    t = x2.shape[0]
    ns = seq // tm
    row = lambda i: (i, 0)
    pos = lambda i: (i % ns, 0)
    full = lambda i: (0, 0)
    widths = (CONV_CH, CONV_CH, HALF, LANES, LANES, HALF)
    return pl.pallas_call(
        _inproj0_kernel,
        grid=(t // tm,),
        in_specs=[pl.BlockSpec((tm, D_MODEL), row), pl.BlockSpec((1, D_MODEL), full),
                  pl.BlockSpec((D_MODEL, IN_EVEN), full),
                  pl.BlockSpec((tm, LANES), pos), pl.BlockSpec((tm, LANES), pos), pl.BlockSpec((tm, LANES), pos)],
        out_specs=[pl.BlockSpec((tm, wd), row) for wd in widths],
        out_shape=[jax.ShapeDtypeStruct((t, wd), BF16) for wd in widths],
        compiler_params=_cparams(("parallel",)),
        name="inproj0",
    )(x2, g, w, cos, sa, sb)


def _convmod_kernel(xc_ref, xp_ref, xn_ref, ga_ref, cw_ref, cb_ref, lg_ref, lb_ref, o_ref, rol_ref, *, ts, rs):
    i = pl.program_id(1)
    last = pl.num_programs(1) - 1
    xp, xn = xp_ref[0], xn_ref[0]
    padded = jnp.concatenate([jnp.where(i > 0, xp, jnp.zeros_like(xp)), xc_ref[0],
                              jnp.where(i < last, xn, jnp.zeros_like(xn))], axis=0)
    rows = ts + 2 * HALO
    rol_ref[0] = padded.astype(F32)
    for rb in range(0, rows, BLOCK):
        nr = min(BLOCK, rows - rb)
        nk = min(BLOCK + HALO, rows - rb)
        r_i = lax.broadcasted_iota(jnp.int32, (nr, nk), 0)
        c_i = lax.broadcasted_iota(jnp.int32, (nr, nk), 1)
        window = padded[rb:rb + nk]
        for j in range(1, SUBLANES):
            shift = jnp.where(c_i == r_i + j, 1.0, 0.0).astype(BF16)
            rol_ref[j, rb:rb + nr] = jnp.dot(shift, window, preferred_element_type=F32)
    first = HALO - (CONV_WIDTH - 1) // 2

    for r0 in range(0, ts, rs):
        acc = jnp.broadcast_to(cb_ref[...], (rs, CONV_CH))
        for w in range(CONV_WIDTH):
            a, j = divmod(first + w, SUBLANES)
            lo = r0 + a * SUBLANES
            acc = acc + rol_ref[j, lo:lo + rs, :] * jnp.concatenate([cw_ref[w]] * (rs // SUBLANES), axis=0)
        mu = jnp.mean(acc, axis=-1, keepdims=True)
        xc = acc - mu
        var = jnp.mean(xc * xc, axis=-1, keepdims=True)
        y = xc * lax.rsqrt(var + EPS) * lg_ref[...] + lb_ref[...]
        o_ref[0, r0:r0 + rs, :] = (_silu(y) * ga_ref[0, r0:r0 + rs, :].astype(F32)).astype(BF16)


def _halo_specs(ts, seq, width):
    hb = ts // HALO
    nh = seq // HALO
    return [pl.BlockSpec((1, ts, width), lambda b, i: (b, i, 0)),
            pl.BlockSpec((1, HALO, width), lambda b, i: (b, jnp.maximum(i * hb - 1, 0), 0)),
            pl.BlockSpec((1, HALO, width), lambda b, i: (b, jnp.minimum((i + 1) * hb, nh - 1), 0))]


def _convmod(apre, ga, cw, cb, lg, lb, ts=512, rs=32):
    bsz, seq, _ = apre.shape
    vec = pl.BlockSpec((1, CONV_CH), lambda b, i: (0, 0))
    return pl.pallas_call(
        functools.partial(_convmod_kernel, ts=ts, rs=rs),
        grid=(bsz, seq // ts),
        in_specs=_halo_specs(ts, seq, CONV_CH) + [
            pl.BlockSpec((1, ts, CONV_CH), lambda b, i: (b, i, 0)),
            pl.BlockSpec((CONV_WIDTH, SUBLANES, CONV_CH), lambda b, i: (0, 0, 0)), vec, vec, vec],
        out_specs=pl.BlockSpec((1, ts, CONV_CH), lambda b, i: (b, i, 0)),
        out_shape=jax.ShapeDtypeStruct((bsz, seq, CONV_CH), BF16),
        scratch_shapes=[pltpu.VMEM((SUBLANES, ts + 2 * HALO, CONV_CH), F32)],
        compiler_params=_cparams(("parallel", "parallel")),
        name="convmod",
    )(apre, apre, apre, ga, cw, cb, lg, lb)


def _swa_kernel(sink_ref, q_ref, kp_ref, kc_ref, kn_ref, vp_ref, vc_ref, vn_ref, gb_ref, o_ref, *, nblk):
    n = pl.program_id(1)
    last = pl.num_programs(1) - 1
    kcat = jnp.concatenate([kp_ref[0], kc_ref[0], kn_ref[0]], axis=0)
    vcat = jnp.concatenate([vp_ref[0], vc_ref[0], vn_ref[0]], axis=0)
    vaug = jnp.concatenate([vcat, jnp.ones_like(vcat)], axis=1)
    rows = lax.broadcasted_iota(jnp.int32, (2 * BLOCK, 3 * BLOCK), 0)
    qi = jnp.where(rows >= BLOCK, rows - BLOCK, rows)
    ci = lax.broadcasted_iota(jnp.int32, (2 * BLOCK, 3 * BLOCK), 1)
    rel = ci - BLOCK - qi
    band = (rel >= -BLOCK) & (rel <= BLOCK)
    first_key = jnp.where(n > 0, 0, BLOCK)
    end_key = jnp.where(n < last, 3 * BLOCK, 2 * BLOCK)
    lane = lax.broadcasted_iota(jnp.int32, (BLOCK, LANES), 1)
    low = lane < HEAD_DIM
    row1 = lax.broadcasted_iota(jnp.int32, (2 * BLOCK, 1), 0)
    npair = SWA_HEADS // 2
    sinks = [jnp.where(row1 < BLOCK, sink_ref[j], sink_ref[j + npair]) * LOG2E for j in range(npair)]
    masks = []
    for b in range(nblk):
        mask = band
        if b == 0:
            mask = mask & (ci >= first_key)
        if b == nblk - 1:
            mask = mask & (ci < end_key)
        masks.append(mask)
    csl = [slice(j * LANES, (j + 1) * LANES) for j in range(npair)]
    for b in range(nblk):
        rsl = slice(b * BLOCK, (b + 1) * BLOCK)
        kwin = kcat[b * BLOCK:(b + 3) * BLOCK]
        vwin = vaug[b * BLOCK:(b + 3) * BLOCK]
        qqs = []
        for j in range(npair):
            qv = q_ref[0, rsl, csl[j]]
            zero = jnp.zeros_like(qv)
            qqs.append(jnp.concatenate([jnp.where(low, qv, zero), jnp.where(low, zero, qv)], axis=0))
        ss = [jnp.where(masks[b], lax.dot_general(qq, kwin, (((1,), (1,)), ((), ())), preferred_element_type=F32),
                        NEG_BIG) for qq in qqs]
        ms = [jnp.maximum(jnp.max(ss[j], axis=-1, keepdims=True), sinks[j]) for j in range(npair)]
        pvs = [jnp.dot(jnp.exp2(ss[j] - ms[j]).astype(BF16), vwin, preferred_element_type=F32) for j in range(npair)]
        for j in range(npair):
            pv = pvs[j][:, 0:LANES] / (pvs[j][:, LANES:2 * LANES] + jnp.exp2(sinks[j] - ms[j]))
            o = jnp.where(low, pv[:BLOCK], pv[BLOCK:])
            o_ref[0, rsl, csl[j]] = (o * gb_ref[0, rsl, csl[j]].astype(F32)).astype(BF16)


def _swa(sink, q, k, v, gb, nblk=4):
    bsz, seq, _ = q.shape
    rows = nblk * BLOCK
    nb = seq // BLOCK
    cur = lambda b, n: (b, n, 0)
    prv = lambda b, n: (b, jnp.maximum(n * nblk - 1, 0), 0)
    nxt = lambda b, n: (b, jnp.minimum((n + 1) * nblk, nb - 1), 0)
    edge = lambda im: pl.BlockSpec((1, BLOCK, LANES), im)
    mid = pl.BlockSpec((1, rows, LANES), cur)
    wide = pl.BlockSpec((1, rows, HALF), cur)
    return pl.pallas_call(
        functools.partial(_swa_kernel, nblk=nblk),
        grid=(bsz, seq // rows),
        in_specs=[pl.BlockSpec(memory_space=pltpu.SMEM), wide,
                  edge(prv), mid, edge(nxt), edge(prv), mid, edge(nxt), wide],
        out_specs=wide,
        out_shape=jax.ShapeDtypeStruct((bsz, seq, HALF), BF16),
        compiler_params=_cparams(("parallel", "parallel")),
        name="swa",
    )(sink, q, k, k, k, v, v, v, gb)


def _mid_kernel(x_ref, a_ref, o_ref, wo_ref, g_ref, w_ref, cos_ref, sa_ref, sb_ref,
                x1_ref, q_ref, k_ref, v_ref, cg_ref, zg_ref, xbc_ref, dt_ref):
    parts = 2
    rows = x_ref.shape[0] // parts
    groups = [slice(part * rows, (part + 1) * rows) for part in range(parts)]
    hbs = []
    for r in groups:
        x1 = (x_ref[r, :] + jnp.dot(a_ref[r, :], wo_ref[0:HALF, :], preferred_element_type=F32)
              + jnp.dot(o_ref[r, :], wo_ref[HALF:D_MODEL, :], preferred_element_type=F32))
        x1_ref[r, :] = x1
        hbs.append(_rmsnorm(x1, g_ref[...]).astype(BF16))
    ones = jnp.ones((rows, LANES), BF16)
    for r, hb in zip(groups, hbs):
        def mm(a, b):
            return jnp.dot(hb, w_ref[:, a:b], preferred_element_type=F32)

        cos, sa, sb = cos_ref[r, :], sa_ref[r, :], sb_ref[r, :]
        q_ref[r, :] = (_rope_cols(mm(*ODD["q"]), cos, sa, sb) * (HEAD_DIM ** -0.5 * LOG2E)).astype(BF16)
        k_ref[r, :] = _rope_cols(mm(*ODD["k"]), cos, sa, sb).astype(BF16)
        v = mm(*ODD["v"]).astype(BF16)
        for h in range(DIFF_HEADS):
            v_ref[r, 2 * h * LANES:(2 * h + 1) * LANES] = v[:, h * LANES:(h + 1) * LANES]
            v_ref[r, (2 * h + 1) * LANES:(2 * h + 2) * LANES] = ones
        dt_ref[r, :] = mm(*ODD["dt"])
        cg_ref[r, :] = _silu(mm(*ODD["c_gate"])).astype(BF16)
        zg_ref[r, :] = _silu(mm(*ODD["z"])).astype(BF16)
        xbc_ref[r, :] = mm(*ODD["xbc"]).astype(BF16)


def _mid(x2, a2, o2, wo, g, w, cos, sa, sb, seq, tm):
    t = x2.shape[0]
    ns = seq // tm
    row = lambda i: (i, 0)
    pos = lambda i: (i % ns, 0)
    full = lambda i: (0, 0)
    outs = ((D_MODEL, F32), (HALF, BF16), (HALF, BF16), (2 * HALF, BF16), (HALF, BF16), (SSM_INNER, BF16),
            (SSM_CONV_DIM, BF16), (LANES, F32))
    return pl.pallas_call(
        _mid_kernel,
        grid=(t // tm,),
        in_specs=[pl.BlockSpec((tm, D_MODEL), row), pl.BlockSpec((tm, HALF), row), pl.BlockSpec((tm, HALF), row),
                  pl.BlockSpec((D_MODEL, D_MODEL), full), pl.BlockSpec((1, D_MODEL), full),
                  pl.BlockSpec((D_MODEL, IN_ODD_PAD), full),
                  pl.BlockSpec((tm, LANES), pos), pl.BlockSpec((tm, LANES), pos), pl.BlockSpec((tm, LANES), pos)],
        out_specs=[pl.BlockSpec((tm, wd), row) for wd, _ in outs],
        out_shape=[jax.ShapeDtypeStruct((t, wd), dt) for wd, dt in outs],
        compiler_params=_cparams(("parallel",)),
        name="outproj0_inproj1",
    )(x2, a2, o2, wo, g, w, cos, sa, sb)


def _diff_kernel(lq1_ref, lk1_ref, lq2_ref, lk2_ref, g_ref, q_ref, qn_ref, k_ref, v_ref, cg_ref, o_ref,
                 s0_ref, s1_ref, mx0_ref, mx1_ref, m_ref, acc_ref, *, kc, unroll, lam_init):
    tq = q_ref.shape[1]
    nch = k_ref.shape[1] // kc
    lane = lax.broadcasted_iota(jnp.int32, (tq, LANES), 1)

    def stack_maps(q):
        zero = jnp.zeros_like(q)
        return jnp.concatenate([jnp.where(lane < HEAD_DIM, q, zero), jnp.where(lane < HEAD_DIM, zero, q)], axis=0)

    slots = ((s0_ref, mx0_ref), (s1_ref, mx1_ref))

    def scores(qs, c, slot):
        s_ref, mx_ref = slots[slot]
        kch = k_ref[0, pl.ds(pl.multiple_of(c * kc, kc), kc), :]
        s = lax.dot_general(qs, kch, (((1,), (1,)), ((), ())), preferred_element_type=F32)
        s_ref[...] = s
        mx_ref[...] = jnp.broadcast_to(jnp.max(s, axis=-1, keepdims=True), mx_ref.shape)

    def update(c, slot):
        s_ref, mx_ref = slots[slot]
        vch = v_ref[0, pl.ds(pl.multiple_of(c * kc, kc), kc), :]
        m_old = m_ref[...]
        m_new = jnp.maximum(m_old, mx_ref[...])
        alpha = jnp.exp2(m_old - m_new)
        p = jnp.exp2(s_ref[...] - jnp.concatenate([m_new] * (kc // LANES), axis=1)).astype(BF16)
        acc_ref[...] = acc_ref[...] * jnp.concatenate([alpha, alpha], axis=1) + jnp.dot(
            p, vch, preferred_element_type=F32)
        m_ref[...] = m_new

    @pl.when(pl.program_id(2) == 0)
    def _():
        scores(stack_maps(q_ref[0]), 0, 0)

    qq = stack_maps(q_ref[0])
    qq_next = stack_maps(qn_ref[0])
    m_ref[...] = jnp.full(m_ref.shape, NEG_BIG, F32)
    acc_ref[...] = jnp.zeros(acc_ref.shape, F32)
    ntrips = nch // unroll

    def body(t, carry):
        for u in range(unroll):
            c = unroll * t + u
            if u < unroll - 1:
                scores(qq, c + 1, (u + 1) % 2)
            elif ntrips == 1:
                scores(qq_next, 0, 0)
            else:
                wrap = t == ntrips - 1
                scores(jnp.where(wrap, qq_next, qq), jnp.where(wrap, 0, c + 1), 0)
            update(c, u % 2)
        return carry

    if ntrips == 1:
        body(0, 0)
    else:
        lax.fori_loop(0, ntrips, body, 0)

    lam = (jnp.exp(jnp.sum(lq1_ref[...] * lk1_ref[...], axis=-1, keepdims=True))
           - jnp.exp(jnp.sum(lq2_ref[...] * lk2_ref[...], axis=-1, keepdims=True)) + lam_init)
    o0 = acc_ref[0:tq, 0:LANES] / acc_ref[0:tq, LANES:2 * LANES]
    o1 = acc_ref[tq:2 * tq, 0:LANES] / acc_ref[tq:2 * tq, LANES:2 * LANES]
    o = o0 - lam * o1
    o = o * lax.rsqrt(jnp.mean(o * o, axis=-1, keepdims=True) + EPS) * g_ref[...] * (1.0 - lam_init)
    o_ref[0] = (o * cg_ref[0].astype(F32)).astype(BF16)


def _diff_attn(lq1, lk1, lq2, lk2, g, q, k, v, cg, lam_init, tq=512):
    bsz, seq, _ = q.shape
    kc = next(c for c in (1024, 512, 256) if seq % (2 * c) == 0)
    unroll = seq // kc
    nq = seq // tq
    small = pl.BlockSpec((1, HEAD_DIM), lambda b, h, i: (0, 0))
    tile = pl.BlockSpec((1, tq, LANES), lambda b, h, i: (b, i, h))
    next_tile = pl.BlockSpec((1, tq, LANES), lambda b, h, i: (b, jnp.minimum(i + 1, nq - 1), h))
    return pl.pallas_call(
        functools.partial(_diff_kernel, kc=kc, unroll=unroll, lam_init=lam_init),
        grid=(bsz, DIFF_HEADS, nq),
        in_specs=[small, small, small, small, pl.BlockSpec((1, LANES), lambda b, h, i: (0, 0)), tile, next_tile,
                  pl.BlockSpec((1, seq, LANES), lambda b, h, i: (b, 0, h)),
                  pl.BlockSpec((1, seq, 2 * LANES), lambda b, h, i: (b, 0, h)), tile],
        out_specs=tile,
        out_shape=jax.ShapeDtypeStruct((bsz, seq, HALF), BF16),
        scratch_shapes=[pltpu.VMEM((2 * tq, kc), F32), pltpu.VMEM((2 * tq, kc), F32),
                        pltpu.VMEM((2 * tq, LANES), F32), pltpu.VMEM((2 * tq, LANES), F32),
                        pltpu.VMEM((2 * tq, LANES), F32), pltpu.VMEM((2 * tq, 2 * LANES), F32)],
        compiler_params=_cparams(("parallel", "parallel", "arbitrary")),
        name="diff_attn",
    )(lq1, lk1, lq2, lk2, g, q, q, k, v, cg)


def _split3(x):
    hi = x.astype(BF16)
    r = x - hi.astype(F32)
    mid = r.astype(BF16)
    lo = (r - mid.astype(F32)).astype(BF16)
    return hi, mid, lo


def _ssd_fwd_kernel(xc_ref, xp_ref, xn_ref, dt_ref, cw_ref, cb_ref, dtb_ref, a_ref, dsk_ref, y_ref, xconv_ref,
                    carry_ref, xf_ref):
    L = BLOCK
    nchunk = xc_ref.shape[1] // L
    c = pl.program_id(1)
    last = pl.num_programs(1) - 1

    @pl.when(c == 0)
    def _():
        carry_ref[...] = jnp.zeros(carry_ref.shape, F32)

    xp, xn = xp_ref[0], xn_ref[0]
    halo_p = jnp.where(c > 0, xp, jnp.zeros_like(xp))
    halo_n = jnp.where(c < last, xn, jnp.zeros_like(xn))
    r_i = lax.broadcasted_iota(jnp.int32, (L, L + 2 * HALO), 0)
    c_i = lax.broadcasted_iota(jnp.int32, (L, L + 2 * HALO), 1)
    first = HALO - (SSM_CONV - 1) // 2
    shifts = [None if first + w == HALO else jnp.where(c_i == r_i + first + w, 1.0, 0.0).astype(BF16)
              for w in range(SSM_CONV)]
    wd = 2 * LANES
    for nb in range(SSM_CONV_DIM // wd):
        cols = slice(nb * wd, (nb + 1) * wd)
        padded = jnp.concatenate([halo_p[:, cols], xc_ref[0, :, cols], halo_n[:, cols]], axis=0)
        for ci in range(nchunk):
            r0 = ci * L
            window = padded[r0:r0 + L + 2 * HALO]
            acc = jnp.broadcast_to(cb_ref[:, cols], (L, wd))
            for w in range(SSM_CONV):
                tap = (xc_ref[0, r0:r0 + L, cols].astype(F32) if shifts[w] is None
                       else jnp.dot(shifts[w], window, preferred_element_type=F32))
                acc = acc + tap * cw_ref[w:w + 1, cols]
            xbc = _silu(acc)
            xconv_ref[0, r0:r0 + L, cols] = xbc.astype(BF16)
            xf_ref[r0:r0 + L, cols] = xbc

    def xcols(k, a, b, dtype):
        return xf_ref[k * L:(k + 1) * L, a:b].astype(dtype)

    def emit(k, j, y):
        cols = slice(j * LANES, (j + 1) * LANES)
        y_ref[0, k * L:(k + 1) * L, cols] = y + xcols(k, j * LANES, (j + 1) * LANES, F32) * dsk_ref[:, cols]

    _ssd_scan_block(xcols, list(range(nchunk)), dt_ref, dtb_ref, a_ref, carry_ref, emit, reverse=False, col0=0)


def _ssd_bwd_kernel(xconv_ref, dt_ref, dtb_ref, a_ref, yf_ref, zg_ref, sg_ref, x1_ref, c_ref, wo_ref, fg_ref, o_ref,
                    carry_ref, yb_ref):
    L = BLOCK
    nchunk = xconv_ref.shape[1] // L

    @pl.when(pl.program_id(1) == 0)
    def _():
        carry_ref[...] = jnp.zeros(carry_ref.shape, F32)

    x2 = x1_ref[0] + jnp.dot(c_ref[0], wo_ref[0:HALF, :], preferred_element_type=F32)

    def xcols(k, a, b, dtype):
        return xconv_ref[0, k * L:(k + 1) * L, a:b].astype(dtype)

    def emit(k, j, y):
        yb_ref[k * L:(k + 1) * L, j * LANES:(j + 1) * LANES] = y

    _ssd_scan_block(xcols, list(reversed(range(nchunk))), dt_ref, dtb_ref, a_ref, carry_ref, emit,
                    reverse=True, col0=SSM_HEADS)
    y = (yf_ref[0] + yb_ref[...]) * zg_ref[0].astype(F32)
    d = _rmsnorm(y, sg_ref[...]).astype(BF16)
    x2 = x2 + jnp.dot(d, wo_ref[HALF:D_MODEL, :], preferred_element_type=F32)
    o_ref[0] = _rmsnorm(x2, fg_ref[...])


def _ssd_scan_block(xcols, order, dt_ref, dtb_ref, a_ref, carry_ref, emit, *, reverse, col0):
    L = BLOCK
    nchunk = len(order)
    chunk = lambda m, k: m[k * L:(k + 1) * L]

    z = dt_ref[0] + dtb_ref[...]
    dt = jnp.maximum(z, 0.0) + jnp.log(1.0 + jnp.exp(-jnp.abs(z)))
    a = dt * a_ref[...]
    r_i = lax.broadcasted_iota(jnp.int32, (L, L), 0)
    c_i = lax.broadcasted_iota(jnp.int32, (L, L), 1)
    causal = (r_i <= c_i) if reverse else (r_i >= c_i)
    tri = jnp.where(causal, 1.0, 0.0).astype(BF16)
    pieces = _split3(a * LOG2E)
    acum = [sum(jnp.dot(tri, chunk(pc, k), preferred_element_type=F32) for pc in pieces) for k in range(nchunk)]
    acum_t = [m.T for m in acum]
    edge = 0 if reverse else L - 1
    a_end = [m[edge:edge + 1, :] for m in acum]

    lane = lax.broadcasted_iota(jnp.int32, (L, LANES), 1)
    low = lane < HEAD_DIM
    lane1 = lax.broadcasted_iota(jnp.int32, (1, LANES), 1)

    cbs, bts, cms = [], [], []
    for k in range(nchunk):
        bm = [xcols(k, SSM_INNER + g * SSM_STATE, SSM_INNER + (g + 1) * SSM_STATE, BF16) for g in range(2)]
        cm = [xcols(k, SSM_INNER + (2 + g) * SSM_STATE, SSM_INNER + (3 + g) * SSM_STATE, BF16) for g in range(2)]
        cbs.append([lax.dot_general(cm[g], bm[g], (((1,), (1,)), ((), ())), preferred_element_type=F32)
                    for g in range(2)])
        bts.append([bm[g].T for g in range(2)])
        cms.append(cm)
    for j in range(SSM_HEADS // 2):
        g = j // 2
        ca = col0 + 2 * j
        ydiag, states, decay_out, decay_in = [], [], [], []
        for k in range(nchunk):
            acb = [jnp.broadcast_to(acum[k][:, ca + hh:ca + hh + 1], (L, LANES)) for hh in range(2)]
            ac_pair = jnp.where(low, acb[0], acb[1])
            dtk = chunk(dt, k)
            dt_pair = jnp.where(low, dtk[:, ca:ca + 1], dtk[:, ca + 1:ca + 2])
            end_pair = jnp.where(lane1 < HEAD_DIM, a_end[k][:, ca:ca + 1], a_end[k][:, ca + 1:ca + 2])
            xdt = xcols(k, j * LANES, (j + 1) * LANES, F32) * dt_pair
            xdt_b = xdt.astype(BF16)
            zero = jnp.zeros_like(xdt_b)
            ms = []
            for hh in range(2):
                seg = acb[hh] - acum_t[k][ca + hh:ca + hh + 1, :]
                dec = jnp.exp2(jnp.where(causal, seg, NEG_BIG))
                ms.append((cbs[k][g] * dec).astype(BF16))
            lhs = jnp.concatenate(ms, axis=1)
            rhs = jnp.concatenate([jnp.where(low, xdt_b, zero), jnp.where(low, zero, xdt_b)], axis=0)
            ydiag.append(jnp.dot(lhs, rhs, preferred_element_type=F32))
            states.append(jnp.dot(bts[k][g], (xdt * jnp.exp2(end_pair - ac_pair)).astype(BF16),
                                  preferred_element_type=F32))
            decay_out.append(jnp.exp2(end_pair))
            decay_in.append(jnp.exp2(ac_pair))
        carry = carry_ref[j]
        for k in order:
            emit(k, j, ydiag[k] + jnp.dot(cms[k][g], carry.astype(BF16), preferred_element_type=F32) * decay_in[k])
            carry = carry * decay_out[k] + states[k]
        carry_ref[j] = carry


_SSD_CARRY = pltpu.VMEM((SSM_HEADS // 2, SSM_STATE, LANES), F32)
_SSD_ROWS = 4 * BLOCK


def _ssd_fwd(xbc, dt, cw, cb, dtb, a, dsk):
    bsz, seq, _ = xbc.shape
    vec = lambda wd: pl.BlockSpec((1, wd), lambda b, c: (0, 0))
    chunk = lambda wd: pl.BlockSpec((1, _SSD_ROWS, wd), lambda b, c: (b, c, 0))
    return pl.pallas_call(
        _ssd_fwd_kernel,
        grid=(bsz, seq // _SSD_ROWS),
        in_specs=_halo_specs(_SSD_ROWS, seq, SSM_CONV_DIM) + [
            chunk(LANES), pl.BlockSpec((SSM_CONV, SSM_CONV_DIM), lambda b, c: (0, 0)), vec(SSM_CONV_DIM),
            vec(LANES), vec(LANES), vec(SSM_INNER)],
        out_specs=[chunk(SSM_INNER), chunk(SSM_CONV_DIM)],
        out_shape=[jax.ShapeDtypeStruct((bsz, seq, SSM_INNER), F32),
                   jax.ShapeDtypeStruct((bsz, seq, SSM_CONV_DIM), BF16)],
        scratch_shapes=[_SSD_CARRY, pltpu.VMEM((_SSD_ROWS, SSM_CONV_DIM), F32)],
        compiler_params=_cparams(("parallel", "arbitrary")),
        name="ssd_fwd",
    )(xbc, xbc, xbc, dt, cw, cb, dtb, a, dsk)


def _ssd_bwd_final(xconv, dt, dtb, a, yf, zg, sg, x1, c, wo, fg):
    bsz, seq, _ = xconv.shape
    nc = seq // _SSD_ROWS
    vec = lambda wd: pl.BlockSpec((1, wd), lambda b, c: (0, 0))
    chunk = lambda wd: pl.BlockSpec((1, _SSD_ROWS, wd), lambda b, c: (b, nc - 1 - c, 0))
    return pl.pallas_call(
        _ssd_bwd_kernel,
        grid=(bsz, nc),
        in_specs=[chunk(SSM_CONV_DIM), chunk(LANES), vec(LANES), vec(LANES), chunk(SSM_INNER), chunk(SSM_INNER),
                  vec(SSM_INNER), chunk(D_MODEL), chunk(HALF), pl.BlockSpec((D_MODEL, D_MODEL), lambda b, c: (0, 0)),
                  vec(D_MODEL)],
        out_specs=chunk(D_MODEL),
        out_shape=jax.ShapeDtypeStruct((bsz, seq, D_MODEL), F32),
        scratch_shapes=[_SSD_CARRY, pltpu.VMEM((_SSD_ROWS, SSM_INNER), F32)],
        compiler_params=_cparams(("parallel", "arbitrary")),
        name="ssd_bwd_final",
    )(xconv, dt, dtb, a, yf, zg, sg, x1, c, wo, fg)


def _rope_tables(seq):
    inv = 1.0 / (ROPE_THETA ** (jnp.arange(0, HEAD_DIM, 2, dtype=F32) / HEAD_DIM))
    f = jnp.arange(seq, dtype=F32)[:, None] * inv[None, :]
    cos, sin = jnp.cos(f), jnp.sin(f)
    cos = jnp.concatenate([cos] * 4, axis=-1)
    sin = jnp.concatenate([sin] * 4, axis=-1)
    first_half = (jnp.arange(LANES) % HEAD_DIM) < HEAD_DIM // 2
    return cos, jnp.where(first_half, -sin, 0.0), jnp.where(first_half, 0.0, sin)


def _pair_perm():
    idx = []
    for j in range(SWA_HEADS // 2):
        idx += list(range(j * HEAD_DIM, (j + 1) * HEAD_DIM))
        idx += list(range((j + 4) * HEAD_DIM, (j + 5) * HEAD_DIM))
    return jnp.asarray(idx, jnp.int32)


def _pad_lanes(v, offset):
    return jnp.zeros((1, LANES), F32).at[0, offset:offset + v.shape[0]].set(v.astype(F32))


def _prepare(norm_g, w_in0, conv_w, conv_b, conv_ln_g, conv_ln_b, sink, w_out0, w_in1, lambda_q1, lambda_k1,
             lambda_q2, lambda_k2, diff_norm_g, ssm_conv_w, ssm_conv_b, dt_bias_f, dt_bias_b, a_log_f, a_log_b,
             d_skip, ssm_norm_g, w_out1, final_norm_g):
    perm = _pair_perm()
    w0 = w_in0[0].astype(BF16)
    (q0, q1), (g0, g1) = EVEN["q"], EVEN["b_gate"]
    w0 = jnp.concatenate([w0[:, :q0], w0[:, q0:q1][:, perm], w0[:, q1:g0], w0[:, g0:g1][:, perm]], axis=1)
    wo0 = w_out0[0].astype(BF16)
    wo0 = jnp.concatenate([wo0[:HALF], wo0[HALF:][perm]], axis=0)
    w1 = jnp.pad(w_in1[0].astype(BF16), ((0, 0), (0, IN_ODD_PAD - IN_ODD)))
    row = lambda v: v.astype(F32).reshape(1, -1)
    return dict(
        g0=row(norm_g[0]), g1=row(norm_g[1]), w0=w0, wo0=wo0, w1=w1,
        wo1=w_out1[0].astype(BF16), cb=row(conv_b[0]), lg=row(conv_ln_g[0]),
        cw=jnp.broadcast_to(conv_w[0].astype(F32)[:, None, :], (CONV_WIDTH, SUBLANES, CONV_CH)),
        lb=row(conv_ln_b[0]), sink=sink[0].astype(F32),
        lq1=row(lambda_q1[0]), lk1=row(lambda_k1[0]), lq2=row(lambda_q2[0]), lk2=row(lambda_k2[0]),
        dg=row(diff_norm_g[0]), scw=ssm_conv_w[0].astype(F32), scb=row(ssm_conv_b[0]),
        dtb_f=_pad_lanes(dt_bias_f[0], 0), dtb_b=_pad_lanes(dt_bias_b[0], SSM_HEADS),
        a_f=_pad_lanes(-jnp.exp(a_log_f[0].astype(F32)), 0), a_b=_pad_lanes(-jnp.exp(a_log_b[0].astype(F32)), SSM_HEADS),
        dsk=row(jnp.repeat(d_skip[0].astype(F32), HEAD_DIM)), sg=row(ssm_norm_g[0]), fg=row(final_norm_g))


def _trunk(x, p, tm=512):
    bsz, seq, _ = x.shape
    assert seq % tm == 0 and seq % 512 == 0
    t = bsz * seq
    lam_init = 0.8 - 0.6 * math.exp(-0.3 * 1)
    cos, sa, sb = _rope_tables(seq)
    x2 = x.reshape(t, D_MODEL)
    apre, ga, q, k, v, gb = _inproj0(x2, p["g0"], p["w0"], cos, sa, sb, seq, tm)
    r3 = lambda arr: arr.reshape(bsz, seq, arr.shape[-1])
    a = _convmod(r3(apre), r3(ga), p["cw"], p["cb"], p["lg"], p["lb"])
    o = _swa(p["sink"], r3(q), r3(k), r3(v), r3(gb))
    x1, q1, k1, v1, cg, zg, xbc, dt = _mid(x2, a.reshape(t, HALF), o.reshape(t, HALF), p["wo0"], p["g1"], p["w1"],
                                           cos, sa, sb, seq, tm)
    c = _diff_attn(p["lq1"], p["lk1"], p["lq2"], p["lk2"], p["dg"], r3(q1), r3(k1), r3(v1), r3(cg), lam_init)
    yf, xconv = _ssd_fwd(r3(xbc), r3(dt), p["scw"], p["scb"], p["dtb_f"], p["a_f"], p["dsk"])
    return _ssd_bwd_final(xconv, r3(dt), p["dtb_b"], p["a_b"], yf, r3(zg), p["sg"], r3(x1), c, p["wo1"], p["fg"])


def kernel(x_prompt, x_sample, norm_g, w_in0, conv_w, conv_b, conv_ln_g, conv_ln_b, sink, w_out0, w_in1, lambda_q1, lambda_k1, lambda_q2, lambda_k2, diff_norm_g, ssm_conv_w, ssm_conv_b, dt_bias_f, dt_bias_b, a_log_f, a_log_b, d_skip, ssm_norm_g, w_out1, final_norm_g):
    p = _prepare(norm_g, w_in0, conv_w, conv_b, conv_ln_g, conv_ln_b, sink, w_out0, w_in1, lambda_q1, lambda_k1,
                 lambda_q2, lambda_k2, diff_norm_g, ssm_conv_w, ssm_conv_b, dt_bias_f, dt_bias_b, a_log_f, a_log_b,
                 d_skip, ssm_norm_g, w_out1, final_norm_g)
    return (_trunk(x_prompt, p), _trunk(x_sample, p))
```

```python
import functools
import math

import jax
import jax.numpy as jnp
from jax import lax
from jax.experimental import pallas as pl
from jax.experimental.pallas import tpu as pltpu

F32 = jnp.float32
BF16 = jnp.bfloat16

D_MODEL = 1024
EPS = 1e-6
ROPE_THETA = 10000.0
HEAD_DIM = 64
LANES = 128
SUBLANES = 8
HALO = 16
CONV_CH = 512
CONV_WIDTH = 31
SWA_HEADS = 8
BLOCK = 128
DIFF_HEADS = 4
SSM_INNER = 512
SSM_HEADS = 8
SSM_STATE = 128
SSM_CONV = 5
SSM_CONV_DIM = 1024
HALF = D_MODEL // 2


def _col_ranges(**widths):
    out, start = {}, 0
    for name, width in widths.items():
        out[name] = (start, start + width)
        start += width
    return out


EVEN = _col_ranges(a_val=CONV_CH, a_glu=CONV_CH, a_gate=CONV_CH, q=HALF, k=LANES, v=LANES, b_gate=HALF)
ODD = _col_ranges(q=HALF, k=HALF, v=HALF, c_gate=HALF, z=SSM_INNER, xbc=SSM_CONV_DIM, dt=LANES)
IN_EVEN = EVEN["b_gate"][1]
IN_ODD = ODD["dt"][0] + 2 * SSM_HEADS
IN_ODD_PAD = ODD["dt"][1]
NEG_BIG = -1e30
LOG2E = math.log2(math.e)
VMEM_LIMIT = 56 * 1024 * 1024


def _cparams(sem):
    return pltpu.CompilerParams(dimension_semantics=sem, vmem_limit_bytes=VMEM_LIMIT)


def _sigmoid(x):
    return 1.0 / (1.0 + jnp.exp2(x * (-LOG2E)))


def _silu(x):
    return x * _sigmoid(x)


def _rmsnorm(x, g):
    return x * lax.rsqrt(jnp.mean(x * x, axis=-1, keepdims=True) + EPS) * g


def _rope_cols(x, cos, sa, sb):
    outs = []
    for j in range(x.shape[1] // LANES):
        xj = x[:, j * LANES:(j + 1) * LANES]
        outs.append(xj * cos + pltpu.roll(xj, LANES - 32, 1) * sa + pltpu.roll(xj, 32, 1) * sb)
    return outs[0] if len(outs) == 1 else jnp.concatenate(outs, axis=1)


def _inproj0_kernel(x_ref, g_ref, w_ref, cos_ref, sa_ref, sb_ref,
                    apre_ref, ga_ref, q_ref, k_ref, v_ref, gb_ref):
    parts = 2
    rows = x_ref.shape[0] // parts
    groups = [slice(part * rows, (part + 1) * rows) for part in range(parts)]
    hbs = [_rmsnorm(x_ref[r, :], g_ref[...]).astype(BF16) for r in groups]
    for r, hb in zip(groups, hbs):
        def mm(a, b):
            return jnp.dot(hb, w_ref[:, a:b], preferred_element_type=F32)

        cos, sa, sb = cos_ref[r, :], sa_ref[r, :], sb_ref[r, :]
        apre_ref[r, :] = (mm(*EVEN["a_val"]) * _sigmoid(mm(*EVEN["a_glu"]))).astype(BF16)
        ga_ref[r, :] = _silu(mm(*EVEN["a_gate"])).astype(BF16)
        gb_ref[r, :] = _silu(mm(*EVEN["b_gate"])).astype(BF16)
        q_ref[r, :] = (_rope_cols(mm(*EVEN["q"]), cos, sa, sb) * (HEAD_DIM ** -0.5 * LOG2E)).astype(BF16)
        kv = mm(EVEN["k"][0], EVEN["v"][1])
        k_ref[r, :] = _rope_cols(kv[:, 0:LANES], cos, sa, sb).astype(BF16)
        v_ref[r, :] = kv[:, LANES:2 * LANES].astype(BF16)


def _inproj0(x2, g, w, cos, sa, sb, seq, tm):
    t = x2.shape[0]
    ns = seq // tm
    row = lambda i: (i, 0)
    pos = lambda i: (i % ns, 0)
    full = lambda i: (0, 0)
    widths = (CONV_CH, CONV_CH, HALF, LANES, LANES, HALF)
    return pl.pallas_call(
        _inproj0_kernel,
        grid=(t // tm,),
        in_specs=[pl.BlockSpec((tm, D_MODEL), row), pl.BlockSpec((1, D_MODEL), full),
                  pl.BlockSpec((D_MODEL, IN_EVEN), full),
                  pl.BlockSpec((tm, LANES), pos), pl.BlockSpec((tm, LANES), pos), pl.BlockSpec((tm, LANES), pos)],
        out_specs=[pl.BlockSpec((tm, wd), row) for wd in widths],
        out_shape=[jax.ShapeDtypeStruct((t, wd), BF16) for wd in widths],
        compiler_params=_cparams(("parallel",)),
        name="inproj0",
    )(x2, g, w, cos, sa, sb)


def _convmod_kernel(xc_ref, xp_ref, xn_ref, ga_ref, cw_ref, cb_ref, lg_ref, lb_ref, o_ref, rol_ref, *, ts, rs):
    i = pl.program_id(1)
    last = pl.num_programs(1) - 1
    xp, xn = xp_ref[0], xn_ref[0]
    padded = jnp.concatenate([jnp.where(i > 0, xp, jnp.zeros_like(xp)), xc_ref[0],
                              jnp.where(i < last, xn, jnp.zeros_like(xn))], axis=0)
    rows = ts + 2 * HALO
    rol_ref[0] = padded.astype(F32)
    for rb in range(0, rows, BLOCK):
        nr = min(BLOCK, rows - rb)
        nk = min(BLOCK + HALO, rows - rb)
        r_i = lax.broadcasted_iota(jnp.int32, (nr, nk), 0)
        c_i = lax.broadcasted_iota(jnp.int32, (nr, nk), 1)
        window = padded[rb:rb + nk]
        for j in range(1, SUBLANES):
            shift = jnp.where(c_i == r_i + j, 1.0, 0.0).astype(BF16)
            rol_ref[j, rb:rb + nr] = jnp.dot(shift, window, preferred_element_type=F32)
    first = HALO - (CONV_WIDTH - 1) // 2

    for r0 in range(0, ts, rs):
        acc = jnp.broadcast_to(cb_ref[...], (rs, CONV_CH))
        for w in range(CONV_WIDTH):
            a, j = divmod(first + w, SUBLANES)
            lo = r0 + a * SUBLANES
            acc = acc + rol_ref[j, lo:lo + rs, :] * jnp.concatenate([cw_ref[w]] * (rs // SUBLANES), axis=0)
        mu = jnp.mean(acc, axis=-1, keepdims=True)
        xc = acc - mu
        var = jnp.mean(xc * xc, axis=-1, keepdims=True)
        y = xc * lax.rsqrt(var + EPS) * lg_ref[...] + lb_ref[...]
        o_ref[0, r0:r0 + rs, :] = (_silu(y) * ga_ref[0, r0:r0 + rs, :].astype(F32)).astype(BF16)


def _halo_specs(ts, seq, width):
    hb = ts // HALO
    nh = seq // HALO
    return [pl.BlockSpec((1, ts, width), lambda b, i: (b, i, 0)),
            pl.BlockSpec((1, HALO, width), lambda b, i: (b, jnp.maximum(i * hb - 1, 0), 0)),
            pl.BlockSpec((1, HALO, width), lambda b, i: (b, jnp.minimum((i + 1) * hb, nh - 1), 0))]


def _convmod(apre, ga, cw, cb, lg, lb, ts=512, rs=32):
    bsz, seq, _ = apre.shape
    vec = pl.BlockSpec((1, CONV_CH), lambda b, i: (0, 0))
    return pl.pallas_call(
        functools.partial(_convmod_kernel, ts=ts, rs=rs),
        grid=(bsz, seq // ts),
        in_specs=_halo_specs(ts, seq, CONV_CH) + [
            pl.BlockSpec((1, ts, CONV_CH), lambda b, i: (b, i, 0)),
            pl.BlockSpec((CONV_WIDTH, SUBLANES, CONV_CH), lambda b, i: (0, 0, 0)), vec, vec, vec],
        out_specs=pl.BlockSpec((1, ts, CONV_CH), lambda b, i: (b, i, 0)),
        out_shape=jax.ShapeDtypeStruct((bsz, seq, CONV_CH), BF16),
        scratch_shapes=[pltpu.VMEM((SUBLANES, ts + 2 * HALO, CONV_CH), F32)],
        compiler_params=_cparams(("parallel", "parallel")),
        name="convmod",
    )(apre, apre, apre, ga, cw, cb, lg, lb)


def _swa_kernel(sink_ref, q_ref, kp_ref, kc_ref, kn_ref, vp_ref, vc_ref, vn_ref, gb_ref, o_ref, *, nblk):
    n = pl.program_id(1)
    last = pl.num_programs(1) - 1
    kcat = jnp.concatenate([kp_ref[0], kc_ref[0], kn_ref[0]], axis=0)
    vcat = jnp.concatenate([vp_ref[0], vc_ref[0], vn_ref[0]], axis=0)
    vaug = jnp.concatenate([vcat, jnp.ones_like(vcat)], axis=1)
    rows = lax.broadcasted_iota(jnp.int32, (2 * BLOCK, 3 * BLOCK), 0)
    qi = jnp.where(rows >= BLOCK, rows - BLOCK, rows)
    ci = lax.broadcasted_iota(jnp.int32, (2 * BLOCK, 3 * BLOCK), 1)
    rel = ci - BLOCK - qi
    band = (rel >= -BLOCK) & (rel <= BLOCK)
    first_key = jnp.where(n > 0, 0, BLOCK)
    end_key = jnp.where(n < last, 3 * BLOCK, 2 * BLOCK)
    lane = lax.broadcasted_iota(jnp.int32, (BLOCK, LANES), 1)
    low = lane < HEAD_DIM
    row1 = lax.broadcasted_iota(jnp.int32, (2 * BLOCK, 1), 0)
    npair = SWA_HEADS // 2
    sinks = [jnp.where(row1 < BLOCK, sink_ref[j], sink_ref[j + npair]) * LOG2E for j in range(npair)]
    masks = []
    for b in range(nblk):
        mask = band
        if b == 0:
            mask = mask & (ci >= first_key)
        if b == nblk - 1:
            mask = mask & (ci < end_key)
        masks.append(mask)
    csl = [slice(j * LANES, (j + 1) * LANES) for j in range(npair)]
    for b in range(nblk):
        rsl = slice(b * BLOCK, (b + 1) * BLOCK)
        kwin = kcat[b * BLOCK:(b + 3) * BLOCK]
        vwin = vaug[b * BLOCK:(b + 3) * BLOCK]
        qqs = []
        for j in range(npair):
            qv = q_ref[0, rsl, csl[j]]
            zero = jnp.zeros_like(qv)
            qqs.append(jnp.concatenate([jnp.where(low, qv, zero), jnp.where(low, zero, qv)], axis=0))
        ss = [jnp.where(masks[b], lax.dot_general(qq, kwin, (((1,), (1,)), ((), ())), preferred_element_type=F32),
                        NEG_BIG) for qq in qqs]
        ms = [jnp.maximum(jnp.max(ss[j], axis=-1, keepdims=True), sinks[j]) for j in range(npair)]
        pvs = [jnp.dot(jnp.exp2(ss[j] - ms[j]).astype(BF16), vwin, preferred_element_type=F32) for j in range(npair)]
        for j in range(npair):
            pv = pvs[j][:, 0:LANES] / (pvs[j][:, LANES:2 * LANES] + jnp.exp2(sinks[j] - ms[j]))
            o = jnp.where(low, pv[:BLOCK], pv[BLOCK:])
            o_ref[0, rsl, csl[j]] = (o * gb_ref[0, rsl, csl[j]].astype(F32)).astype(BF16)


def _swa(sink, q, k, v, gb, nblk=4):
    bsz, seq, _ = q.shape
    rows = nblk * BLOCK
    nb = seq // BLOCK
    cur = lambda b, n: (b, n, 0)
    prv = lambda b, n: (b, jnp.maximum(n * nblk - 1, 0), 0)
    nxt = lambda b, n: (b, jnp.minimum((n + 1) * nblk, nb - 1), 0)
    edge = lambda im: pl.BlockSpec((1, BLOCK, LANES), im)
    mid = pl.BlockSpec((1, rows, LANES), cur)
    wide = pl.BlockSpec((1, rows, HALF), cur)
    return pl.pallas_call(
        functools.partial(_swa_kernel, nblk=nblk),
        grid=(bsz, seq // rows),
        in_specs=[pl.BlockSpec(memory_space=pltpu.SMEM), wide,
                  edge(prv), mid, edge(nxt), edge(prv), mid, edge(nxt), wide],
        out_specs=wide,
        out_shape=jax.ShapeDtypeStruct((bsz, seq, HALF), BF16),
        compiler_params=_cparams(("parallel", "parallel")),
        name="swa",
    )(sink, q, k, k, k, v, v, v, gb)


def _mid_kernel(x_ref, a_ref, o_ref, wo_ref, g_ref, w_ref, cos_ref, sa_ref, sb_ref,
                x1_ref, q_ref, k_ref, v_ref, cg_ref, zg_ref, xbc_ref, dt_ref):
    parts = 2
    rows = x_ref.shape[0] // parts
    groups = [slice(part * rows, (part + 1) * rows) for part in range(parts)]
    hbs = []
    for r in groups:
        x1 = (x_ref[r, :] + jnp.dot(a_ref[r, :], wo_ref[0:HALF, :], preferred_element_type=F32)
              + jnp.dot(o_ref[r, :], wo_ref[HALF:D_MODEL, :], preferred_element_type=F32))
        x1_ref[r, :] = x1
        hbs.append(_rmsnorm(x1, g_ref[...]).astype(BF16))
    ones = jnp.ones((rows, LANES), BF16)
    for r, hb in zip(groups, hbs):
        def mm(a, b):
            return jnp.dot(hb, w_ref[:, a:b], preferred_element_type=F32)

        cos, sa, sb = cos_ref[r, :], sa_ref[r, :], sb_ref[r, :]
        q_ref[r, :] = (_rope_cols(mm(*ODD["q"]), cos, sa, sb) * (HEAD_DIM ** -0.5 * LOG2E)).astype(BF16)
        k_ref[r, :] = _rope_cols(mm(*ODD["k"]), cos, sa, sb).astype(BF16)
        v = mm(*ODD["v"]).astype(BF16)
        for h in range(DIFF_HEADS):
            v_ref[r, 2 * h * LANES:(2 * h + 1) * LANES] = v[:, h * LANES:(h + 1) * LANES]
            v_ref[r, (2 * h + 1) * LANES:(2 * h + 2) * LANES] = ones
        dt_ref[r, :] = mm(*ODD["dt"])
        cg_ref[r, :] = _silu(mm(*ODD["c_gate"])).astype(BF16)
        zg_ref[r, :] = _silu(mm(*ODD["z"])).astype(BF16)
        xbc_ref[r, :] = mm(*ODD["xbc"]).astype(BF16)


def _mid(x2, a2, o2, wo, g, w, cos, sa, sb, seq, tm):
    t = x2.shape[0]
    ns = seq // tm
    row = lambda i: (i, 0)
    pos = lambda i: (i % ns, 0)
    full = lambda i: (0, 0)
    outs = ((D_MODEL, F32), (HALF, BF16), (HALF, BF16), (2 * HALF, BF16), (HALF, BF16), (SSM_INNER, BF16),
            (SSM_CONV_DIM, BF16), (LANES, F32))
    return pl.pallas_call(
        _mid_kernel,
        grid=(t // tm,),
        in_specs=[pl.BlockSpec((tm, D_MODEL), row), pl.BlockSpec((tm, HALF), row), pl.BlockSpec((tm, HALF), row),
                  pl.BlockSpec((D_MODEL, D_MODEL), full), pl.BlockSpec((1, D_MODEL), full),
                  pl.BlockSpec((D_MODEL, IN_ODD_PAD), full),
                  pl.BlockSpec((tm, LANES), pos), pl.BlockSpec((tm, LANES), pos), pl.BlockSpec((tm, LANES), pos)],
        out_specs=[pl.BlockSpec((tm, wd), row) for wd, _ in outs],
        out_shape=[jax.ShapeDtypeStruct((t, wd), dt) for wd, dt in outs],
        compiler_params=_cparams(("parallel",)),
        name="outproj0_inproj1",
    )(x2, a2, o2, wo, g, w, cos, sa, sb)


def _diff_kernel(lq1_ref, lk1_ref, lq2_ref, lk2_ref, g_ref, q_ref, qn_ref, k_ref, v_ref, cg_ref, o_ref,
                 s0_ref, s1_ref, mx0_ref, mx1_ref, m_ref, acc_ref, *, tq, kc, lam_init):
    nsub = q_ref.shape[1] // tq
    nch = k_ref.shape[1] // kc
    lane = lax.broadcasted_iota(jnp.int32, (tq, LANES), 1)

    def stack_maps(q):
        zero = jnp.zeros_like(q)
        return jnp.concatenate([jnp.where(lane < HEAD_DIM, q, zero), jnp.where(lane < HEAD_DIM, zero, q)], axis=0)

    slots = ((s0_ref, mx0_ref), (s1_ref, mx1_ref))

    def scores(qs, c, slot):
        s_ref, mx_ref = slots[slot]
        kch = k_ref[0, pl.ds(pl.multiple_of(c * kc, kc), kc), :]
        s = lax.dot_general(qs, kch, (((1,), (1,)), ((), ())), preferred_element_type=F32)
        s_ref[...] = s
        mx_ref[...] = jnp.broadcast_to(jnp.max(s, axis=-1, keepdims=True), mx_ref.shape)

    def update(c, slot, sub):
        s_ref, mx_ref = slots[slot]
        vch = v_ref[0, pl.ds(pl.multiple_of(c * kc, kc), kc), :]
        m_old = m_ref[sub]
        m_new = jnp.maximum(m_old, mx_ref[...])
        alpha = jnp.exp2(m_old - m_new)
        p = jnp.exp2(s_ref[...] - jnp.concatenate([m_new] * (kc // LANES), axis=1)).astype(BF16)
        acc_ref[sub] = acc_ref[sub] * jnp.concatenate([alpha, alpha], axis=1) + jnp.dot(
            p, vch, preferred_element_type=F32)
        m_ref[sub] = m_new

    @pl.when(pl.program_id(2) == 0)
    def _():
        scores(stack_maps(q_ref[0, 0:tq]), 0, 0)

    qqs = [stack_maps(q_ref[0, sub * tq:(sub + 1) * tq]) for sub in range(nsub)] + [stack_maps(qn_ref[0])]
    lam = (jnp.exp(jnp.sum(lq1_ref[...] * lk1_ref[...], axis=-1, keepdims=True))
           - jnp.exp(jnp.sum(lq2_ref[...] * lk2_ref[...], axis=-1, keepdims=True)) + lam_init)
    for sub in range(nsub):
        rows = slice(sub * tq, (sub + 1) * tq)
        m_ref[sub] = jnp.full(m_ref.shape[1:], NEG_BIG, F32)
        acc_ref[sub] = jnp.zeros(acc_ref.shape[1:], F32)
        for c in range(nch):
            if c + 1 < nch:
                scores(qqs[sub], c + 1, (c + 1) % 2)
            else:
                scores(qqs[sub + 1], 0, 0)
            update(c, c % 2, sub)
        o0 = acc_ref[sub, 0:tq, 0:LANES] / acc_ref[sub, 0:tq, LANES:2 * LANES]
        o1 = acc_ref[sub, tq:2 * tq, 0:LANES] / acc_ref[sub, tq:2 * tq, LANES:2 * LANES]
        o = o0 - lam * o1
        o = o * lax.rsqrt(jnp.mean(o * o, axis=-1, keepdims=True) + EPS) * g_ref[...] * (1.0 - lam_init)
        o_ref[0, rows, :] = (o * cg_ref[0, rows, :].astype(F32)).astype(BF16)


def _diff_attn(lq1, lk1, lq2, lk2, g, q, k, v, cg, lam_init, tq=512):
    bsz, seq, _ = q.shape
    kc = next(c for c in (1024, 512, 256) if seq % (2 * c) == 0)
    nsub = 2 if seq <= 4096 and seq % (2 * tq) == 0 else 1
    nsteps = seq // (nsub * tq)
    small = pl.BlockSpec((1, HEAD_DIM), lambda b, h, i: (0, 0))
    tile = pl.BlockSpec((1, nsub * tq, LANES), lambda b, h, i: (b, i, h))
    next_tile = pl.BlockSpec((1, tq, LANES), lambda b, h, i: (b, jnp.minimum(i + 1, nsteps - 1) * nsub, h))
    return pl.pallas_call(
        functools.partial(_diff_kernel, tq=tq, kc=kc, lam_init=lam_init),
        grid=(bsz, DIFF_HEADS, nsteps),
        in_specs=[small, small, small, small, pl.BlockSpec((1, LANES), lambda b, h, i: (0, 0)), tile, next_tile,
                  pl.BlockSpec((1, seq, LANES), lambda b, h, i: (b, 0, h)),
                  pl.BlockSpec((1, seq, 2 * LANES), lambda b, h, i: (b, 0, h)), tile],
        out_specs=tile,
        out_shape=jax.ShapeDtypeStruct((bsz, seq, HALF), BF16),
        scratch_shapes=[pltpu.VMEM((2 * tq, kc), F32), pltpu.VMEM((2 * tq, kc), F32),
                        pltpu.VMEM((2 * tq, LANES), F32), pltpu.VMEM((2 * tq, LANES), F32),
                        pltpu.VMEM((nsub, 2 * tq, LANES), F32), pltpu.VMEM((nsub, 2 * tq, 2 * LANES), F32)],
        compiler_params=_cparams(("parallel", "parallel", "arbitrary")),
        name="diff_attn",
    )(lq1, lk1, lq2, lk2, g, q, q, k, v, cg)


def _split3(x):
    hi = x.astype(BF16)
    r = x - hi.astype(F32)
    mid = r.astype(BF16)
    lo = (r - mid.astype(F32)).astype(BF16)
    return hi, mid, lo


def _ssd_fwd_kernel(xc_ref, xp_ref, xn_ref, dt_ref, cw_ref, cb_ref, dtb_ref, a_ref, dsk_ref, y_ref, xconv_ref,
                    carry_ref, xf_ref):
    L = BLOCK
    nchunk = xc_ref.shape[1] // L
    c = pl.program_id(1)
    last = pl.num_programs(1) - 1

    @pl.when(c == 0)
    def _():
        carry_ref[...] = jnp.zeros(carry_ref.shape, F32)

    xp, xn = xp_ref[0], xn_ref[0]
    halo_p = jnp.where(c > 0, xp, jnp.zeros_like(xp))
    halo_n = jnp.where(c < last, xn, jnp.zeros_like(xn))
    r_i = lax.broadcasted_iota(jnp.int32, (L, L + 2 * HALO), 0)
    c_i = lax.broadcasted_iota(jnp.int32, (L, L + 2 * HALO), 1)
    first = HALO - (SSM_CONV - 1) // 2
    shifts = [None if first + w == HALO else jnp.where(c_i == r_i + first + w, 1.0, 0.0).astype(BF16)
              for w in range(SSM_CONV)]
    wd = 2 * LANES
    for nb in range(SSM_CONV_DIM // wd):
        cols = slice(nb * wd, (nb + 1) * wd)
        padded = jnp.concatenate([halo_p[:, cols], xc_ref[0, :, cols], halo_n[:, cols]], axis=0)
        for ci in range(nchunk):
            r0 = ci * L
            window = padded[r0:r0 + L + 2 * HALO]
            acc = jnp.broadcast_to(cb_ref[:, cols], (L, wd))
            for w in range(SSM_CONV):
                tap = (xc_ref[0, r0:r0 + L, cols].astype(F32) if shifts[w] is None
                       else jnp.dot(shifts[w], window, preferred_element_type=F32))
                acc = acc + tap * cw_ref[w:w + 1, cols]
            xbc = _silu(acc)
            xconv_ref[0, r0:r0 + L, cols] = xbc.astype(BF16)
            xf_ref[r0:r0 + L, cols] = xbc

    def xcols(k, a, b, dtype):
        return xf_ref[k * L:(k + 1) * L, a:b].astype(dtype)

    def emit(k, j, y):
        cols = slice(j * LANES, (j + 1) * LANES)
        y_ref[0, k * L:(k + 1) * L, cols] = y + xcols(k, j * LANES, (j + 1) * LANES, F32) * dsk_ref[:, cols]

    _ssd_scan_block(xcols, list(range(nchunk)), dt_ref, dtb_ref, a_ref, carry_ref, emit, reverse=False, col0=0)


def _ssd_bwd_kernel(xconv_ref, dt_ref, dtb_ref, a_ref, yf_ref, zg_ref, sg_ref, x1_ref, c_ref, wo_ref, fg_ref, o_ref,
                    carry_ref, yb_ref):
    L = BLOCK
    nchunk = xconv_ref.shape[1] // L

    @pl.when(pl.program_id(1) == 0)
    def _():
        carry_ref[...] = jnp.zeros(carry_ref.shape, F32)

    x2 = x1_ref[0] + jnp.dot(c_ref[0], wo_ref[0:HALF, :], preferred_element_type=F32)

    def xcols(k, a, b, dtype):
        return xconv_ref[0, k * L:(k + 1) * L, a:b].astype(dtype)

    def emit(k, j, y):
        yb_ref[k * L:(k + 1) * L, j * LANES:(j + 1) * LANES] = y

    _ssd_scan_block(xcols, list(reversed(range(nchunk))), dt_ref, dtb_ref, a_ref, carry_ref, emit,
                    reverse=True, col0=SSM_HEADS)
    y = (yf_ref[0] + yb_ref[...]) * zg_ref[0].astype(F32)
    d = _rmsnorm(y, sg_ref[...]).astype(BF16)
    x2 = x2 + jnp.dot(d, wo_ref[HALF:D_MODEL, :], preferred_element_type=F32)
    o_ref[0] = _rmsnorm(x2, fg_ref[...])


def _ssd_scan_block(xcols, order, dt_ref, dtb_ref, a_ref, carry_ref, emit, *, reverse, col0):
    L = BLOCK
    nchunk = len(order)
    chunk = lambda m, k: m[k * L:(k + 1) * L]

    z = dt_ref[0] + dtb_ref[...]
    dt = jnp.maximum(z, 0.0) + jnp.log(1.0 + jnp.exp(-jnp.abs(z)))
    a = dt * a_ref[...]
    r_i = lax.broadcasted_iota(jnp.int32, (L, L), 0)
    c_i = lax.broadcasted_iota(jnp.int32, (L, L), 1)
    causal = (r_i <= c_i) if reverse else (r_i >= c_i)
    tri = jnp.where(causal, 1.0, 0.0).astype(BF16)
    pieces = _split3(a * LOG2E)
    acum = [sum(jnp.dot(tri, chunk(pc, k), preferred_element_type=F32) for pc in pieces) for k in range(nchunk)]
    acum_t = [m.T for m in acum]
    edge = 0 if reverse else L - 1
    a_end = [m[edge:edge + 1, :] for m in acum]

    lane = lax.broadcasted_iota(jnp.int32, (L, LANES), 1)
    low = lane < HEAD_DIM
    lane1 = lax.broadcasted_iota(jnp.int32, (1, LANES), 1)

    cbs, bts, cms = [], [], []
    for k in range(nchunk):
        bm = [xcols(k, SSM_INNER + g * SSM_STATE, SSM_INNER + (g + 1) * SSM_STATE, BF16) for g in range(2)]
        cm = [xcols(k, SSM_INNER + (2 + g) * SSM_STATE, SSM_INNER + (3 + g) * SSM_STATE, BF16) for g in range(2)]
        cbs.append([lax.dot_general(cm[g], bm[g], (((1,), (1,)), ((), ())), preferred_element_type=F32)
                    for g in range(2)])
        bts.append([bm[g].T for g in range(2)])
        cms.append(cm)
    for j in range(SSM_HEADS // 2):
        g = j // 2
        ca = col0 + 2 * j
        ydiag, states, decay_out, decay_in = [], [], [], []
        for k in range(nchunk):
            acb = [jnp.broadcast_to(acum[k][:, ca + hh:ca + hh + 1], (L, LANES)) for hh in range(2)]
            ac_pair = jnp.where(low, acb[0], acb[1])
            dtk = chunk(dt, k)
            dt_pair = jnp.where(low, dtk[:, ca:ca + 1], dtk[:, ca + 1:ca + 2])
            end_pair = jnp.where(lane1 < HEAD_DIM, a_end[k][:, ca:ca + 1], a_end[k][:, ca + 1:ca + 2])
            xdt = xcols(k, j * LANES, (j + 1) * LANES, F32) * dt_pair
            xdt_b = xdt.astype(BF16)
            zero = jnp.zeros_like(xdt_b)
            ms = []
            for hh in range(2):
                seg = acb[hh] - acum_t[k][ca + hh:ca + hh + 1, :]
                dec = jnp.exp2(jnp.where(causal, seg, NEG_BIG))
                ms.append((cbs[k][g] * dec).astype(BF16))
            lhs = jnp.concatenate(ms, axis=1)
            rhs = jnp.concatenate([jnp.where(low, xdt_b, zero), jnp.where(low, zero, xdt_b)], axis=0)
            ydiag.append(jnp.dot(lhs, rhs, preferred_element_type=F32))
            states.append(jnp.dot(bts[k][g], (xdt * jnp.exp2(end_pair - ac_pair)).astype(BF16),
                                  preferred_element_type=F32))
            decay_out.append(jnp.exp2(end_pair))
            decay_in.append(jnp.exp2(ac_pair))
        carry = carry_ref[j]
        for k in order:
            emit(k, j, ydiag[k] + jnp.dot(cms[k][g], carry.astype(BF16), preferred_element_type=F32) * decay_in[k])
            carry = carry * decay_out[k] + states[k]
        carry_ref[j] = carry


_SSD_CARRY = pltpu.VMEM((SSM_HEADS // 2, SSM_STATE, LANES), F32)
_SSD_ROWS = 4 * BLOCK


def _ssd_fwd(xbc, dt, cw, cb, dtb, a, dsk):
    bsz, seq, _ = xbc.shape
    vec = lambda wd: pl.BlockSpec((1, wd), lambda b, c: (0, 0))
    chunk = lambda wd: pl.BlockSpec((1, _SSD_ROWS, wd), lambda b, c: (b, c, 0))
    return pl.pallas_call(
        _ssd_fwd_kernel,
        grid=(bsz, seq // _SSD_ROWS),
        in_specs=_halo_specs(_SSD_ROWS, seq, SSM_CONV_DIM) + [
            chunk(LANES), pl.BlockSpec((SSM_CONV, SSM_CONV_DIM), lambda b, c: (0, 0)), vec(SSM_CONV_DIM),
            vec(LANES), vec(LANES), vec(SSM_INNER)],
        out_specs=[chunk(SSM_INNER), chunk(SSM_CONV_DIM)],
        out_shape=[jax.ShapeDtypeStruct((bsz, seq, SSM_INNER), F32),
                   jax.ShapeDtypeStruct((bsz, seq, SSM_CONV_DIM), BF16)],
        scratch_shapes=[_SSD_CARRY, pltpu.VMEM((_SSD_ROWS, SSM_CONV_DIM), F32)],
        compiler_params=_cparams(("parallel", "arbitrary")),
        name="ssd_fwd",
    )(xbc, xbc, xbc, dt, cw, cb, dtb, a, dsk)


def _ssd_bwd_final(xconv, dt, dtb, a, yf, zg, sg, x1, c, wo, fg):
    bsz, seq, _ = xconv.shape
    nc = seq // _SSD_ROWS
    vec = lambda wd: pl.BlockSpec((1, wd), lambda b, c: (0, 0))
    chunk = lambda wd: pl.BlockSpec((1, _SSD_ROWS, wd), lambda b, c: (b, nc - 1 - c, 0))
    return pl.pallas_call(
        _ssd_bwd_kernel,
        grid=(bsz, nc),
        in_specs=[chunk(SSM_CONV_DIM), chunk(LANES), vec(LANES), vec(LANES), chunk(SSM_INNER), chunk(SSM_INNER),
                  vec(SSM_INNER), chunk(D_MODEL), chunk(HALF), pl.BlockSpec((D_MODEL, D_MODEL), lambda b, c: (0, 0)),
                  vec(D_MODEL)],
        out_specs=chunk(D_MODEL),
        out_shape=jax.ShapeDtypeStruct((bsz, seq, D_MODEL), F32),
        scratch_shapes=[_SSD_CARRY, pltpu.VMEM((_SSD_ROWS, SSM_INNER), F32)],
        compiler_params=_cparams(("parallel", "arbitrary")),
        name="ssd_bwd_final",
    )(xconv, dt, dtb, a, yf, zg, sg, x1, c, wo, fg)


def _rope_tables(seq):
    inv = 1.0 / (ROPE_THETA ** (jnp.arange(0, HEAD_DIM, 2, dtype=F32) / HEAD_DIM))
    f = jnp.arange(seq, dtype=F32)[:, None] * inv[None, :]
    cos, sin = jnp.cos(f), jnp.sin(f)
    cos = jnp.concatenate([cos] * 4, axis=-1)
    sin = jnp.concatenate([sin] * 4, axis=-1)
    first_half = (jnp.arange(LANES) % HEAD_DIM) < HEAD_DIM // 2
    return cos, jnp.where(first_half, -sin, 0.0), jnp.where(first_half, 0.0, sin)


def _pair_perm():
    idx = []
    for j in range(SWA_HEADS // 2):
        idx += list(range(j * HEAD_DIM, (j + 1) * HEAD_DIM))
        idx += list(range((j + 4) * HEAD_DIM, (j + 5) * HEAD_DIM))
    return jnp.asarray(idx, jnp.int32)


def _pad_lanes(v, offset):
    return jnp.zeros((1, LANES), F32).at[0, offset:offset + v.shape[0]].set(v.astype(F32))


def _prepare(norm_g, w_in0, conv_w, conv_b, conv_ln_g, conv_ln_b, sink, w_out0, w_in1, lambda_q1, lambda_k1,
             lambda_q2, lambda_k2, diff_norm_g, ssm_conv_w, ssm_conv_b, dt_bias_f, dt_bias_b, a_log_f, a_log_b,
             d_skip, ssm_norm_g, w_out1, final_norm_g):
    perm = _pair_perm()
    w0 = w_in0[0].astype(BF16)
    (q0, q1), (g0, g1) = EVEN["q"], EVEN["b_gate"]
    w0 = jnp.concatenate([w0[:, :q0], w0[:, q0:q1][:, perm], w0[:, q1:g0], w0[:, g0:g1][:, perm]], axis=1)
    wo0 = w_out0[0].astype(BF16)
    wo0 = jnp.concatenate([wo0[:HALF], wo0[HALF:][perm]], axis=0)
    w1 = jnp.pad(w_in1[0].astype(BF16), ((0, 0), (0, IN_ODD_PAD - IN_ODD)))
    row = lambda v: v.astype(F32).reshape(1, -1)
    return dict(
        g0=row(norm_g[0]), g1=row(norm_g[1]), w0=w0, wo0=wo0, w1=w1,
        wo1=w_out1[0].astype(BF16), cb=row(conv_b[0]), lg=row(conv_ln_g[0]),
        cw=jnp.broadcast_to(conv_w[0].astype(F32)[:, None, :], (CONV_WIDTH, SUBLANES, CONV_CH)),
        lb=row(conv_ln_b[0]), sink=sink[0].astype(F32),
        lq1=row(lambda_q1[0]), lk1=row(lambda_k1[0]), lq2=row(lambda_q2[0]), lk2=row(lambda_k2[0]),
        dg=row(diff_norm_g[0]), scw=ssm_conv_w[0].astype(F32), scb=row(ssm_conv_b[0]),
        dtb_f=_pad_lanes(dt_bias_f[0], 0), dtb_b=_pad_lanes(dt_bias_b[0], SSM_HEADS),
        a_f=_pad_lanes(-jnp.exp(a_log_f[0].astype(F32)), 0), a_b=_pad_lanes(-jnp.exp(a_log_b[0].astype(F32)), SSM_HEADS),
        dsk=row(jnp.repeat(d_skip[0].astype(F32), HEAD_DIM)), sg=row(ssm_norm_g[0]), fg=row(final_norm_g))


def _trunk(x, p, tm=512):
    bsz, seq, _ = x.shape
    assert seq % tm == 0 and seq % 512 == 0
    t = bsz * seq
    lam_init = 0.8 - 0.6 * math.exp(-0.3 * 1)
    cos, sa, sb = _rope_tables(seq)
    x2 = x.reshape(t, D_MODEL)
    apre, ga, q, k, v, gb = _inproj0(x2, p["g0"], p["w0"], cos, sa, sb, seq, tm)
    r3 = lambda arr: arr.reshape(bsz, seq, arr.shape[-1])
    a = _convmod(r3(apre), r3(ga), p["cw"], p["cb"], p["lg"], p["lb"])
    o = _swa(p["sink"], r3(q), r3(k), r3(v), r3(gb))
    x1, q1, k1, v1, cg, zg, xbc, dt = _mid(x2, a.reshape(t, HALF), o.reshape(t, HALF), p["wo0"], p["g1"], p["w1"],
                                           cos, sa, sb, seq, tm)
    c = _diff_attn(p["lq1"], p["lk1"], p["lq2"], p["lk2"], p["dg"], r3(q1), r3(k1), r3(v1), r3(cg), lam_init)
    yf, xconv = _ssd_fwd(r3(xbc), r3(dt), p["scw"], p["scb"], p["dtb_f"], p["a_f"], p["dsk"])
    return _ssd_bwd_final(xconv, r3(dt), p["dtb_b"], p["a_b"], yf, r3(zg), p["sg"], r3(x1), c, p["wo1"], p["fg"])


def kernel(x_prompt, x_sample, norm_g, w_in0, conv_w, conv_b, conv_ln_g, conv_ln_b, sink, w_out0, w_in1, lambda_q1, lambda_k1, lambda_q2, lambda_k2, diff_norm_g, ssm_conv_w, ssm_conv_b, dt_bias_f, dt_bias_b, a_log_f, a_log_b, d_skip, ssm_norm_g, w_out1, final_norm_g):
    p = _prepare(norm_g, w_in0, conv_w, conv_b, conv_ln_g, conv_ln_b, sink, w_out0, w_in1, lambda_q1, lambda_k1,
                 lambda_q2, lambda_k2, diff_norm_g, ssm_conv_w, ssm_conv_b, dt_bias_f, dt_bias_b, a_log_f, a_log_b,
                 d_skip, ssm_norm_g, w_out1, final_norm_g)
    return (_trunk(x_prompt, p), _trunk(x_sample, p))
```

```python
import functools
import math

import jax
import jax.numpy as jnp
from jax import lax
from jax.experimental import pallas as pl
from jax.experimental.pallas import tpu as pltpu

F32 = jnp.float32
BF16 = jnp.bfloat16

D_MODEL = 1024
EPS = 1e-6
ROPE_THETA = 10000.0
HEAD_DIM = 64
LANES = 128
SUBLANES = 8
HALO = 16
CONV_CH = 512
CONV_WIDTH = 31
SWA_HEADS = 8
BLOCK = 128
DIFF_HEADS = 4
SSM_INNER = 512
SSM_HEADS = 8
SSM_STATE = 128
SSM_CONV = 5
SSM_CONV_DIM = 1024
HALF = D_MODEL // 2


def _col_ranges(**widths):
    out, start = {}, 0
    for name, width in widths.items():
        out[name] = (start, start + width)
        start += width
    return out


EVEN = _col_ranges(a_val=CONV_CH, a_glu=CONV_CH, a_gate=CONV_CH, q=HALF, k=LANES, v=LANES, b_gate=HALF)
ODD = _col_ranges(q=HALF, k=HALF, v=HALF, c_gate=HALF, z=SSM_INNER, xbc=SSM_CONV_DIM, dt=LANES)
IN_EVEN = EVEN["b_gate"][1]
IN_ODD = ODD["dt"][0] + 2 * SSM_HEADS
IN_ODD_PAD = ODD["dt"][1]
NEG_BIG = -1e30
LOG2E = math.log2(math.e)
VMEM_LIMIT = 56 * 1024 * 1024


def _cparams(sem):
    return pltpu.CompilerParams(dimension_semantics=sem, vmem_limit_bytes=VMEM_LIMIT)


def _sigmoid(x):
    return 1.0 / (1.0 + jnp.exp2(x * (-LOG2E)))


def _silu(x):
    return x * _sigmoid(x)


def _rmsnorm(x, g):
    return x * lax.rsqrt(jnp.mean(x * x, axis=-1, keepdims=True) + EPS) * g


def _rope_cols(x, cos, sa, sb):
    outs = []
    for j in range(x.shape[1] // LANES):
        xj = x[:, j * LANES:(j + 1) * LANES]
        outs.append(xj * cos + pltpu.roll(xj, LANES - 32, 1) * sa + pltpu.roll(xj, 32, 1) * sb)
    return outs[0] if len(outs) == 1 else jnp.concatenate(outs, axis=1)


def _inproj0_kernel(x_ref, g_ref, w_ref, cos_ref, sa_ref, sb_ref,
                    apre_ref, ga_ref, q_ref, k_ref, v_ref, gb_ref):
    parts = 2
    rows = x_ref.shape[0] // parts
    groups = [slice(part * rows, (part + 1) * rows) for part in range(parts)]
    hbs = [_rmsnorm(x_ref[r, :], g_ref[...]).astype(BF16) for r in groups]
    for r, hb in zip(groups, hbs):
        def mm(a, b):
            return jnp.dot(hb, w_ref[:, a:b], preferred_element_type=F32)

        cos, sa, sb = cos_ref[r, :], sa_ref[r, :], sb_ref[r, :]
        apre_ref[r, :] = (mm(*EVEN["a_val"]) * _sigmoid(mm(*EVEN["a_glu"]))).astype(BF16)
        ga_ref[r, :] = _silu(mm(*EVEN["a_gate"])).astype(BF16)
        gb_ref[r, :] = _silu(mm(*EVEN["b_gate"])).astype(BF16)
        q_ref[r, :] = (_rope_cols(mm(*EVEN["q"]), cos, sa, sb) * (HEAD_DIM ** -0.5 * LOG2E)).astype(BF16)
        kv = mm(EVEN["k"][0], EVEN["v"][1])
        k_ref[r, :] = _rope_cols(kv[:, 0:LANES], cos, sa, sb).astype(BF16)
        v_ref[r, :] = kv[:, LANES:2 * LANES].astype(BF16)


def _inproj0(x2, g, w, cos, sa, sb, seq, tm):
    t = x2.shape[0]
    ns = seq // tm
    row = lambda i: (i, 0)
    pos = lambda i: (i % ns, 0)
    full = lambda i: (0, 0)
    widths = (CONV_CH, CONV_CH, HALF, LANES, LANES, HALF)
    return pl.pallas_call(
        _inproj0_kernel,
        grid=(t // tm,),
        in_specs=[pl.BlockSpec((tm, D_MODEL), row), pl.BlockSpec((1, D_MODEL), full),
                  pl.BlockSpec((D_MODEL, IN_EVEN), full),
                  pl.BlockSpec((tm, LANES), pos), pl.BlockSpec((tm, LANES), pos), pl.BlockSpec((tm, LANES), pos)],
        out_specs=[pl.BlockSpec((tm, wd), row) for wd in widths],
        out_shape=[jax.ShapeDtypeStruct((t, wd), BF16) for wd in widths],
        compiler_params=_cparams(("parallel",)),
        name="inproj0",
    )(x2, g, w, cos, sa, sb)


def _convmod_kernel(xc_ref, xp_ref, xn_ref, ga_ref, cw_ref, cb_ref, lg_ref, lb_ref, o_ref, rol_ref, *, ts, rs):
    i = pl.program_id(1)
    last = pl.num_programs(1) - 1
    xp, xn = xp_ref[0], xn_ref[0]
    padded = jnp.concatenate([jnp.where(i > 0, xp, jnp.zeros_like(xp)), xc_ref[0],
                              jnp.where(i < last, xn, jnp.zeros_like(xn))], axis=0)
    rows = ts + 2 * HALO
    rol_ref[0] = padded.astype(F32)
    for rb in range(0, rows, BLOCK):
        nr = min(BLOCK, rows - rb)
        nk = min(BLOCK + HALO, rows - rb)
        r_i = lax.broadcasted_iota(jnp.int32, (nr, nk), 0)
        c_i = lax.broadcasted_iota(jnp.int32, (nr, nk), 1)
        window = padded[rb:rb + nk]
        for j in range(1, SUBLANES):
            shift = jnp.where(c_i == r_i + j, 1.0, 0.0).astype(BF16)
            rol_ref[j, rb:rb + nr] = jnp.dot(shift, window, preferred_element_type=F32)
    first = HALO - (CONV_WIDTH - 1) // 2

    for r0 in range(0, ts, rs):
        acc = jnp.broadcast_to(cb_ref[...], (rs, CONV_CH))
        for w in range(CONV_WIDTH):
            a, j = divmod(first + w, SUBLANES)
            lo = r0 + a * SUBLANES
            acc = acc + rol_ref[j, lo:lo + rs, :] * jnp.concatenate([cw_ref[w]] * (rs // SUBLANES), axis=0)
        mu = jnp.mean(acc, axis=-1, keepdims=True)
        xc = acc - mu
        var = jnp.mean(xc * xc, axis=-1, keepdims=True)
        y = xc * lax.rsqrt(var + EPS) * lg_ref[...] + lb_ref[...]
        o_ref[0, r0:r0 + rs, :] = (_silu(y) * ga_ref[0, r0:r0 + rs, :].astype(F32)).astype(BF16)


def _halo_specs(ts, seq, width):
    hb = ts // HALO
    nh = seq // HALO
    return [pl.BlockSpec((1, ts, width), lambda b, i: (b, i, 0)),
            pl.BlockSpec((1, HALO, width), lambda b, i: (b, jnp.maximum(i * hb - 1, 0), 0)),
            pl.BlockSpec((1, HALO, width), lambda b, i: (b, jnp.minimum((i + 1) * hb, nh - 1), 0))]


def _convmod(apre, ga, cw, cb, lg, lb, ts=1024, rs=32):
    bsz, seq, _ = apre.shape
    vec = pl.BlockSpec((1, CONV_CH), lambda b, i: (0, 0))
    return pl.pallas_call(
        functools.partial(_convmod_kernel, ts=ts, rs=rs),
        grid=(bsz, seq // ts),
        in_specs=_halo_specs(ts, seq, CONV_CH) + [
            pl.BlockSpec((1, ts, CONV_CH), lambda b, i: (b, i, 0)),
            pl.BlockSpec((CONV_WIDTH, SUBLANES, CONV_CH), lambda b, i: (0, 0, 0)), vec, vec, vec],
        out_specs=pl.BlockSpec((1, ts, CONV_CH), lambda b, i: (b, i, 0)),
        out_shape=jax.ShapeDtypeStruct((bsz, seq, CONV_CH), BF16),
        scratch_shapes=[pltpu.VMEM((SUBLANES, ts + 2 * HALO, CONV_CH), F32)],
        compiler_params=_cparams(("parallel", "parallel")),
        name="convmod",
    )(apre, apre, apre, ga, cw, cb, lg, lb)


def _swa_kernel(sink_ref, q_ref, kp_ref, kc_ref, kn_ref, vp_ref, vc_ref, vn_ref, gb_ref, o_ref, *, nblk):
    n = pl.program_id(1)
    last = pl.num_programs(1) - 1
    kcat = jnp.concatenate([kp_ref[0], kc_ref[0], kn_ref[0]], axis=0)
    vcat = jnp.concatenate([vp_ref[0], vc_ref[0], vn_ref[0]], axis=0)
    vaug = jnp.concatenate([vcat, jnp.ones_like(vcat)], axis=1)
    rows = lax.broadcasted_iota(jnp.int32, (2 * BLOCK, 3 * BLOCK), 0)
    qi = jnp.where(rows >= BLOCK, rows - BLOCK, rows)
    ci = lax.broadcasted_iota(jnp.int32, (2 * BLOCK, 3 * BLOCK), 1)
    rel = ci - BLOCK - qi
    band = (rel >= -BLOCK) & (rel <= BLOCK)
    first_key = jnp.where(n > 0, 0, BLOCK)
    end_key = jnp.where(n < last, 3 * BLOCK, 2 * BLOCK)
    lane = lax.broadcasted_iota(jnp.int32, (BLOCK, LANES), 1)
    low = lane < HEAD_DIM
    row1 = lax.broadcasted_iota(jnp.int32, (2 * BLOCK, 1), 0)
    npair = SWA_HEADS // 2
    sinks = [jnp.where(row1 < BLOCK, sink_ref[j], sink_ref[j + npair]) * LOG2E for j in range(npair)]
    masks = []
    for b in range(nblk):
        mask = band
        if b == 0:
            mask = mask & (ci >= first_key)
        if b == nblk - 1:
            mask = mask & (ci < end_key)
        masks.append(mask)
    csl = [slice(j * LANES, (j + 1) * LANES) for j in range(npair)]
    for b in range(nblk):
        rsl = slice(b * BLOCK, (b + 1) * BLOCK)
        kwin = kcat[b * BLOCK:(b + 3) * BLOCK]
        vwin = vaug[b * BLOCK:(b + 3) * BLOCK]
        qqs = []
        for j in range(npair):
            qv = q_ref[0, rsl, csl[j]]
            zero = jnp.zeros_like(qv)
            qqs.append(jnp.concatenate([jnp.where(low, qv, zero), jnp.where(low, zero, qv)], axis=0))
        ss = [jnp.where(masks[b], lax.dot_general(qq, kwin, (((1,), (1,)), ((), ())), preferred_element_type=F32),
                        NEG_BIG) for qq in qqs]
        ms = [jnp.maximum(jnp.max(ss[j], axis=-1, keepdims=True), sinks[j]) for j in range(npair)]
        pvs = [jnp.dot(jnp.exp2(ss[j] - ms[j]).astype(BF16), vwin, preferred_element_type=F32) for j in range(npair)]
        for j in range(npair):
            pv = pvs[j][:, 0:LANES] / (pvs[j][:, LANES:2 * LANES] + jnp.exp2(sinks[j] - ms[j]))
            o = jnp.where(low, pv[:BLOCK], pv[BLOCK:])
            o_ref[0, rsl, csl[j]] = (o * gb_ref[0, rsl, csl[j]].astype(F32)).astype(BF16)


def _swa(sink, q, k, v, gb, nblk=8):
    bsz, seq, _ = q.shape
    rows = nblk * BLOCK
    nb = seq // BLOCK
    cur = lambda b, n: (b, n, 0)
    prv = lambda b, n: (b, jnp.maximum(n * nblk - 1, 0), 0)
    nxt = lambda b, n: (b, jnp.minimum((n + 1) * nblk, nb - 1), 0)
    edge = lambda im: pl.BlockSpec((1, BLOCK, LANES), im)
    mid = pl.BlockSpec((1, rows, LANES), cur)
    wide = pl.BlockSpec((1, rows, HALF), cur)
    return pl.pallas_call(
        functools.partial(_swa_kernel, nblk=nblk),
        grid=(bsz, seq // rows),
        in_specs=[pl.BlockSpec(memory_space=pltpu.SMEM), wide,
                  edge(prv), mid, edge(nxt), edge(prv), mid, edge(nxt), wide],
        out_specs=wide,
        out_shape=jax.ShapeDtypeStruct((bsz, seq, HALF), BF16),
        compiler_params=_cparams(("parallel", "parallel")),
        name="swa",
    )(sink, q, k, k, k, v, v, v, gb)


def _mid_kernel(x_ref, a_ref, o_ref, wo_ref, g_ref, w_ref, cos_ref, sa_ref, sb_ref,
                x1_ref, q_ref, k_ref, v_ref, cg_ref, zg_ref, xbc_ref, dt_ref):
    parts = 2
    rows = x_ref.shape[0] // parts
    groups = [slice(part * rows, (part + 1) * rows) for part in range(parts)]
    hbs = []
    for r in groups:
        x1 = (x_ref[r, :] + jnp.dot(a_ref[r, :], wo_ref[0:HALF, :], preferred_element_type=F32)
              + jnp.dot(o_ref[r, :], wo_ref[HALF:D_MODEL, :], preferred_element_type=F32))
        x1_ref[r, :] = x1
        hbs.append(_rmsnorm(x1, g_ref[...]).astype(BF16))
    ones = jnp.ones((rows, LANES), BF16)
    for r, hb in zip(groups, hbs):
        def mm(a, b):
            return jnp.dot(hb, w_ref[:, a:b], preferred_element_type=F32)

        cos, sa, sb = cos_ref[r, :], sa_ref[r, :], sb_ref[r, :]
        q_ref[r, :] = (_rope_cols(mm(*ODD["q"]), cos, sa, sb) * (HEAD_DIM ** -0.5 * LOG2E)).astype(BF16)
        k_ref[r, :] = _rope_cols(mm(*ODD["k"]), cos, sa, sb).astype(BF16)
        v = mm(*ODD["v"]).astype(BF16)
        for h in range(DIFF_HEADS):
            v_ref[r, 2 * h * LANES:(2 * h + 1) * LANES] = v[:, h * LANES:(h + 1) * LANES]
            v_ref[r, (2 * h + 1) * LANES:(2 * h + 2) * LANES] = ones
        dt_ref[r, :] = mm(*ODD["dt"])
        cg_ref[r, :] = _silu(mm(*ODD["c_gate"])).astype(BF16)
        zg_ref[r, :] = _silu(mm(*ODD["z"])).astype(BF16)
        xbc_ref[r, :] = mm(*ODD["xbc"]).astype(BF16)


def _mid(x2, a2, o2, wo, g, w, cos, sa, sb, seq, tm):
    t = x2.shape[0]
    ns = seq // tm
    row = lambda i: (i, 0)
    pos = lambda i: (i % ns, 0)
    full = lambda i: (0, 0)
    outs = ((D_MODEL, F32), (HALF, BF16), (HALF, BF16), (2 * HALF, BF16), (HALF, BF16), (SSM_INNER, BF16),
            (SSM_CONV_DIM, BF16), (LANES, F32))
    return pl.pallas_call(
        _mid_kernel,
        grid=(t // tm,),
        in_specs=[pl.BlockSpec((tm, D_MODEL), row), pl.BlockSpec((tm, HALF), row), pl.BlockSpec((tm, HALF), row),
                  pl.BlockSpec((D_MODEL, D_MODEL), full), pl.BlockSpec((1, D_MODEL), full),
                  pl.BlockSpec((D_MODEL, IN_ODD_PAD), full),
                  pl.BlockSpec((tm, LANES), pos), pl.BlockSpec((tm, LANES), pos), pl.BlockSpec((tm, LANES), pos)],
        out_specs=[pl.BlockSpec((tm, wd), row) for wd, _ in outs],
        out_shape=[jax.ShapeDtypeStruct((t, wd), dt) for wd, dt in outs],
        compiler_params=_cparams(("parallel",)),
        name="outproj0_inproj1",
    )(x2, a2, o2, wo, g, w, cos, sa, sb)


def _diff_kernel(lq1_ref, lk1_ref, lq2_ref, lk2_ref, g_ref, q_ref, qn_ref, k_ref, v_ref, cg_ref, o_ref,
                 s0_ref, s1_ref, mx0_ref, mx1_ref, m_ref, acc_ref, *, tq, kc, lam_init):
    nsub = q_ref.shape[1] // tq
    nch = k_ref.shape[1] // kc
    lane = lax.broadcasted_iota(jnp.int32, (tq, LANES), 1)

    def stack_maps(q):
        zero = jnp.zeros_like(q)
        return jnp.concatenate([jnp.where(lane < HEAD_DIM, q, zero), jnp.where(lane < HEAD_DIM, zero, q)], axis=0)

    slots = ((s0_ref, mx0_ref), (s1_ref, mx1_ref))

    def scores(qs, c, slot):
        s_ref, mx_ref = slots[slot]
        kch = k_ref[0, pl.ds(pl.multiple_of(c * kc, kc), kc), :]
        s = lax.dot_general(qs, kch, (((1,), (1,)), ((), ())), preferred_element_type=F32)
        s_ref[...] = s
        mx_ref[...] = jnp.broadcast_to(jnp.max(s, axis=-1, keepdims=True), mx_ref.shape)

    def update(c, slot, sub):
        s_ref, mx_ref = slots[slot]
        vch = v_ref[0, pl.ds(pl.multiple_of(c * kc, kc), kc), :]
        m_old = m_ref[sub]
        m_new = jnp.maximum(m_old, mx_ref[...])
        alpha = jnp.exp2(m_old - m_new)
        p = jnp.exp2(s_ref[...] - jnp.concatenate([m_new] * (kc // LANES), axis=1)).astype(BF16)
        acc_ref[sub] = acc_ref[sub] * jnp.concatenate([alpha, alpha], axis=1) + jnp.dot(
            p, vch, preferred_element_type=F32)
        m_ref[sub] = m_new

    @pl.when(pl.program_id(2) == 0)
    def _():
        scores(stack_maps(q_ref[0, 0:tq]), 0, 0)

    qqs = [stack_maps(q_ref[0, sub * tq:(sub + 1) * tq]) for sub in range(nsub)] + [stack_maps(qn_ref[0])]
    lam = (jnp.exp(jnp.sum(lq1_ref[...] * lk1_ref[...], axis=-1, keepdims=True))
           - jnp.exp(jnp.sum(lq2_ref[...] * lk2_ref[...], axis=-1, keepdims=True)) + lam_init)
    for sub in range(nsub):
        rows = slice(sub * tq, (sub + 1) * tq)
        m_ref[sub] = jnp.full(m_ref.shape[1:], NEG_BIG, F32)
        acc_ref[sub] = jnp.zeros(acc_ref.shape[1:], F32)
        for c in range(nch):
            if c + 1 < nch:
                scores(qqs[sub], c + 1, (c + 1) % 2)
            else:
                scores(qqs[sub + 1], 0, 0)
            update(c, c % 2, sub)
        o0 = acc_ref[sub, 0:tq, 0:LANES] / acc_ref[sub, 0:tq, LANES:2 * LANES]
        o1 = acc_ref[sub, tq:2 * tq, 0:LANES] / acc_ref[sub, tq:2 * tq, LANES:2 * LANES]
        o = o0 - lam * o1
        o = o * lax.rsqrt(jnp.mean(o * o, axis=-1, keepdims=True) + EPS) * g_ref[...] * (1.0 - lam_init)
        o_ref[0, rows, :] = (o * cg_ref[0, rows, :].astype(F32)).astype(BF16)


def _diff_attn(lq1, lk1, lq2, lk2, g, q, k, v, cg, lam_init, tq=512):
    bsz, seq, _ = q.shape
    kc = next(c for c in (1024, 512, 256) if seq % (2 * c) == 0)
    nsub = 2 if seq <= 4096 and seq % (2 * tq) == 0 else 1
    nsteps = seq // (nsub * tq)
    small = pl.BlockSpec((1, HEAD_DIM), lambda b, h, i: (0, 0))
    tile = pl.BlockSpec((1, nsub * tq, LANES), lambda b, h, i: (b, i, h))
    next_tile = pl.BlockSpec((1, tq, LANES), lambda b, h, i: (b, jnp.minimum(i + 1, nsteps - 1) * nsub, h))
    return pl.pallas_call(
        functools.partial(_diff_kernel, tq=tq, kc=kc, lam_init=lam_init),
        grid=(bsz, DIFF_HEADS, nsteps),
        in_specs=[small, small, small, small, pl.BlockSpec((1, LANES), lambda b, h, i: (0, 0)), tile, next_tile,
                  pl.BlockSpec((1, seq, LANES), lambda b, h, i: (b, 0, h)),
                  pl.BlockSpec((1, seq, 2 * LANES), lambda b, h, i: (b, 0, h)), tile],
        out_specs=tile,
        out_shape=jax.ShapeDtypeStruct((bsz, seq, HALF), BF16),
        scratch_shapes=[pltpu.VMEM((2 * tq, kc), F32), pltpu.VMEM((2 * tq, kc), F32),
                        pltpu.VMEM((2 * tq, LANES), F32), pltpu.VMEM((2 * tq, LANES), F32),
                        pltpu.VMEM((nsub, 2 * tq, LANES), F32), pltpu.VMEM((nsub, 2 * tq, 2 * LANES), F32)],
        compiler_params=_cparams(("parallel", "parallel", "arbitrary")),
        name="diff_attn",
    )(lq1, lk1, lq2, lk2, g, q, q, k, v, cg)


def _split3(x):
    hi = x.astype(BF16)
    r = x - hi.astype(F32)
    mid = r.astype(BF16)
    lo = (r - mid.astype(F32)).astype(BF16)
    return hi, mid, lo


def _ssd_fwd_kernel(xc_ref, xp_ref, xn_ref, dt_ref, cw_ref, cb_ref, dtb_ref, a_ref, dsk_ref, y_ref, xconv_ref,
                    carry_ref, xf_ref):
    L = BLOCK
    nchunk = xc_ref.shape[1] // L
    c = pl.program_id(1)
    last = pl.num_programs(1) - 1

    @pl.when(c == 0)
    def _():
        carry_ref[...] = jnp.zeros(carry_ref.shape, F32)

    xp, xn = xp_ref[0], xn_ref[0]
    halo_p = jnp.where(c > 0, xp, jnp.zeros_like(xp))
    halo_n = jnp.where(c < last, xn, jnp.zeros_like(xn))
    r_i = lax.broadcasted_iota(jnp.int32, (L, L + 2 * HALO), 0)
    c_i = lax.broadcasted_iota(jnp.int32, (L, L + 2 * HALO), 1)
    first = HALO - (SSM_CONV - 1) // 2
    shifts = [None if first + w == HALO else jnp.where(c_i == r_i + first + w, 1.0, 0.0).astype(BF16)
              for w in range(SSM_CONV)]
    wd = 2 * LANES
    for nb in range(SSM_CONV_DIM // wd):
        cols = slice(nb * wd, (nb + 1) * wd)
        padded = jnp.concatenate([halo_p[:, cols], xc_ref[0, :, cols], halo_n[:, cols]], axis=0)
        for ci in range(nchunk):
            r0 = ci * L
            window = padded[r0:r0 + L + 2 * HALO]
            acc = jnp.broadcast_to(cb_ref[:, cols], (L, wd))
            for w in range(SSM_CONV):
                tap = (xc_ref[0, r0:r0 + L, cols].astype(F32) if shifts[w] is None
                       else jnp.dot(shifts[w], window, preferred_element_type=F32))
                acc = acc + tap * cw_ref[w:w + 1, cols]
            xbc = _silu(acc)
            xconv_ref[0, r0:r0 + L, cols] = xbc.astype(BF16)
            xf_ref[r0:r0 + L, cols] = xbc

    def xcols(k, a, b, dtype):
        return xf_ref[k * L:(k + 1) * L, a:b].astype(dtype)

    def emit(k, j, y):
        cols = slice(j * LANES, (j + 1) * LANES)
        y_ref[0, k * L:(k + 1) * L, cols] = y + xcols(k, j * LANES, (j + 1) * LANES, F32) * dsk_ref[:, cols]

    _ssd_scan_block(xcols, list(range(nchunk)), dt_ref, dtb_ref, a_ref, carry_ref, emit, reverse=False, col0=0)


def _ssd_bwd_kernel(xconv_ref, dt_ref, dtb_ref, a_ref, yf_ref, zg_ref, sg_ref, x1_ref, c_ref, wo_ref, fg_ref, o_ref,
                    carry_ref, yb_ref):
    L = BLOCK
    nchunk = xconv_ref.shape[1] // L

    @pl.when(pl.program_id(1) == 0)
    def _():
        carry_ref[...] = jnp.zeros(carry_ref.shape, F32)

    x2 = x1_ref[0] + jnp.dot(c_ref[0], wo_ref[0:HALF, :], preferred_element_type=F32)

    def xcols(k, a, b, dtype):
        return xconv_ref[0, k * L:(k + 1) * L, a:b].astype(dtype)

    def emit(k, j, y):
        yb_ref[k * L:(k + 1) * L, j * LANES:(j + 1) * LANES] = y

    _ssd_scan_block(xcols, list(reversed(range(nchunk))), dt_ref, dtb_ref, a_ref, carry_ref, emit,
                    reverse=True, col0=SSM_HEADS)
    y = (yf_ref[0] + yb_ref[...]) * zg_ref[0].astype(F32)
    d = _rmsnorm(y, sg_ref[...]).astype(BF16)
    x2 = x2 + jnp.dot(d, wo_ref[HALF:D_MODEL, :], preferred_element_type=F32)
    o_ref[0] = _rmsnorm(x2, fg_ref[...])


def _ssd_scan_block(xcols, order, dt_ref, dtb_ref, a_ref, carry_ref, emit, *, reverse, col0):
    L = BLOCK
    nchunk = len(order)
    chunk = lambda m, k: m[k * L:(k + 1) * L]

    z = dt_ref[0] + dtb_ref[...]
    dt = jnp.maximum(z, 0.0) + jnp.log(1.0 + jnp.exp(-jnp.abs(z)))
    a = dt * a_ref[...]
    r_i = lax.broadcasted_iota(jnp.int32, (L, L), 0)
    c_i = lax.broadcasted_iota(jnp.int32, (L, L), 1)
    causal = (r_i <= c_i) if reverse else (r_i >= c_i)
    tri = jnp.where(causal, 1.0, 0.0).astype(BF16)
    pieces = _split3(a * LOG2E)
    acum = [sum(jnp.dot(tri, chunk(pc, k), preferred_element_type=F32) for pc in pieces) for k in range(nchunk)]
    acum_t = [m.T for m in acum]
    edge = 0 if reverse else L - 1
    a_end = [m[edge:edge + 1, :] for m in acum]

    lane = lax.broadcasted_iota(jnp.int32, (L, LANES), 1)
    low = lane < HEAD_DIM
    lane1 = lax.broadcasted_iota(jnp.int32, (1, LANES), 1)

    cbs, bts, cms = [], [], []
    for k in range(nchunk):
        bm = [xcols(k, SSM_INNER + g * SSM_STATE, SSM_INNER + (g + 1) * SSM_STATE, BF16) for g in range(2)]
        cm = [xcols(k, SSM_INNER + (2 + g) * SSM_STATE, SSM_INNER + (3 + g) * SSM_STATE, BF16) for g in range(2)]
        cbs.append([lax.dot_general(cm[g], bm[g], (((1,), (1,)), ((), ())), preferred_element_type=F32)
                    for g in range(2)])
        bts.append([bm[g].T for g in range(2)])
        cms.append(cm)
    for j in range(SSM_HEADS // 2):
        g = j // 2
        ca = col0 + 2 * j
        ydiag, states, decay_out, decay_in = [], [], [], []
        for k in range(nchunk):
            acb = [jnp.broadcast_to(acum[k][:, ca + hh:ca + hh + 1], (L, LANES)) for hh in range(2)]
            ac_pair = jnp.where(low, acb[0], acb[1])
            dtk = chunk(dt, k)
            dt_pair = jnp.where(low, dtk[:, ca:ca + 1], dtk[:, ca + 1:ca + 2])
            end_pair = jnp.where(lane1 < HEAD_DIM, a_end[k][:, ca:ca + 1], a_end[k][:, ca + 1:ca + 2])
            xdt = xcols(k, j * LANES, (j + 1) * LANES, F32) * dt_pair
            xdt_b = xdt.astype(BF16)
            zero = jnp.zeros_like(xdt_b)
            ms = []
            for hh in range(2):
                seg = acb[hh] - acum_t[k][ca + hh:ca + hh + 1, :]
                dec = jnp.exp2(jnp.where(causal, seg, NEG_BIG))
                ms.append((cbs[k][g] * dec).astype(BF16))
            lhs = jnp.concatenate(ms, axis=1)
            rhs = jnp.concatenate([jnp.where(low, xdt_b, zero), jnp.where(low, zero, xdt_b)], axis=0)
            ydiag.append(jnp.dot(lhs, rhs, preferred_element_type=F32))
            states.append(jnp.dot(bts[k][g], (xdt * jnp.exp2(end_pair - ac_pair)).astype(BF16),
                                  preferred_element_type=F32))
            decay_out.append(jnp.exp2(end_pair))
            decay_in.append(jnp.exp2(ac_pair))
        carry = carry_ref[j]
        for k in order:
            emit(k, j, ydiag[k] + jnp.dot(cms[k][g], carry.astype(BF16), preferred_element_type=F32) * decay_in[k])
            carry = carry * decay_out[k] + states[k]
        carry_ref[j] = carry


_SSD_CARRY = pltpu.VMEM((SSM_HEADS // 2, SSM_STATE, LANES), F32)
_SSD_ROWS = 8 * BLOCK


def _ssd_fwd(xbc, dt, cw, cb, dtb, a, dsk):
    bsz, seq, _ = xbc.shape
    vec = lambda wd: pl.BlockSpec((1, wd), lambda b, c: (0, 0))
    chunk = lambda wd: pl.BlockSpec((1, _SSD_ROWS, wd), lambda b, c: (b, c, 0))
    return pl.pallas_call(
        _ssd_fwd_kernel,
        grid=(bsz, seq // _SSD_ROWS),
        in_specs=_halo_specs(_SSD_ROWS, seq, SSM_CONV_DIM) + [
            chunk(LANES), pl.BlockSpec((SSM_CONV, SSM_CONV_DIM), lambda b, c: (0, 0)), vec(SSM_CONV_DIM),
            vec(LANES), vec(LANES), vec(SSM_INNER)],
        out_specs=[chunk(SSM_INNER), chunk(SSM_CONV_DIM)],
        out_shape=[jax.ShapeDtypeStruct((bsz, seq, SSM_INNER), F32),
                   jax.ShapeDtypeStruct((bsz, seq, SSM_CONV_DIM), BF16)],
        scratch_shapes=[_SSD_CARRY, pltpu.VMEM((_SSD_ROWS, SSM_CONV_DIM), F32)],
        compiler_params=_cparams(("parallel", "arbitrary")),
        name="ssd_fwd",
    )(xbc, xbc, xbc, dt, cw, cb, dtb, a, dsk)


def _ssd_bwd_final(xconv, dt, dtb, a, yf, zg, sg, x1, c, wo, fg):
    bsz, seq, _ = xconv.shape
    nc = seq // _SSD_ROWS
    vec = lambda wd: pl.BlockSpec((1, wd), lambda b, c: (0, 0))
    chunk = lambda wd: pl.BlockSpec((1, _SSD_ROWS, wd), lambda b, c: (b, nc - 1 - c, 0))
    return pl.pallas_call(
        _ssd_bwd_kernel,
        grid=(bsz, nc),
        in_specs=[chunk(SSM_CONV_DIM), chunk(LANES), vec(LANES), vec(LANES), chunk(SSM_INNER), chunk(SSM_INNER),
                  vec(SSM_INNER), chunk(D_MODEL), chunk(HALF), pl.BlockSpec((D_MODEL, D_MODEL), lambda b, c: (0, 0)),
                  vec(D_MODEL)],
        out_specs=chunk(D_MODEL),
        out_shape=jax.ShapeDtypeStruct((bsz, seq, D_MODEL), F32),
        scratch_shapes=[_SSD_CARRY, pltpu.VMEM((_SSD_ROWS, SSM_INNER), F32)],
        compiler_params=_cparams(("parallel", "arbitrary")),
        name="ssd_bwd_final",
    )(xconv, dt, dtb, a, yf, zg, sg, x1, c, wo, fg)


def _rope_tables(seq):
    inv = 1.0 / (ROPE_THETA ** (jnp.arange(0, HEAD_DIM, 2, dtype=F32) / HEAD_DIM))
    f = jnp.arange(seq, dtype=F32)[:, None] * inv[None, :]
    cos, sin = jnp.cos(f), jnp.sin(f)
    cos = jnp.concatenate([cos] * 4, axis=-1)
    sin = jnp.concatenate([sin] * 4, axis=-1)
    first_half = (jnp.arange(LANES) % HEAD_DIM) < HEAD_DIM // 2
    return cos, jnp.where(first_half, -sin, 0.0), jnp.where(first_half, 0.0, sin)


def _pair_perm():
    idx = []
    for j in range(SWA_HEADS // 2):
        idx += list(range(j * HEAD_DIM, (j + 1) * HEAD_DIM))
        idx += list(range((j + 4) * HEAD_DIM, (j + 5) * HEAD_DIM))
    return jnp.asarray(idx, jnp.int32)


def _pad_lanes(v, offset):
    return jnp.zeros((1, LANES), F32).at[0, offset:offset + v.shape[0]].set(v.astype(F32))


def _prepare(norm_g, w_in0, conv_w, conv_b, conv_ln_g, conv_ln_b, sink, w_out0, w_in1, lambda_q1, lambda_k1,
             lambda_q2, lambda_k2, diff_norm_g, ssm_conv_w, ssm_conv_b, dt_bias_f, dt_bias_b, a_log_f, a_log_b,
             d_skip, ssm_norm_g, w_out1, final_norm_g):
    perm = _pair_perm()
    w0 = w_in0[0].astype(BF16)
    (q0, q1), (g0, g1) = EVEN["q"], EVEN["b_gate"]
    w0 = jnp.concatenate([w0[:, :q0], w0[:, q0:q1][:, perm], w0[:, q1:g0], w0[:, g0:g1][:, perm]], axis=1)
    wo0 = w_out0[0].astype(BF16)
    wo0 = jnp.concatenate([wo0[:HALF], wo0[HALF:][perm]], axis=0)
    w1 = jnp.pad(w_in1[0].astype(BF16), ((0, 0), (0, IN_ODD_PAD - IN_ODD)))
    row = lambda v: v.astype(F32).reshape(1, -1)
    return dict(
        g0=row(norm_g[0]), g1=row(norm_g[1]), w0=w0, wo0=wo0, w1=w1,
        wo1=w_out1[0].astype(BF16), cb=row(conv_b[0]), lg=row(conv_ln_g[0]),
        cw=jnp.broadcast_to(conv_w[0].astype(F32)[:, None, :], (CONV_WIDTH, SUBLANES, CONV_CH)),
        lb=row(conv_ln_b[0]), sink=sink[0].astype(F32),
        lq1=row(lambda_q1[0]), lk1=row(lambda_k1[0]), lq2=row(lambda_q2[0]), lk2=row(lambda_k2[0]),
        dg=row(diff_norm_g[0]), scw=ssm_conv_w[0].astype(F32), scb=row(ssm_conv_b[0]),
        dtb_f=_pad_lanes(dt_bias_f[0], 0), dtb_b=_pad_lanes(dt_bias_b[0], SSM_HEADS),
        a_f=_pad_lanes(-jnp.exp(a_log_f[0].astype(F32)), 0), a_b=_pad_lanes(-jnp.exp(a_log_b[0].astype(F32)), SSM_HEADS),
        dsk=row(jnp.repeat(d_skip[0].astype(F32), HEAD_DIM)), sg=row(ssm_norm_g[0]), fg=row(final_norm_g))


def _trunk(x, p, tm=512):
    bsz, seq, _ = x.shape
    assert seq % tm == 0 and seq % _SSD_ROWS == 0
    t = bsz * seq
    lam_init = 0.8 - 0.6 * math.exp(-0.3 * 1)
    cos, sa, sb = _rope_tables(seq)
    x2 = x.reshape(t, D_MODEL)
    apre, ga, q, k, v, gb = _inproj0(x2, p["g0"], p["w0"], cos, sa, sb, seq, tm)
    r3 = lambda arr: arr.reshape(bsz, seq, arr.shape[-1])
    a = _convmod(r3(apre), r3(ga), p["cw"], p["cb"], p["lg"], p["lb"])
    o = _swa(p["sink"], r3(q), r3(k), r3(v), r3(gb))
    x1, q1, k1, v1, cg, zg, xbc, dt = _mid(x2, a.reshape(t, HALF), o.reshape(t, HALF), p["wo0"], p["g1"], p["w1"],
                                           cos, sa, sb, seq, tm)
    c = _diff_attn(p["lq1"], p["lk1"], p["lq2"], p["lk2"], p["dg"], r3(q1), r3(k1), r3(v1), r3(cg), lam_init)
    yf, xconv = _ssd_fwd(r3(xbc), r3(dt), p["scw"], p["scb"], p["dtb_f"], p["a_f"], p["dsk"])
    return _ssd_bwd_final(xconv, r3(dt), p["dtb_b"], p["a_b"], yf, r3(zg), p["sg"], r3(x1), c, p["wo1"], p["fg"])


def kernel(x_prompt, x_sample, norm_g, w_in0, conv_w, conv_b, conv_ln_g, conv_ln_b, sink, w_out0, w_in1, lambda_q1, lambda_k1, lambda_q2, lambda_k2, diff_norm_g, ssm_conv_w, ssm_conv_b, dt_bias_f, dt_bias_b, a_log_f, a_log_b, d_skip, ssm_norm_g, w_out1, final_norm_g):
    p = _prepare(norm_g, w_in0, conv_w, conv_b, conv_ln_g, conv_ln_b, sink, w_out0, w_in1, lambda_q1, lambda_k1,
                 lambda_q2, lambda_k2, diff_norm_g, ssm_conv_w, ssm_conv_b, dt_bias_f, dt_bias_b, a_log_f, a_log_b,
                 d_skip, ssm_norm_g, w_out1, final_norm_g)
    return (_trunk(x_prompt, p), _trunk(x_sample, p))
```

```python
import functools
import math

import jax
import jax.numpy as jnp
from jax import lax
from jax.experimental import pallas as pl
from jax.experimental.pallas import tpu as pltpu

F32 = jnp.float32
BF16 = jnp.bfloat16

D_MODEL = 1024
EPS = 1e-6
ROPE_THETA = 10000.0
HEAD_DIM = 64
LANES = 128
SUBLANES = 8
HALO = 16
CONV_CH = 512
CONV_WIDTH = 31
SWA_HEADS = 8
BLOCK = 128
DIFF_HEADS = 4
SSM_INNER = 512
SSM_HEADS = 8
SSM_STATE = 128
SSM_CONV = 5
SSM_CONV_DIM = 1024
HALF = D_MODEL // 2


def _col_ranges(**widths):
    out, start = {}, 0
    for name, width in widths.items():
        out[name] = (start, start + width)
        start += width
    return out


EVEN = _col_ranges(a_val=CONV_CH, a_glu=CONV_CH, a_gate=CONV_CH, q=HALF, k=LANES, v=LANES, b_gate=HALF)
ODD = _col_ranges(q=HALF, k=HALF, v=HALF, c_gate=HALF, z=SSM_INNER, xbc=SSM_CONV_DIM, dt=LANES)
IN_EVEN = EVEN["b_gate"][1]
IN_ODD = ODD["dt"][0] + 2 * SSM_HEADS
IN_ODD_PAD = ODD["dt"][1]
NEG_BIG = -1e30
LOG2E = math.log2(math.e)
VMEM_LIMIT = 56 * 1024 * 1024


def _cparams(sem):
    return pltpu.CompilerParams(dimension_semantics=sem, vmem_limit_bytes=VMEM_LIMIT)


def _sigmoid(x):
    return 1.0 / (1.0 + jnp.exp2(x * (-LOG2E)))


def _silu(x):
    return x * _sigmoid(x)


def _rmsnorm(x, g):
    return x * lax.rsqrt(jnp.mean(x * x, axis=-1, keepdims=True) + EPS) * g


def _rope_cols(x, cos, sa, sb):
    outs = []
    for j in range(x.shape[1] // LANES):
        xj = x[:, j * LANES:(j + 1) * LANES]
        outs.append(xj * cos + pltpu.roll(xj, LANES - 32, 1) * sa + pltpu.roll(xj, 32, 1) * sb)
    return outs[0] if len(outs) == 1 else jnp.concatenate(outs, axis=1)


def _inproj0_kernel(x_ref, g_ref, w_ref, cos_ref, sa_ref, sb_ref,
                    apre_ref, ga_ref, q_ref, k_ref, v_ref, gb_ref):
    parts = 2
    rows = x_ref.shape[0] // parts
    groups = [slice(part * rows, (part + 1) * rows) for part in range(parts)]
    hbs = [_rmsnorm(x_ref[r, :], g_ref[...]).astype(BF16) for r in groups]
    for r, hb in zip(groups, hbs):
        def mm(a, b):
            return jnp.dot(hb, w_ref[:, a:b], preferred_element_type=F32)

        cos, sa, sb = cos_ref[r, :], sa_ref[r, :], sb_ref[r, :]
        q_ref[r, :] = (_rope_cols(mm(*EVEN["q"]), cos, sa, sb) * (HEAD_DIM ** -0.5 * LOG2E)).astype(BF16)
        apre_ref[r, :] = (mm(*EVEN["a_val"]) * _sigmoid(mm(*EVEN["a_glu"]))).astype(BF16)
        ga_ref[r, :] = _silu(mm(*EVEN["a_gate"])).astype(BF16)
        gb_ref[r, :] = _silu(mm(*EVEN["b_gate"])).astype(BF16)
        kv = mm(EVEN["k"][0], EVEN["v"][1])
        k_ref[r, :] = _rope_cols(kv[:, 0:LANES], cos, sa, sb).astype(BF16)
        v_ref[r, :] = kv[:, LANES:2 * LANES].astype(BF16)


def _inproj0(x2, g, w, cos, sa, sb, seq, tm):
    t = x2.shape[0]
    ns = seq // tm
    row = lambda i: (i, 0)
    pos = lambda i: (i % ns, 0)
    full = lambda i: (0, 0)
    widths = (CONV_CH, CONV_CH, HALF, LANES, LANES, HALF)
    return pl.pallas_call(
        _inproj0_kernel,
        grid=(t // tm,),
        in_specs=[pl.BlockSpec((tm, D_MODEL), row), pl.BlockSpec((1, D_MODEL), full),
                  pl.BlockSpec((D_MODEL, IN_EVEN), full),
                  pl.BlockSpec((tm, LANES), pos), pl.BlockSpec((tm, LANES), pos), pl.BlockSpec((tm, LANES), pos)],
        out_specs=[pl.BlockSpec((tm, wd), row) for wd in widths],
        out_shape=[jax.ShapeDtypeStruct((t, wd), BF16) for wd in widths],
        compiler_params=_cparams(("parallel",)),
        name="inproj0",
    )(x2, g, w, cos, sa, sb)


def _convmod_kernel(xc_ref, xp_ref, xn_ref, ga_ref, cw_ref, cb_ref, lg_ref, lb_ref, o_ref, rol_ref, *, ts, rs):
    i = pl.program_id(1)
    last = pl.num_programs(1) - 1
    xp, xn = xp_ref[0], xn_ref[0]
    padded = jnp.concatenate([jnp.where(i > 0, xp, jnp.zeros_like(xp)), xc_ref[0],
                              jnp.where(i < last, xn, jnp.zeros_like(xn))], axis=0)
    rows = ts + 2 * HALO
    rol_ref[0] = padded.astype(F32)
    for rb in range(0, rows, BLOCK):
        nr = min(BLOCK, rows - rb)
        nk = min(BLOCK + HALO, rows - rb)
        r_i = lax.broadcasted_iota(jnp.int32, (nr, nk), 0)
        c_i = lax.broadcasted_iota(jnp.int32, (nr, nk), 1)
        window = padded[rb:rb + nk]
        for j in range(1, SUBLANES):
            shift = jnp.where(c_i == r_i + j, 1.0, 0.0).astype(BF16)
            rol_ref[j, rb:rb + nr] = jnp.dot(shift, window, preferred_element_type=F32)
    first = HALO - (CONV_WIDTH - 1) // 2

    for r0 in range(0, ts, rs):
        acc = jnp.broadcast_to(cb_ref[...], (rs, CONV_CH))
        for w in range(CONV_WIDTH):
            a, j = divmod(first + w, SUBLANES)
            lo = r0 + a * SUBLANES
            acc = acc + rol_ref[j, lo:lo + rs, :] * jnp.concatenate([cw_ref[w]] * (rs // SUBLANES), axis=0)
        mu = jnp.mean(acc, axis=-1, keepdims=True)
        xc = acc - mu
        var = jnp.mean(xc * xc, axis=-1, keepdims=True)
        y = xc * lax.rsqrt(var + EPS) * lg_ref[...] + lb_ref[...]
        o_ref[0, r0:r0 + rs, :] = (_silu(y) * ga_ref[0, r0:r0 + rs, :].astype(F32)).astype(BF16)


def _halo_specs(ts, seq, width):
    hb = ts // HALO
    nh = seq // HALO
    return [pl.BlockSpec((1, ts, width), lambda b, i: (b, i, 0)),
            pl.BlockSpec((1, HALO, width), lambda b, i: (b, jnp.maximum(i * hb - 1, 0), 0)),
            pl.BlockSpec((1, HALO, width), lambda b, i: (b, jnp.minimum((i + 1) * hb, nh - 1), 0))]


def _convmod(apre, ga, cw, cb, lg, lb, ts=1024, rs=32):
    bsz, seq, _ = apre.shape
    vec = pl.BlockSpec((1, CONV_CH), lambda b, i: (0, 0))
    return pl.pallas_call(
        functools.partial(_convmod_kernel, ts=ts, rs=rs),
        grid=(bsz, seq // ts),
        in_specs=_halo_specs(ts, seq, CONV_CH) + [
            pl.BlockSpec((1, ts, CONV_CH), lambda b, i: (b, i, 0)),
            pl.BlockSpec((CONV_WIDTH, SUBLANES, CONV_CH), lambda b, i: (0, 0, 0)), vec, vec, vec],
        out_specs=pl.BlockSpec((1, ts, CONV_CH), lambda b, i: (b, i, 0)),
        out_shape=jax.ShapeDtypeStruct((bsz, seq, CONV_CH), BF16),
        scratch_shapes=[pltpu.VMEM((SUBLANES, ts + 2 * HALO, CONV_CH), F32)],
        compiler_params=_cparams(("parallel", "parallel")),
        name="convmod",
    )(apre, apre, apre, ga, cw, cb, lg, lb)


def _swa_kernel(sink_ref, q_ref, kp_ref, kc_ref, kn_ref, vp_ref, vc_ref, vn_ref, gb_ref, o_ref, *, nblk):
    n = pl.program_id(1)
    last = pl.num_programs(1) - 1
    kcat = jnp.concatenate([kp_ref[0], kc_ref[0], kn_ref[0]], axis=0)
    vcat = jnp.concatenate([vp_ref[0], vc_ref[0], vn_ref[0]], axis=0)
    vaug = jnp.concatenate([vcat, jnp.ones_like(vcat)], axis=1)
    rows = lax.broadcasted_iota(jnp.int32, (2 * BLOCK, 3 * BLOCK), 0)
    qi = jnp.where(rows >= BLOCK, rows - BLOCK, rows)
    ci = lax.broadcasted_iota(jnp.int32, (2 * BLOCK, 3 * BLOCK), 1)
    rel = ci - BLOCK - qi
    band = (rel >= -BLOCK) & (rel <= BLOCK)
    first_key = jnp.where(n > 0, 0, BLOCK)
    end_key = jnp.where(n < last, 3 * BLOCK, 2 * BLOCK)
    lane = lax.broadcasted_iota(jnp.int32, (BLOCK, LANES), 1)
    low = lane < HEAD_DIM
    row1 = lax.broadcasted_iota(jnp.int32, (2 * BLOCK, 1), 0)
    npair = SWA_HEADS // 2
    sinks = [jnp.where(row1 < BLOCK, sink_ref[j], sink_ref[j + npair]) * LOG2E for j in range(npair)]
    masks = []
    for b in range(nblk):
        mask = band
        if b == 0:
            mask = mask & (ci >= first_key)
        if b == nblk - 1:
            mask = mask & (ci < end_key)
        masks.append(mask)
    csl = [slice(j * LANES, (j + 1) * LANES) for j in range(npair)]
    for b in range(nblk):
        rsl = slice(b * BLOCK, (b + 1) * BLOCK)
        kwin = kcat[b * BLOCK:(b + 3) * BLOCK]
        vwin = vaug[b * BLOCK:(b + 3) * BLOCK]
        qqs = []
        for j in range(npair):
            qv = q_ref[0, rsl, csl[j]]
            zero = jnp.zeros_like(qv)
            qqs.append(jnp.concatenate([jnp.where(low, qv, zero), jnp.where(low, zero, qv)], axis=0))
        ss = [jnp.where(masks[b], lax.dot_general(qq, kwin, (((1,), (1,)), ((), ())), preferred_element_type=F32),
                        NEG_BIG) for qq in qqs]
        ms = [jnp.maximum(jnp.max(ss[j], axis=-1, keepdims=True), sinks[j]) for j in range(npair)]
        pvs = [jnp.dot(jnp.exp2(ss[j] - ms[j]).astype(BF16), vwin, preferred_element_type=F32) for j in range(npair)]
        for j in range(npair):
            pv = pvs[j][:, 0:LANES] / (pvs[j][:, LANES:2 * LANES] + jnp.exp2(sinks[j] - ms[j]))
            o = jnp.where(low, pv[:BLOCK], pv[BLOCK:])
            o_ref[0, rsl, csl[j]] = (o * gb_ref[0, rsl, csl[j]].astype(F32)).astype(BF16)


def _swa(sink, q, k, v, gb, nblk=8):
    bsz, seq, _ = q.shape
    rows = nblk * BLOCK
    nb = seq // BLOCK
    cur = lambda b, n: (b, n, 0)
    prv = lambda b, n: (b, jnp.maximum(n * nblk - 1, 0), 0)
    nxt = lambda b, n: (b, jnp.minimum((n + 1) * nblk, nb - 1), 0)
    edge = lambda im: pl.BlockSpec((1, BLOCK, LANES), im)
    mid = pl.BlockSpec((1, rows, LANES), cur)
    wide = pl.BlockSpec((1, rows, HALF), cur)
    return pl.pallas_call(
        functools.partial(_swa_kernel, nblk=nblk),
        grid=(bsz, seq // rows),
        in_specs=[pl.BlockSpec(memory_space=pltpu.SMEM), wide,
                  edge(prv), mid, edge(nxt), edge(prv), mid, edge(nxt), wide],
        out_specs=wide,
        out_shape=jax.ShapeDtypeStruct((bsz, seq, HALF), BF16),
        compiler_params=_cparams(("parallel", "parallel")),
        name="swa",
    )(sink, q, k, k, k, v, v, v, gb)


def _mid_kernel(x_ref, a_ref, o_ref, wo_ref, g_ref, w_ref, cos_ref, sa_ref, sb_ref,
                x1_ref, q_ref, k_ref, v_ref, cg_ref, zg_ref, xbc_ref, dt_ref):
    parts = 2
    rows = x_ref.shape[0] // parts
    groups = [slice(part * rows, (part + 1) * rows) for part in range(parts)]
    hbs = []
    for r in groups:
        x1 = (x_ref[r, :] + jnp.dot(a_ref[r, :], wo_ref[0:HALF, :], preferred_element_type=F32)
              + jnp.dot(o_ref[r, :], wo_ref[HALF:D_MODEL, :], preferred_element_type=F32))
        x1_ref[r, :] = x1
        hbs.append(_rmsnorm(x1, g_ref[...]).astype(BF16))
    ones = jnp.ones((rows, LANES), BF16)
    for r, hb in zip(groups, hbs):
        def mm(a, b):
            return jnp.dot(hb, w_ref[:, a:b], preferred_element_type=F32)

        cos, sa, sb = cos_ref[r, :], sa_ref[r, :], sb_ref[r, :]
        q_ref[r, :] = (_rope_cols(mm(*ODD["q"]), cos, sa, sb) * (HEAD_DIM ** -0.5 * LOG2E)).astype(BF16)
        k_ref[r, :] = _rope_cols(mm(*ODD["k"]), cos, sa, sb).astype(BF16)
        v = mm(*ODD["v"]).astype(BF16)
        for h in range(DIFF_HEADS):
            v_ref[r, 2 * h * LANES:(2 * h + 1) * LANES] = v[:, h * LANES:(h + 1) * LANES]
            v_ref[r, (2 * h + 1) * LANES:(2 * h + 2) * LANES] = ones
        dt_ref[r, :] = mm(*ODD["dt"])
        cg_ref[r, :] = _silu(mm(*ODD["c_gate"])).astype(BF16)
        zg_ref[r, :] = _silu(mm(*ODD["z"])).astype(BF16)
        xbc_ref[r, :] = mm(*ODD["xbc"]).astype(BF16)


def _mid(x2, a2, o2, wo, g, w, cos, sa, sb, seq, tm):
    t = x2.shape[0]
    ns = seq // tm
    row = lambda i: (i, 0)
    pos = lambda i: (i % ns, 0)
    full = lambda i: (0, 0)
    outs = ((D_MODEL, F32), (HALF, BF16), (HALF, BF16), (2 * HALF, BF16), (HALF, BF16), (SSM_INNER, BF16),
            (SSM_CONV_DIM, BF16), (LANES, F32))
    return pl.pallas_call(
        _mid_kernel,
        grid=(t // tm,),
        in_specs=[pl.BlockSpec((tm, D_MODEL), row), pl.BlockSpec((tm, HALF), row), pl.BlockSpec((tm, HALF), row),
                  pl.BlockSpec((D_MODEL, D_MODEL), full), pl.BlockSpec((1, D_MODEL), full),
                  pl.BlockSpec((D_MODEL, IN_ODD_PAD), full),
                  pl.BlockSpec((tm, LANES), pos), pl.BlockSpec((tm, LANES), pos), pl.BlockSpec((tm, LANES), pos)],
        out_specs=[pl.BlockSpec((tm, wd), row) for wd, _ in outs],
        out_shape=[jax.ShapeDtypeStruct((t, wd), dt) for wd, dt in outs],
        compiler_params=_cparams(("parallel",)),
        name="outproj0_inproj1",
    )(x2, a2, o2, wo, g, w, cos, sa, sb)


def _diff_kernel(lq1_ref, lk1_ref, lq2_ref, lk2_ref, g_ref, q_ref, qn_ref, k_ref, v_ref, cg_ref, o_ref,
                 s0_ref, s1_ref, mx0_ref, mx1_ref, m_ref, acc_ref, *, tq, kc, lam_init):
    nsub = q_ref.shape[1] // tq
    nch = k_ref.shape[1] // kc
    lane = lax.broadcasted_iota(jnp.int32, (tq, LANES), 1)

    def stack_maps(q):
        zero = jnp.zeros_like(q)
        return jnp.concatenate([jnp.where(lane < HEAD_DIM, q, zero), jnp.where(lane < HEAD_DIM, zero, q)], axis=0)

    slots = ((s0_ref, mx0_ref), (s1_ref, mx1_ref))

    def scores(qs, c, slot):
        s_ref, mx_ref = slots[slot]
        kch = k_ref[0, pl.ds(pl.multiple_of(c * kc, kc), kc), :]
        s = lax.dot_general(qs, kch, (((1,), (1,)), ((), ())), preferred_element_type=F32)
        s_ref[...] = s
        mx_ref[...] = jnp.broadcast_to(jnp.max(s, axis=-1, keepdims=True), mx_ref.shape)

    def update(c, slot, sub):
        s_ref, mx_ref = slots[slot]
        vch = v_ref[0, pl.ds(pl.multiple_of(c * kc, kc), kc), :]
        m_old = m_ref[sub]
        m_new = jnp.maximum(m_old, mx_ref[...])
        alpha = jnp.exp2(m_old - m_new)
        p = jnp.exp2(s_ref[...] - jnp.concatenate([m_new] * (kc // LANES), axis=1)).astype(BF16)
        acc_ref[sub] = acc_ref[sub] * jnp.concatenate([alpha, alpha], axis=1) + jnp.dot(
            p, vch, preferred_element_type=F32)
        m_ref[sub] = m_new

    @pl.when(pl.program_id(2) == 0)
    def _():
        scores(stack_maps(q_ref[0, 0:tq]), 0, 0)

    qqs = [stack_maps(q_ref[0, sub * tq:(sub + 1) * tq]) for sub in range(nsub)] + [stack_maps(qn_ref[0])]
    lam = (jnp.exp(jnp.sum(lq1_ref[...] * lk1_ref[...], axis=-1, keepdims=True))
           - jnp.exp(jnp.sum(lq2_ref[...] * lk2_ref[...], axis=-1, keepdims=True)) + lam_init)
    for sub in range(nsub):
        rows = slice(sub * tq, (sub + 1) * tq)
        m_ref[sub] = jnp.full(m_ref.shape[1:], NEG_BIG, F32)
        acc_ref[sub] = jnp.zeros(acc_ref.shape[1:], F32)
        for c in range(nch):
            if c + 1 < nch:
                scores(qqs[sub], c + 1, (c + 1) % 2)
            else:
                scores(qqs[sub + 1], 0, 0)
            update(c, c % 2, sub)
        o0 = acc_ref[sub, 0:tq, 0:LANES] / acc_ref[sub, 0:tq, LANES:2 * LANES]
        o1 = acc_ref[sub, tq:2 * tq, 0:LANES] / acc_ref[sub, tq:2 * tq, LANES:2 * LANES]
        o = o0 - lam * o1
        o = o * lax.rsqrt(jnp.mean(o * o, axis=-1, keepdims=True) + EPS) * g_ref[...] * (1.0 - lam_init)
        o_ref[0, rows, :] = (o * cg_ref[0, rows, :].astype(F32)).astype(BF16)


def _diff_attn(lq1, lk1, lq2, lk2, g, q, k, v, cg, lam_init, tq=512):
    bsz, seq, _ = q.shape
    kc = next(c for c in (1024, 512, 256) if seq % (2 * c) == 0)
    nsub = 2 if seq <= 4096 and seq % (2 * tq) == 0 else 1
    nsteps = seq // (nsub * tq)
    small = pl.BlockSpec((1, HEAD_DIM), lambda b, h, i: (0, 0))
    tile = pl.BlockSpec((1, nsub * tq, LANES), lambda b, h, i: (b, i, h))
    next_tile = pl.BlockSpec((1, tq, LANES), lambda b, h, i: (b, jnp.minimum(i + 1, nsteps - 1) * nsub, h))
    return pl.pallas_call(
        functools.partial(_diff_kernel, tq=tq, kc=kc, lam_init=lam_init),
        grid=(bsz, DIFF_HEADS, nsteps),
        in_specs=[small, small, small, small, pl.BlockSpec((1, LANES), lambda b, h, i: (0, 0)), tile, next_tile,
                  pl.BlockSpec((1, seq, LANES), lambda b, h, i: (b, 0, h)),
                  pl.BlockSpec((1, seq, 2 * LANES), lambda b, h, i: (b, 0, h)), tile],
        out_specs=tile,
        out_shape=jax.ShapeDtypeStruct((bsz, seq, HALF), BF16),
        scratch_shapes=[pltpu.VMEM((2 * tq, kc), F32), pltpu.VMEM((2 * tq, kc), F32),
                        pltpu.VMEM((2 * tq, LANES), F32), pltpu.VMEM((2 * tq, LANES), F32),
                        pltpu.VMEM((nsub, 2 * tq, LANES), F32), pltpu.VMEM((nsub, 2 * tq, 2 * LANES), F32)],
        compiler_params=_cparams(("parallel", "parallel", "arbitrary")),
        name="diff_attn",
    )(lq1, lk1, lq2, lk2, g, q, q, k, v, cg)


def _split3(x):
    hi = x.astype(BF16)
    r = x - hi.astype(F32)
    mid = r.astype(BF16)
    lo = (r - mid.astype(F32)).astype(BF16)
    return hi, mid, lo


def _ssd_fwd_kernel(xc_ref, xp_ref, xn_ref, dt_ref, cw_ref, cb_ref, dtb_ref, a_ref, dsk_ref, y_ref, xconv_ref,
                    carry_ref, xf_ref):
    L = BLOCK
    nchunk = xc_ref.shape[1] // L
    c = pl.program_id(1)
    last = pl.num_programs(1) - 1

    @pl.when(c == 0)
    def _():
        carry_ref[...] = jnp.zeros(carry_ref.shape, F32)

    xp, xn = xp_ref[0], xn_ref[0]
    halo_p = jnp.where(c > 0, xp, jnp.zeros_like(xp))
    halo_n = jnp.where(c < last, xn, jnp.zeros_like(xn))
    r_i = lax.broadcasted_iota(jnp.int32, (L, L + 2 * HALO), 0)
    c_i = lax.broadcasted_iota(jnp.int32, (L, L + 2 * HALO), 1)
    first = HALO - (SSM_CONV - 1) // 2
    shifts = [None if first + w == HALO else jnp.where(c_i == r_i + first + w, 1.0, 0.0).astype(BF16)
              for w in range(SSM_CONV)]
    wd = 2 * LANES
    for nb in range(SSM_CONV_DIM // wd):
        cols = slice(nb * wd, (nb + 1) * wd)
        padded = jnp.concatenate([halo_p[:, cols], xc_ref[0, :, cols], halo_n[:, cols]], axis=0)
        for ci in range(nchunk):
            r0 = ci * L
            window = padded[r0:r0 + L + 2 * HALO]
            acc = jnp.broadcast_to(cb_ref[:, cols], (L, wd))
            for w in range(SSM_CONV):
                tap = (xc_ref[0, r0:r0 + L, cols].astype(F32) if shifts[w] is None
                       else jnp.dot(shifts[w], window, preferred_element_type=F32))
                acc = acc + tap * cw_ref[w:w + 1, cols]
            xbc = _silu(acc)
            xconv_ref[0, r0:r0 + L, cols] = xbc.astype(BF16)
            xf_ref[r0:r0 + L, cols] = xbc

    def xcols(k, a, b, dtype):
        return xf_ref[k * L:(k + 1) * L, a:b].astype(dtype)

    def emit(k, j, y):
        cols = slice(j * LANES, (j + 1) * LANES)
        y_ref[0, k * L:(k + 1) * L, cols] = y + xcols(k, j * LANES, (j + 1) * LANES, F32) * dsk_ref[:, cols]

    _ssd_scan_block(xcols, list(range(nchunk)), dt_ref, dtb_ref, a_ref, carry_ref, emit, reverse=False, col0=0)


def _ssd_bwd_kernel(xconv_ref, dt_ref, dtb_ref, a_ref, yf_ref, zg_ref, sg_ref, x1_ref, c_ref, wo_ref, fg_ref, o_ref,
                    carry_ref, yb_ref):
    L = BLOCK
    nchunk = xconv_ref.shape[1] // L

    @pl.when(pl.program_id(1) == 0)
    def _():
        carry_ref[...] = jnp.zeros(carry_ref.shape, F32)

    x2 = x1_ref[0] + jnp.dot(c_ref[0], wo_ref[0:HALF, :], preferred_element_type=F32)

    def xcols(k, a, b, dtype):
        return xconv_ref[0, k * L:(k + 1) * L, a:b].astype(dtype)

    def emit(k, j, y):
        yb_ref[k * L:(k + 1) * L, j * LANES:(j + 1) * LANES] = y

    _ssd_scan_block(xcols, list(reversed(range(nchunk))), dt_ref, dtb_ref, a_ref, carry_ref, emit,
                    reverse=True, col0=SSM_HEADS)
    y = (yf_ref[0] + yb_ref[...]) * zg_ref[0].astype(F32)
    d = _rmsnorm(y, sg_ref[...]).astype(BF16)
    x2 = x2 + jnp.dot(d, wo_ref[HALF:D_MODEL, :], preferred_element_type=F32)
    o_ref[0] = _rmsnorm(x2, fg_ref[...])


def _ssd_scan_block(xcols, order, dt_ref, dtb_ref, a_ref, carry_ref, emit, *, reverse, col0):
    L = BLOCK
    nchunk = len(order)
    chunk = lambda m, k: m[k * L:(k + 1) * L]

    z = dt_ref[0] + dtb_ref[...]
    dt = jnp.maximum(z, 0.0) + jnp.log(1.0 + jnp.exp(-jnp.abs(z)))
    a = dt * a_ref[...]
    r_i = lax.broadcasted_iota(jnp.int32, (L, L), 0)
    c_i = lax.broadcasted_iota(jnp.int32, (L, L), 1)
    causal = (r_i <= c_i) if reverse else (r_i >= c_i)
    tri = jnp.where(causal, 1.0, 0.0).astype(BF16)
    pieces = _split3(a * LOG2E)
    acum = [sum(jnp.dot(tri, chunk(pc, k), preferred_element_type=F32) for pc in pieces) for k in range(nchunk)]
    acum_t = [m.T for m in acum]
    edge = 0 if reverse else L - 1
    a_end = [m[edge:edge + 1, :] for m in acum]

    lane = lax.broadcasted_iota(jnp.int32, (L, LANES), 1)
    low = lane < HEAD_DIM
    lane1 = lax.broadcasted_iota(jnp.int32, (1, LANES), 1)

    cbs, bts, cms = [], [], []
    for k in range(nchunk):
        bm = [xcols(k, SSM_INNER + g * SSM_STATE, SSM_INNER + (g + 1) * SSM_STATE, BF16) for g in range(2)]
        cm = [xcols(k, SSM_INNER + (2 + g) * SSM_STATE, SSM_INNER + (3 + g) * SSM_STATE, BF16) for g in range(2)]
        cbs.append([lax.dot_general(cm[g], bm[g], (((1,), (1,)), ((), ())), preferred_element_type=F32)
                    for g in range(2)])
        bts.append([bm[g].T for g in range(2)])
        cms.append(cm)
    for j in range(SSM_HEADS // 2):
        g = j // 2
        ca = col0 + 2 * j
        ydiag, states, decay_out, decay_in = [], [], [], []
        for k in range(nchunk):
            acb = [jnp.broadcast_to(acum[k][:, ca + hh:ca + hh + 1], (L, LANES)) for hh in range(2)]
            ac_pair = jnp.where(low, acb[0], acb[1])
            dtk = chunk(dt, k)
            dt_pair = jnp.where(low, dtk[:, ca:ca + 1], dtk[:, ca + 1:ca + 2])
            end_pair = jnp.where(lane1 < HEAD_DIM, a_end[k][:, ca:ca + 1], a_end[k][:, ca + 1:ca + 2])
            xdt = xcols(k, j * LANES, (j + 1) * LANES, F32) * dt_pair
            xdt_b = xdt.astype(BF16)
            zero = jnp.zeros_like(xdt_b)
            ms = []
            for hh in range(2):
                seg = acb[hh] - acum_t[k][ca + hh:ca + hh + 1, :]
                dec = jnp.exp2(jnp.where(causal, seg, NEG_BIG))
                ms.append((cbs[k][g] * dec).astype(BF16))
            lhs = jnp.concatenate(ms, axis=1)
            rhs = jnp.concatenate([jnp.where(low, xdt_b, zero), jnp.where(low, zero, xdt_b)], axis=0)
            ydiag.append(jnp.dot(lhs, rhs, preferred_element_type=F32))
            states.append(jnp.dot(bts[k][g], (xdt * jnp.exp2(end_pair - ac_pair)).astype(BF16),
                                  preferred_element_type=F32))
            decay_out.append(jnp.exp2(end_pair))
            decay_in.append(jnp.exp2(ac_pair))
        carry = carry_ref[j]
        for k in order:
            emit(k, j, ydiag[k] + jnp.dot(cms[k][g], carry.astype(BF16), preferred_element_type=F32) * decay_in[k])
            carry = carry * decay_out[k] + states[k]
        carry_ref[j] = carry


_SSD_CARRY = pltpu.VMEM((SSM_HEADS // 2, SSM_STATE, LANES), F32)
_SSD_ROWS = 8 * BLOCK


def _ssd_fwd(xbc, dt, cw, cb, dtb, a, dsk):
    bsz, seq, _ = xbc.shape
    vec = lambda wd: pl.BlockSpec((1, wd), lambda b, c: (0, 0))
    chunk = lambda wd: pl.BlockSpec((1, _SSD_ROWS, wd), lambda b, c: (b, c, 0))
    return pl.pallas_call(
        _ssd_fwd_kernel,
        grid=(bsz, seq // _SSD_ROWS),
        in_specs=_halo_specs(_SSD_ROWS, seq, SSM_CONV_DIM) + [
            chunk(LANES), pl.BlockSpec((SSM_CONV, SSM_CONV_DIM), lambda b, c: (0, 0)), vec(SSM_CONV_DIM),
            vec(LANES), vec(LANES), vec(SSM_INNER)],
        out_specs=[chunk(SSM_INNER), chunk(SSM_CONV_DIM)],
        out_shape=[jax.ShapeDtypeStruct((bsz, seq, SSM_INNER), F32),
                   jax.ShapeDtypeStruct((bsz, seq, SSM_CONV_DIM), BF16)],
        scratch_shapes=[_SSD_CARRY, pltpu.VMEM((_SSD_ROWS, SSM_CONV_DIM), F32)],
        compiler_params=_cparams(("parallel", "arbitrary")),
        name="ssd_fwd",
    )(xbc, xbc, xbc, dt, cw, cb, dtb, a, dsk)


def _ssd_bwd_final(xconv, dt, dtb, a, yf, zg, sg, x1, c, wo, fg):
    bsz, seq, _ = xconv.shape
    nc = seq // _SSD_ROWS
    vec = lambda wd: pl.BlockSpec((1, wd), lambda b, c: (0, 0))
    chunk = lambda wd: pl.BlockSpec((1, _SSD_ROWS, wd), lambda b, c: (b, nc - 1 - c, 0))
    return pl.pallas_call(
        _ssd_bwd_kernel,
        grid=(bsz, nc),
        in_specs=[chunk(SSM_CONV_DIM), chunk(LANES), vec(LANES), vec(LANES), chunk(SSM_INNER), chunk(SSM_INNER),
                  vec(SSM_INNER), chunk(D_MODEL), chunk(HALF), pl.BlockSpec((D_MODEL, D_MODEL), lambda b, c: (0, 0)),
                  vec(D_MODEL)],
        out_specs=chunk(D_MODEL),
        out_shape=jax.ShapeDtypeStruct((bsz, seq, D_MODEL), F32),
        scratch_shapes=[_SSD_CARRY, pltpu.VMEM((_SSD_ROWS, SSM_INNER), F32)],
        compiler_params=_cparams(("parallel", "arbitrary")),
        name="ssd_bwd_final",
    )(xconv, dt, dtb, a, yf, zg, sg, x1, c, wo, fg)


def _rope_tables(seq):
    inv = 1.0 / (ROPE_THETA ** (jnp.arange(0, HEAD_DIM, 2, dtype=F32) / HEAD_DIM))
    f = jnp.arange(seq, dtype=F32)[:, None] * inv[None, :]
    cos, sin = jnp.cos(f), jnp.sin(f)
    cos = jnp.concatenate([cos] * 4, axis=-1)
    sin = jnp.concatenate([sin] * 4, axis=-1)
    first_half = (jnp.arange(LANES) % HEAD_DIM) < HEAD_DIM // 2
    return cos, jnp.where(first_half, -sin, 0.0), jnp.where(first_half, 0.0, sin)


def _pair_perm():
    idx = []
    for j in range(SWA_HEADS // 2):
        idx += list(range(j * HEAD_DIM, (j + 1) * HEAD_DIM))
        idx += list(range((j + 4) * HEAD_DIM, (j + 5) * HEAD_DIM))
    return jnp.asarray(idx, jnp.int32)


def _pad_lanes(v, offset):
    return jnp.zeros((1, LANES), F32).at[0, offset:offset + v.shape[0]].set(v.astype(F32))


def _prepare(norm_g, w_in0, conv_w, conv_b, conv_ln_g, conv_ln_b, sink, w_out0, w_in1, lambda_q1, lambda_k1,
             lambda_q2, lambda_k2, diff_norm_g, ssm_conv_w, ssm_conv_b, dt_bias_f, dt_bias_b, a_log_f, a_log_b,
             d_skip, ssm_norm_g, w_out1, final_norm_g):
    perm = _pair_perm()
    w0 = w_in0[0].astype(BF16)
    (q0, q1), (g0, g1) = EVEN["q"], EVEN["b_gate"]
    w0 = jnp.concatenate([w0[:, :q0], w0[:, q0:q1][:, perm], w0[:, q1:g0], w0[:, g0:g1][:, perm]], axis=1)
    wo0 = w_out0[0].astype(BF16)
    wo0 = jnp.concatenate([wo0[:HALF], wo0[HALF:][perm]], axis=0)
    w1 = jnp.pad(w_in1[0].astype(BF16), ((0, 0), (0, IN_ODD_PAD - IN_ODD)))
    row = lambda v: v.astype(F32).reshape(1, -1)
    return dict(
        g0=row(norm_g[0]), g1=row(norm_g[1]), w0=w0, wo0=wo0, w1=w1,
        wo1=w_out1[0].astype(BF16), cb=row(conv_b[0]), lg=row(conv_ln_g[0]),
        cw=jnp.broadcast_to(conv_w[0].astype(F32)[:, None, :], (CONV_WIDTH, SUBLANES, CONV_CH)),
        lb=row(conv_ln_b[0]), sink=sink[0].astype(F32),
        lq1=row(lambda_q1[0]), lk1=row(lambda_k1[0]), lq2=row(lambda_q2[0]), lk2=row(lambda_k2[0]),
        dg=row(diff_norm_g[0]), scw=ssm_conv_w[0].astype(F32), scb=row(ssm_conv_b[0]),
        dtb_f=_pad_lanes(dt_bias_f[0], 0), dtb_b=_pad_lanes(dt_bias_b[0], SSM_HEADS),
        a_f=_pad_lanes(-jnp.exp(a_log_f[0].astype(F32)), 0), a_b=_pad_lanes(-jnp.exp(a_log_b[0].astype(F32)), SSM_HEADS),
        dsk=row(jnp.repeat(d_skip[0].astype(F32), HEAD_DIM)), sg=row(ssm_norm_g[0]), fg=row(final_norm_g))


def _trunk(x, p, tm0=1024, tm=512):
    bsz, seq, _ = x.shape
    assert seq % tm0 == 0 and seq % tm == 0 and seq % _SSD_ROWS == 0
    t = bsz * seq
    lam_init = 0.8 - 0.6 * math.exp(-0.3 * 1)
    cos, sa, sb = _rope_tables(seq)
    x2 = x.reshape(t, D_MODEL)
    apre, ga, q, k, v, gb = _inproj0(x2, p["g0"], p["w0"], cos, sa, sb, seq, tm0)
    r3 = lambda arr: arr.reshape(bsz, seq, arr.shape[-1])
    a = _convmod(r3(apre), r3(ga), p["cw"], p["cb"], p["lg"], p["lb"])
    o = _swa(p["sink"], r3(q), r3(k), r3(v), r3(gb))
    x1, q1, k1, v1, cg, zg, xbc, dt = _mid(x2, a.reshape(t, HALF), o.reshape(t, HALF), p["wo0"], p["g1"], p["w1"],
                                           cos, sa, sb, seq, tm)
    c = _diff_attn(p["lq1"], p["lk1"], p["lq2"], p["lk2"], p["dg"], r3(q1), r3(k1), r3(v1), r3(cg), lam_init)
    yf, xconv = _ssd_fwd(r3(xbc), r3(dt), p["scw"], p["scb"], p["dtb_f"], p["a_f"], p["dsk"])
    return _ssd_bwd_final(xconv, r3(dt), p["dtb_b"], p["a_b"], yf, r3(zg), p["sg"], r3(x1), c, p["wo1"], p["fg"])


def kernel(x_prompt, x_sample, norm_g, w_in0, conv_w, conv_b, conv_ln_g, conv_ln_b, sink, w_out0, w_in1, lambda_q1, lambda_k1, lambda_q2, lambda_k2, diff_norm_g, ssm_conv_w, ssm_conv_b, dt_bias_f, dt_bias_b, a_log_f, a_log_b, d_skip, ssm_norm_g, w_out1, final_norm_g):
    p = _prepare(norm_g, w_in0, conv_w, conv_b, conv_ln_g, conv_ln_b, sink, w_out0, w_in1, lambda_q1, lambda_k1,
                 lambda_q2, lambda_k2, diff_norm_g, ssm_conv_w, ssm_conv_b, dt_bias_f, dt_bias_b, a_log_f, a_log_b,
                 d_skip, ssm_norm_g, w_out1, final_norm_g)
    return (_trunk(x_prompt, p), _trunk(x_sample, p))
```

```python
import functools
import math

import jax
import jax.numpy as jnp
from jax import lax
from jax.experimental import pallas as pl
from jax.experimental.pallas import tpu as pltpu

F32 = jnp.float32
BF16 = jnp.bfloat16

D_MODEL = 1024
EPS = 1e-6
ROPE_THETA = 10000.0
HEAD_DIM = 64
LANES = 128
SUBLANES = 8
HALO = 16
CONV_CH = 512
CONV_WIDTH = 31
SWA_HEADS = 8
BLOCK = 128
DIFF_HEADS = 4
SSM_INNER = 512
SSM_HEADS = 8
SSM_STATE = 128
SSM_CONV = 5
SSM_CONV_DIM = 1024
HALF = D_MODEL // 2


def _col_ranges(**widths):
    out, start = {}, 0
    for name, width in widths.items():
        out[name] = (start, start + width)
        start += width
    return out


EVEN = _col_ranges(a_val=CONV_CH, a_glu=CONV_CH, a_gate=CONV_CH, q=HALF, k=LANES, v=LANES, b_gate=HALF)
ODD = _col_ranges(q=HALF, k=HALF, v=HALF, c_gate=HALF, z=SSM_INNER, xbc=SSM_CONV_DIM, dt=LANES)
IN_EVEN = EVEN["b_gate"][1]
IN_ODD = ODD["dt"][0] + 2 * SSM_HEADS
IN_ODD_PAD = ODD["dt"][1]
NEG_BIG = -1e30
LOG2E = math.log2(math.e)
VMEM_LIMIT = 56 * 1024 * 1024


def _cparams(sem):
    return pltpu.CompilerParams(dimension_semantics=sem, vmem_limit_bytes=VMEM_LIMIT)


def _sigmoid(x):
    return 1.0 / (1.0 + jnp.exp2(x * (-LOG2E)))


def _silu(x):
    return x * _sigmoid(x)


def _rmsnorm(x, g):
    return x * lax.rsqrt(jnp.mean(x * x, axis=-1, keepdims=True) + EPS) * g


def _rope_cols(x, cos, sa, sb):
    outs = []
    for j in range(x.shape[1] // LANES):
        xj = x[:, j * LANES:(j + 1) * LANES]
        outs.append(xj * cos + pltpu.roll(xj, LANES - 32, 1) * sa + pltpu.roll(xj, 32, 1) * sb)
    return outs[0] if len(outs) == 1 else jnp.concatenate(outs, axis=1)


def _inproj0_kernel(x_ref, g_ref, w_ref, cos_ref, sa_ref, sb_ref,
                    apre_ref, ga_ref, q_ref, k_ref, v_ref, gb_ref):
    parts = 2
    rows = x_ref.shape[0] // parts
    groups = [slice(part * rows, (part + 1) * rows) for part in range(parts)]
    hbs = [_rmsnorm(x_ref[r, :], g_ref[...]).astype(BF16) for r in groups]
    for r, hb in zip(groups, hbs):
        def mm(a, b):
            return jnp.dot(hb, w_ref[:, a:b], preferred_element_type=F32)

        cos, sa, sb = cos_ref[r, :], sa_ref[r, :], sb_ref[r, :]
        apre_ref[r, :] = (mm(*EVEN["a_val"]) * _sigmoid(mm(*EVEN["a_glu"]))).astype(BF16)
        ga_ref[r, :] = _silu(mm(*EVEN["a_gate"])).astype(BF16)
        gb_ref[r, :] = _silu(mm(*EVEN["b_gate"])).astype(BF16)
        q_ref[r, :] = (_rope_cols(mm(*EVEN["q"]), cos, sa, sb) * (HEAD_DIM ** -0.5 * LOG2E)).astype(BF16)
        kv = mm(EVEN["k"][0], EVEN["v"][1])
        k_ref[r, :] = _rope_cols(kv[:, 0:LANES], cos, sa, sb).astype(BF16)
        v_ref[r, :] = kv[:, LANES:2 * LANES].astype(BF16)


def _inproj0(x2, g, w, cos, sa, sb, seq, tm):
    t = x2.shape[0]
    ns = seq // tm
    row = lambda i: (i, 0)
    pos = lambda i: (i % ns, 0)
    full = lambda i: (0, 0)
    widths = (CONV_CH, CONV_CH, HALF, LANES, LANES, HALF)
    return pl.pallas_call(
        _inproj0_kernel,
        grid=(t // tm,),
        in_specs=[pl.BlockSpec((tm, D_MODEL), row), pl.BlockSpec((1, D_MODEL), full),
                  pl.BlockSpec((D_MODEL, IN_EVEN), full),
                  pl.BlockSpec((tm, LANES), pos), pl.BlockSpec((tm, LANES), pos), pl.BlockSpec((tm, LANES), pos)],
        out_specs=[pl.BlockSpec((tm, wd), row) for wd in widths],
        out_shape=[jax.ShapeDtypeStruct((t, wd), BF16) for wd in widths],
        compiler_params=_cparams(("parallel",)),
        name="inproj0",
    )(x2, g, w, cos, sa, sb)


def _convmod_kernel(xc_ref, xp_ref, xn_ref, ga_ref, cw_ref, cb_ref, lg_ref, lb_ref, o_ref, rol_ref, *, ts, rs):
    i = pl.program_id(1)
    last = pl.num_programs(1) - 1
    xp, xn = xp_ref[0], xn_ref[0]
    padded = jnp.concatenate([jnp.where(i > 0, xp, jnp.zeros_like(xp)), xc_ref[0],
                              jnp.where(i < last, xn, jnp.zeros_like(xn))], axis=0)
    rows = ts + 2 * HALO
    rol_ref[0] = padded.astype(F32)
    for rb in range(0, rows, BLOCK):
        nr = min(BLOCK, rows - rb)
        nk = min(BLOCK + HALO, rows - rb)
        r_i = lax.broadcasted_iota(jnp.int32, (nr, nk), 0)
        c_i = lax.broadcasted_iota(jnp.int32, (nr, nk), 1)
        window = padded[rb:rb + nk]
        for j in range(1, SUBLANES):
            shift = jnp.where(c_i == r_i + j, 1.0, 0.0).astype(BF16)
            rol_ref[j, rb:rb + nr] = jnp.dot(shift, window, preferred_element_type=F32)
    first = HALO - (CONV_WIDTH - 1) // 2

    for r0 in range(0, ts, rs):
        acc = jnp.broadcast_to(cb_ref[...], (rs, CONV_CH))
        for w in range(CONV_WIDTH):
            a, j = divmod(first + w, SUBLANES)
            lo = r0 + a * SUBLANES
            acc = acc + rol_ref[j, lo:lo + rs, :] * jnp.concatenate([cw_ref[w]] * (rs // SUBLANES), axis=0)
        mu = jnp.mean(acc, axis=-1, keepdims=True)
        xc = acc - mu
        var = jnp.mean(xc * xc, axis=-1, keepdims=True)
        y = xc * lax.rsqrt(var + EPS) * lg_ref[...] + lb_ref[...]
        o_ref[0, r0:r0 + rs, :] = (_silu(y) * ga_ref[0, r0:r0 + rs, :].astype(F32)).astype(BF16)


def _halo_specs(ts, seq, width):
    hb = ts // HALO
    nh = seq // HALO
    return [pl.BlockSpec((1, ts, width), lambda b, i: (b, i, 0)),
            pl.BlockSpec((1, HALO, width), lambda b, i: (b, jnp.maximum(i * hb - 1, 0), 0)),
            pl.BlockSpec((1, HALO, width), lambda b, i: (b, jnp.minimum((i + 1) * hb, nh - 1), 0))]


def _convmod(apre, ga, cw, cb, lg, lb, ts=1024, rs=32):
    bsz, seq, _ = apre.shape
    vec = pl.BlockSpec((1, CONV_CH), lambda b, i: (0, 0))
    return pl.pallas_call(
        functools.partial(_convmod_kernel, ts=ts, rs=rs),
        grid=(bsz, seq // ts),
        in_specs=_halo_specs(ts, seq, CONV_CH) + [
            pl.BlockSpec((1, ts, CONV_CH), lambda b, i: (b, i, 0)),
            pl.BlockSpec((CONV_WIDTH, SUBLANES, CONV_CH), lambda b, i: (0, 0, 0)), vec, vec, vec],
        out_specs=pl.BlockSpec((1, ts, CONV_CH), lambda b, i: (b, i, 0)),
        out_shape=jax.ShapeDtypeStruct((bsz, seq, CONV_CH), BF16),
        scratch_shapes=[pltpu.VMEM((SUBLANES, ts + 2 * HALO, CONV_CH), F32)],
        compiler_params=_cparams(("parallel", "parallel")),
        name="convmod",
    )(apre, apre, apre, ga, cw, cb, lg, lb)


def _swa_kernel(sink_ref, q_ref, kp_ref, kc_ref, kn_ref, vp_ref, vc_ref, vn_ref, gb_ref, o_ref, *, nblk):
    n = pl.program_id(1)
    last = pl.num_programs(1) - 1
    kcat = jnp.concatenate([kp_ref[0], kc_ref[0], kn_ref[0]], axis=0)
    vcat = jnp.concatenate([vp_ref[0], vc_ref[0], vn_ref[0]], axis=0)
    vaug = jnp.concatenate([vcat, jnp.ones_like(vcat)], axis=1)
    rows = lax.broadcasted_iota(jnp.int32, (2 * BLOCK, 3 * BLOCK), 0)
    qi = jnp.where(rows >= BLOCK, rows - BLOCK, rows)
    ci = lax.broadcasted_iota(jnp.int32, (2 * BLOCK, 3 * BLOCK), 1)
    rel = ci - BLOCK - qi
    band = (rel >= -BLOCK) & (rel <= BLOCK)
    first_key = jnp.where(n > 0, 0, BLOCK)
    end_key = jnp.where(n < last, 3 * BLOCK, 2 * BLOCK)
    lane = lax.broadcasted_iota(jnp.int32, (BLOCK, LANES), 1)
    low = lane < HEAD_DIM
    row1 = lax.broadcasted_iota(jnp.int32, (2 * BLOCK, 1), 0)
    npair = SWA_HEADS // 2
    sinks = [jnp.where(row1 < BLOCK, sink_ref[j], sink_ref[j + npair]) * LOG2E for j in range(npair)]
    masks = []
    for b in range(nblk):
        mask = band
        if b == 0:
            mask = mask & (ci >= first_key)
        if b == nblk - 1:
            mask = mask & (ci < end_key)
        masks.append(mask)
    csl = [slice(j * LANES, (j + 1) * LANES) for j in range(npair)]
    for b in range(nblk):
        rsl = slice(b * BLOCK, (b + 1) * BLOCK)
        kwin = kcat[b * BLOCK:(b + 3) * BLOCK]
        vwin = vaug[b * BLOCK:(b + 3) * BLOCK]
        qqs = []
        for j in range(npair):
            qv = q_ref[0, rsl, csl[j]]
            zero = jnp.zeros_like(qv)
            qqs.append(jnp.concatenate([jnp.where(low, qv, zero), jnp.where(low, zero, qv)], axis=0))
        ss = [jnp.where(masks[b], lax.dot_general(qq, kwin, (((1,), (1,)), ((), ())), preferred_element_type=F32),
                        NEG_BIG) for qq in qqs]
        ms = [jnp.maximum(jnp.max(ss[j], axis=-1, keepdims=True), sinks[j]) for j in range(npair)]
        pvs = [jnp.dot(jnp.exp2(ss[j] - ms[j]).astype(BF16), vwin, preferred_element_type=F32) for j in range(npair)]
        for j in range(npair):
            pv = pvs[j][:, 0:LANES] / (pvs[j][:, LANES:2 * LANES] + jnp.exp2(sinks[j] - ms[j]))
            o = jnp.where(low, pv[:BLOCK], pv[BLOCK:])
            o_ref[0, rsl, csl[j]] = (o * gb_ref[0, rsl, csl[j]].astype(F32)).astype(BF16)


def _swa(sink, q, k, v, gb, nblk=8):
    bsz, seq, _ = q.shape
    rows = nblk * BLOCK
    nb = seq // BLOCK
    cur = lambda b, n: (b, n, 0)
    prv = lambda b, n: (b, jnp.maximum(n * nblk - 1, 0), 0)
    nxt = lambda b, n: (b, jnp.minimum((n + 1) * nblk, nb - 1), 0)
    edge = lambda im: pl.BlockSpec((1, BLOCK, LANES), im)
    mid = pl.BlockSpec((1, rows, LANES), cur)
    wide = pl.BlockSpec((1, rows, HALF), cur)
    return pl.pallas_call(
        functools.partial(_swa_kernel, nblk=nblk),
        grid=(bsz, seq // rows),
        in_specs=[pl.BlockSpec(memory_space=pltpu.SMEM), wide,
                  edge(prv), mid, edge(nxt), edge(prv), mid, edge(nxt), wide],
        out_specs=wide,
        out_shape=jax.ShapeDtypeStruct((bsz, seq, HALF), BF16),
        compiler_params=_cparams(("parallel", "parallel")),
        name="swa",
    )(sink, q, k, k, k, v, v, v, gb)


def _mid_kernel(x_ref, a_ref, o_ref, wo_ref, g_ref, w_ref, cos_ref, sa_ref, sb_ref,
                x1_ref, q_ref, k_ref, v_ref, cg_ref, zg_ref, xbc_ref, dt_ref):
    parts = 2
    rows = x_ref.shape[0] // parts
    groups = [slice(part * rows, (part + 1) * rows) for part in range(parts)]
    hbs = []
    for r in groups:
        x1 = (x_ref[r, :] + jnp.dot(a_ref[r, :], wo_ref[0:HALF, :], preferred_element_type=F32)
              + jnp.dot(o_ref[r, :], wo_ref[HALF:D_MODEL, :], preferred_element_type=F32))
        x1_ref[r, :] = x1
        hbs.append(_rmsnorm(x1, g_ref[...]).astype(BF16))
    ones = jnp.ones((rows, LANES), BF16)
    for r, hb in zip(groups, hbs):
        def mm(a, b):
            return jnp.dot(hb, w_ref[:, a:b], preferred_element_type=F32)

        cos, sa, sb = cos_ref[r, :], sa_ref[r, :], sb_ref[r, :]
        q_ref[r, :] = (_rope_cols(mm(*ODD["q"]), cos, sa, sb) * (HEAD_DIM ** -0.5 * LOG2E)).astype(BF16)
        k_ref[r, :] = _rope_cols(mm(*ODD["k"]), cos, sa, sb).astype(BF16)
        v = mm(*ODD["v"]).astype(BF16)
        for h in range(DIFF_HEADS):
            v_ref[r, 2 * h * LANES:(2 * h + 1) * LANES] = v[:, h * LANES:(h + 1) * LANES]
            v_ref[r, (2 * h + 1) * LANES:(2 * h + 2) * LANES] = ones
        dt_ref[r, :] = mm(*ODD["dt"])
        cg_ref[r, :] = _silu(mm(*ODD["c_gate"])).astype(BF16)
        zg_ref[r, :] = _silu(mm(*ODD["z"])).astype(BF16)
        xbc_ref[r, :] = mm(*ODD["xbc"]).astype(BF16)


def _mid(x2, a2, o2, wo, g, w, cos, sa, sb, seq, tm):
    t = x2.shape[0]
    ns = seq // tm
    row = lambda i: (i, 0)
    pos = lambda i: (i % ns, 0)
    full = lambda i: (0, 0)
    outs = ((D_MODEL, F32), (HALF, BF16), (HALF, BF16), (2 * HALF, BF16), (HALF, BF16), (SSM_INNER, BF16),
            (SSM_CONV_DIM, BF16), (LANES, F32))
    return pl.pallas_call(
        _mid_kernel,
        grid=(t // tm,),
        in_specs=[pl.BlockSpec((tm, D_MODEL), row), pl.BlockSpec((tm, HALF), row), pl.BlockSpec((tm, HALF), row),
                  pl.BlockSpec((D_MODEL, D_MODEL), full), pl.BlockSpec((1, D_MODEL), full),
                  pl.BlockSpec((D_MODEL, IN_ODD_PAD), full),
                  pl.BlockSpec((tm, LANES), pos), pl.BlockSpec((tm, LANES), pos), pl.BlockSpec((tm, LANES), pos)],
        out_specs=[pl.BlockSpec((tm, wd), row) for wd, _ in outs],
        out_shape=[jax.ShapeDtypeStruct((t, wd), dt) for wd, dt in outs],
        compiler_params=_cparams(("parallel",)),
        name="outproj0_inproj1",
    )(x2, a2, o2, wo, g, w, cos, sa, sb)


def _diff_kernel(lq1_ref, lk1_ref, lq2_ref, lk2_ref, g_ref, q_ref, qn_ref, k_ref, v_ref, cg_ref, o_ref,
                 s0_ref, s1_ref, mx0_ref, mx1_ref, m_ref, acc_ref, *, tq, kc, lam_init):
    nsub = q_ref.shape[1] // tq
    nch = k_ref.shape[1] // kc
    lane = lax.broadcasted_iota(jnp.int32, (tq, LANES), 1)

    def stack_maps(q):
        zero = jnp.zeros_like(q)
        return jnp.concatenate([jnp.where(lane < HEAD_DIM, q, zero), jnp.where(lane < HEAD_DIM, zero, q)], axis=0)

    slots = ((s0_ref, mx0_ref), (s1_ref, mx1_ref))

    def scores(qs, c, slot):
        s_ref, mx_ref = slots[slot]
        kch = k_ref[0, pl.ds(pl.multiple_of(c * kc, kc), kc), :]
        s = lax.dot_general(qs, kch, (((1,), (1,)), ((), ())), preferred_element_type=F32)
        s_ref[...] = s
        mx_ref[...] = jnp.broadcast_to(jnp.max(s, axis=-1, keepdims=True), mx_ref.shape)

    def update(c, slot, sub):
        s_ref, mx_ref = slots[slot]
        vch = v_ref[0, pl.ds(pl.multiple_of(c * kc, kc), kc), :]
        m_old = m_ref[sub]
        m_new = jnp.maximum(m_old, mx_ref[...])
        alpha = jnp.exp2(m_old - m_new)
        p = jnp.exp2(s_ref[...] - jnp.concatenate([m_new] * (kc // LANES), axis=1)).astype(BF16)
        acc_ref[sub] = acc_ref[sub] * jnp.concatenate([alpha, alpha], axis=1) + jnp.dot(
            p, vch, preferred_element_type=F32)
        m_ref[sub] = m_new

    @pl.when(pl.program_id(2) == 0)
    def _():
        scores(stack_maps(q_ref[0, 0:tq]), 0, 0)

    qqs = [stack_maps(q_ref[0, sub * tq:(sub + 1) * tq]) for sub in range(nsub)] + [stack_maps(qn_ref[0])]
    lam = (jnp.exp(jnp.sum(lq1_ref[...] * lk1_ref[...], axis=-1, keepdims=True))
           - jnp.exp(jnp.sum(lq2_ref[...] * lk2_ref[...], axis=-1, keepdims=True)) + lam_init)
    for sub in range(nsub):
        rows = slice(sub * tq, (sub + 1) * tq)
        m_ref[sub] = jnp.full(m_ref.shape[1:], NEG_BIG, F32)
        acc_ref[sub] = jnp.zeros(acc_ref.shape[1:], F32)
        for c in range(nch):
            if c + 1 < nch:
                scores(qqs[sub], c + 1, (c + 1) % 2)
            else:
                scores(qqs[sub + 1], 0, 0)
            update(c, c % 2, sub)
        o0 = acc_ref[sub, 0:tq, 0:LANES] / acc_ref[sub, 0:tq, LANES:2 * LANES]
        o1 = acc_ref[sub, tq:2 * tq, 0:LANES] / acc_ref[sub, tq:2 * tq, LANES:2 * LANES]
        o = o0 - lam * o1
        o = o * lax.rsqrt(jnp.mean(o * o, axis=-1, keepdims=True) + EPS) * g_ref[...] * (1.0 - lam_init)
        o_ref[0, rows, :] = (o * cg_ref[0, rows, :].astype(F32)).astype(BF16)


def _diff_attn(lq1, lk1, lq2, lk2, g, q, k, v, cg, lam_init, tq=512):
    bsz, seq, _ = q.shape
    kc = next(c for c in (1024, 512, 256) if seq % (2 * c) == 0)
    nsub = next(n for n in (16 * kc // seq, 2, 1) if n >= 1 and (seq // tq) % n == 0)
    nsteps = seq // (nsub * tq)
    small = pl.BlockSpec((1, HEAD_DIM), lambda b, h, i: (0, 0))
    tile = pl.BlockSpec((1, nsub * tq, LANES), lambda b, h, i: (b, i, h))
    next_tile = pl.BlockSpec((1, tq, LANES), lambda b, h, i: (b, jnp.minimum(i + 1, nsteps - 1) * nsub, h))
    return pl.pallas_call(
        functools.partial(_diff_kernel, tq=tq, kc=kc, lam_init=lam_init),
        grid=(bsz, DIFF_HEADS, nsteps),
        in_specs=[small, small, small, small, pl.BlockSpec((1, LANES), lambda b, h, i: (0, 0)), tile, next_tile,
                  pl.BlockSpec((1, seq, LANES), lambda b, h, i: (b, 0, h)),
                  pl.BlockSpec((1, seq, 2 * LANES), lambda b, h, i: (b, 0, h)), tile],
        out_specs=tile,
        out_shape=jax.ShapeDtypeStruct((bsz, seq, HALF), BF16),
        scratch_shapes=[pltpu.VMEM((2 * tq, kc), F32), pltpu.VMEM((2 * tq, kc), F32),
                        pltpu.VMEM((2 * tq, LANES), F32), pltpu.VMEM((2 * tq, LANES), F32),
                        pltpu.VMEM((nsub, 2 * tq, LANES), F32), pltpu.VMEM((nsub, 2 * tq, 2 * LANES), F32)],
        compiler_params=_cparams(("parallel", "parallel", "arbitrary")),
        name="diff_attn",
    )(lq1, lk1, lq2, lk2, g, q, q, k, v, cg)


def _split3(x):
    hi = x.astype(BF16)
    r = x - hi.astype(F32)
    mid = r.astype(BF16)
    lo = (r - mid.astype(F32)).astype(BF16)
    return hi, mid, lo


def _ssd_fwd_kernel(xc_ref, xp_ref, xn_ref, dt_ref, cw_ref, cb_ref, dtb_ref, a_ref, dsk_ref, y_ref, xconv_ref,
                    carry_ref, xf_ref):
    L = BLOCK
    nchunk = xc_ref.shape[1] // L
    c = pl.program_id(1)
    last = pl.num_programs(1) - 1

    @pl.when(c == 0)
    def _():
        carry_ref[...] = jnp.zeros(carry_ref.shape, F32)

    xp, xn = xp_ref[0], xn_ref[0]
    halo_p = jnp.where(c > 0, xp, jnp.zeros_like(xp))
    halo_n = jnp.where(c < last, xn, jnp.zeros_like(xn))
    r_i = lax.broadcasted_iota(jnp.int32, (L, L + 2 * HALO), 0)
    c_i = lax.broadcasted_iota(jnp.int32, (L, L + 2 * HALO), 1)
    first = HALO - (SSM_CONV - 1) // 2
    shifts = [None if first + w == HALO else jnp.where(c_i == r_i + first + w, 1.0, 0.0).astype(BF16)
              for w in range(SSM_CONV)]
    wd = 2 * LANES
    for nb in range(SSM_CONV_DIM // wd):
        cols = slice(nb * wd, (nb + 1) * wd)
        padded = jnp.concatenate([halo_p[:, cols], xc_ref[0, :, cols], halo_n[:, cols]], axis=0)
        for ci in range(nchunk):
            r0 = ci * L
            window = padded[r0:r0 + L + 2 * HALO]
            acc = jnp.broadcast_to(cb_ref[:, cols], (L, wd))
            for w in range(SSM_CONV):
                tap = (xc_ref[0, r0:r0 + L, cols].astype(F32) if shifts[w] is None
                       else jnp.dot(shifts[w], window, preferred_element_type=F32))
                acc = acc + tap * cw_ref[w:w + 1, cols]
            xbc = _silu(acc)
            xconv_ref[0, r0:r0 + L, cols] = xbc.astype(BF16)
            xf_ref[r0:r0 + L, cols] = xbc

    def xcols(k, a, b, dtype):
        return xf_ref[k * L:(k + 1) * L, a:b].astype(dtype)

    def emit(k, j, y):
        cols = slice(j * LANES, (j + 1) * LANES)
        y_ref[0, k * L:(k + 1) * L, cols] = y + xcols(k, j * LANES, (j + 1) * LANES, F32) * dsk_ref[:, cols]

    _ssd_scan_block(xcols, list(range(nchunk)), dt_ref, dtb_ref, a_ref, carry_ref, emit, reverse=False, col0=0)


def _ssd_bwd_kernel(xconv_ref, dt_ref, dtb_ref, a_ref, yf_ref, zg_ref, sg_ref, x1_ref, c_ref, wo_ref, fg_ref, o_ref,
                    carry_ref, yb_ref):
    L = BLOCK
    nchunk = xconv_ref.shape[1] // L

    @pl.when(pl.program_id(1) == 0)
    def _():
        carry_ref[...] = jnp.zeros(carry_ref.shape, F32)

    x2 = x1_ref[0] + jnp.dot(c_ref[0], wo_ref[0:HALF, :], preferred_element_type=F32)

    def xcols(k, a, b, dtype):
        return xconv_ref[0, k * L:(k + 1) * L, a:b].astype(dtype)

    def emit(k, j, y):
        yb_ref[k * L:(k + 1) * L, j * LANES:(j + 1) * LANES] = y

    _ssd_scan_block(xcols, list(reversed(range(nchunk))), dt_ref, dtb_ref, a_ref, carry_ref, emit,
                    reverse=True, col0=SSM_HEADS)
    y = (yf_ref[0] + yb_ref[...]) * zg_ref[0].astype(F32)
    d = _rmsnorm(y, sg_ref[...]).astype(BF16)
    x2 = x2 + jnp.dot(d, wo_ref[HALF:D_MODEL, :], preferred_element_type=F32)
    o_ref[0] = _rmsnorm(x2, fg_ref[...])


def _ssd_scan_block(xcols, order, dt_ref, dtb_ref, a_ref, carry_ref, emit, *, reverse, col0):
    L = BLOCK
    nchunk = len(order)
    chunk = lambda m, k: m[k * L:(k + 1) * L]

    z = dt_ref[0] + dtb_ref[...]
    dt = jnp.maximum(z, 0.0) + jnp.log(1.0 + jnp.exp(-jnp.abs(z)))
    a = dt * a_ref[...]
    r_i = lax.broadcasted_iota(jnp.int32, (L, L), 0)
    c_i = lax.broadcasted_iota(jnp.int32, (L, L), 1)
    causal = (r_i <= c_i) if reverse else (r_i >= c_i)
    tri = jnp.where(causal, 1.0, 0.0).astype(BF16)
    pieces = _split3(a * LOG2E)
    acum = [sum(jnp.dot(tri, chunk(pc, k), preferred_element_type=F32) for pc in pieces) for k in range(nchunk)]
    acum_t = [m.T for m in acum]
    edge = 0 if reverse else L - 1
    a_end = [m[edge:edge + 1, :] for m in acum]

    lane = lax.broadcasted_iota(jnp.int32, (L, LANES), 1)
    low = lane < HEAD_DIM
    lane1 = lax.broadcasted_iota(jnp.int32, (1, LANES), 1)

    cbs, bts, cms = [], [], []
    for k in range(nchunk):
        bm = [xcols(k, SSM_INNER + g * SSM_STATE, SSM_INNER + (g + 1) * SSM_STATE, BF16) for g in range(2)]
        cm = [xcols(k, SSM_INNER + (2 + g) * SSM_STATE, SSM_INNER + (3 + g) * SSM_STATE, BF16) for g in range(2)]
        cbs.append([lax.dot_general(cm[g], bm[g], (((1,), (1,)), ((), ())), preferred_element_type=F32)
                    for g in range(2)])
        bts.append([bm[g].T for g in range(2)])
        cms.append(cm)
    for j in range(SSM_HEADS // 2):
        g = j // 2
        ca = col0 + 2 * j
        ydiag, states, decay_out, decay_in = [], [], [], []
        for k in range(nchunk):
            acb = [jnp.broadcast_to(acum[k][:, ca + hh:ca + hh + 1], (L, LANES)) for hh in range(2)]
            ac_pair = jnp.where(low, acb[0], acb[1])
            dtk = chunk(dt, k)
            dt_pair = jnp.where(low, dtk[:, ca:ca + 1], dtk[:, ca + 1:ca + 2])
            end_pair = jnp.where(lane1 < HEAD_DIM, a_end[k][:, ca:ca + 1], a_end[k][:, ca + 1:ca + 2])
            xdt = xcols(k, j * LANES, (j + 1) * LANES, F32) * dt_pair
            xdt_b = xdt.astype(BF16)
            zero = jnp.zeros_like(xdt_b)
            ms = []
            for hh in range(2):
                seg = acb[hh] - acum_t[k][ca + hh:ca + hh + 1, :]
                dec = jnp.exp2(jnp.where(causal, seg, NEG_BIG))
                ms.append((cbs[k][g] * dec).astype(BF16))
            lhs = jnp.concatenate(ms, axis=1)
            rhs = jnp.concatenate([jnp.where(low, xdt_b, zero), jnp.where(low, zero, xdt_b)], axis=0)
            ydiag.append(jnp.dot(lhs, rhs, preferred_element_type=F32))
            states.append(jnp.dot(bts[k][g], (xdt * jnp.exp2(end_pair - ac_pair)).astype(BF16),
                                  preferred_element_type=F32))
            decay_out.append(jnp.exp2(end_pair))
            decay_in.append(jnp.exp2(ac_pair))
        carry = carry_ref[j]
        for k in order:
            emit(k, j, ydiag[k] + jnp.dot(cms[k][g], carry.astype(BF16), preferred_element_type=F32) * decay_in[k])
            carry = carry * decay_out[k] + states[k]
        carry_ref[j] = carry


_SSD_CARRY = pltpu.VMEM((SSM_HEADS // 2, SSM_STATE, LANES), F32)
_SSD_ROWS = 8 * BLOCK


def _ssd_fwd(xbc, dt, cw, cb, dtb, a, dsk):
    bsz, seq, _ = xbc.shape
    vec = lambda wd: pl.BlockSpec((1, wd), lambda b, c: (0, 0))
    chunk = lambda wd: pl.BlockSpec((1, _SSD_ROWS, wd), lambda b, c: (b, c, 0))
    return pl.pallas_call(
        _ssd_fwd_kernel,
        grid=(bsz, seq // _SSD_ROWS),
        in_specs=_halo_specs(_SSD_ROWS, seq, SSM_CONV_DIM) + [
            chunk(LANES), pl.BlockSpec((SSM_CONV, SSM_CONV_DIM), lambda b, c: (0, 0)), vec(SSM_CONV_DIM),
            vec(LANES), vec(LANES), vec(SSM_INNER)],
        out_specs=[chunk(SSM_INNER), chunk(SSM_CONV_DIM)],
        out_shape=[jax.ShapeDtypeStruct((bsz, seq, SSM_INNER), F32),
                   jax.ShapeDtypeStruct((bsz, seq, SSM_CONV_DIM), BF16)],
        scratch_shapes=[_SSD_CARRY, pltpu.VMEM((_SSD_ROWS, SSM_CONV_DIM), F32)],
        compiler_params=_cparams(("parallel", "arbitrary")),
        name="ssd_fwd",
    )(xbc, xbc, xbc, dt, cw, cb, dtb, a, dsk)


def _ssd_bwd_final(xconv, dt, dtb, a, yf, zg, sg, x1, c, wo, fg):
    bsz, seq, _ = xconv.shape
    nc = seq // _SSD_ROWS
    vec = lambda wd: pl.BlockSpec((1, wd), lambda b, c: (0, 0))
    chunk = lambda wd: pl.BlockSpec((1, _SSD_ROWS, wd), lambda b, c: (b, nc - 1 - c, 0))
    return pl.pallas_call(
        _ssd_bwd_kernel,
        grid=(bsz, nc),
        in_specs=[chunk(SSM_CONV_DIM), chunk(LANES), vec(LANES), vec(LANES), chunk(SSM_INNER), chunk(SSM_INNER),
                  vec(SSM_INNER), chunk(D_MODEL), chunk(HALF), pl.BlockSpec((D_MODEL, D_MODEL), lambda b, c: (0, 0)),
                  vec(D_MODEL)],
        out_specs=chunk(D_MODEL),
        out_shape=jax.ShapeDtypeStruct((bsz, seq, D_MODEL), F32),
        scratch_shapes=[_SSD_CARRY, pltpu.VMEM((_SSD_ROWS, SSM_INNER), F32)],
        compiler_params=_cparams(("parallel", "arbitrary")),
        name="ssd_bwd_final",
    )(xconv, dt, dtb, a, yf, zg, sg, x1, c, wo, fg)


def _rope_tables(seq):
    inv = 1.0 / (ROPE_THETA ** (jnp.arange(0, HEAD_DIM, 2, dtype=F32) / HEAD_DIM))
    f = jnp.arange(seq, dtype=F32)[:, None] * inv[None, :]
    cos, sin = jnp.cos(f), jnp.sin(f)
    cos = jnp.concatenate([cos] * 4, axis=-1)
    sin = jnp.concatenate([sin] * 4, axis=-1)
    first_half = (jnp.arange(LANES) % HEAD_DIM) < HEAD_DIM // 2
    return cos, jnp.where(first_half, -sin, 0.0), jnp.where(first_half, 0.0, sin)


def _pair_perm():
    idx = []
    for j in range(SWA_HEADS // 2):
        idx += list(range(j * HEAD_DIM, (j + 1) * HEAD_DIM))
        idx += list(range((j + 4) * HEAD_DIM, (j + 5) * HEAD_DIM))
    return jnp.asarray(idx, jnp.int32)


def _pad_lanes(v, offset):
    return jnp.zeros((1, LANES), F32).at[0, offset:offset + v.shape[0]].set(v.astype(F32))


def _prepare(norm_g, w_in0, conv_w, conv_b, conv_ln_g, conv_ln_b, sink, w_out0, w_in1, lambda_q1, lambda_k1,
             lambda_q2, lambda_k2, diff_norm_g, ssm_conv_w, ssm_conv_b, dt_bias_f, dt_bias_b, a_log_f, a_log_b,
             d_skip, ssm_norm_g, w_out1, final_norm_g):
    perm = _pair_perm()
    w0 = w_in0[0].astype(BF16)
    (q0, q1), (g0, g1) = EVEN["q"], EVEN["b_gate"]
    w0 = jnp.concatenate([w0[:, :q0], w0[:, q0:q1][:, perm], w0[:, q1:g0], w0[:, g0:g1][:, perm]], axis=1)
    wo0 = w_out0[0].astype(BF16)
    wo0 = jnp.concatenate([wo0[:HALF], wo0[HALF:][perm]], axis=0)
    w1 = jnp.pad(w_in1[0].astype(BF16), ((0, 0), (0, IN_ODD_PAD - IN_ODD)))
    row = lambda v: v.astype(F32).reshape(1, -1)
    return dict(
        g0=row(norm_g[0]), g1=row(norm_g[1]), w0=w0, wo0=wo0, w1=w1,
        wo1=w_out1[0].astype(BF16), cb=row(conv_b[0]), lg=row(conv_ln_g[0]),
        cw=jnp.broadcast_to(conv_w[0].astype(F32)[:, None, :], (CONV_WIDTH, SUBLANES, CONV_CH)),
        lb=row(conv_ln_b[0]), sink=sink[0].astype(F32),
        lq1=row(lambda_q1[0]), lk1=row(lambda_k1[0]), lq2=row(lambda_q2[0]), lk2=row(lambda_k2[0]),
        dg=row(diff_norm_g[0]), scw=ssm_conv_w[0].astype(F32), scb=row(ssm_conv_b[0]),
        dtb_f=_pad_lanes(dt_bias_f[0], 0), dtb_b=_pad_lanes(dt_bias_b[0], SSM_HEADS),
        a_f=_pad_lanes(-jnp.exp(a_log_f[0].astype(F32)), 0), a_b=_pad_lanes(-jnp.exp(a_log_b[0].astype(F32)), SSM_HEADS),
        dsk=row(jnp.repeat(d_skip[0].astype(F32), HEAD_DIM)), sg=row(ssm_norm_g[0]), fg=row(final_norm_g))


def _trunk(x, p, tm=512):
    bsz, seq, _ = x.shape
    assert seq % tm == 0 and seq % _SSD_ROWS == 0
    t = bsz * seq
    lam_init = 0.8 - 0.6 * math.exp(-0.3 * 1)
    cos, sa, sb = _rope_tables(seq)
    x2 = x.reshape(t, D_MODEL)
    apre, ga, q, k, v, gb = _inproj0(x2, p["g0"], p["w0"], cos, sa, sb, seq, tm)
    r3 = lambda arr: arr.reshape(bsz, seq, arr.shape[-1])
    a = _convmod(r3(apre), r3(ga), p["cw"], p["cb"], p["lg"], p["lb"])
    o = _swa(p["sink"], r3(q), r3(k), r3(v), r3(gb))
    x1, q1, k1, v1, cg, zg, xbc, dt = _mid(x2, a.reshape(t, HALF), o.reshape(t, HALF), p["wo0"], p["g1"], p["w1"],
                                           cos, sa, sb, seq, tm)
    c = _diff_attn(p["lq1"], p["lk1"], p["lq2"], p["lk2"], p["dg"], r3(q1), r3(k1), r3(v1), r3(cg), lam_init)
    yf, xconv = _ssd_fwd(r3(xbc), r3(dt), p["scw"], p["scb"], p["dtb_f"], p["a_f"], p["dsk"])
    return _ssd_bwd_final(xconv, r3(dt), p["dtb_b"], p["a_b"], yf, r3(zg), p["sg"], r3(x1), c, p["wo1"], p["fg"])


def kernel(x_prompt, x_sample, norm_g, w_in0, conv_w, conv_b, conv_ln_g, conv_ln_b, sink, w_out0, w_in1, lambda_q1, lambda_k1, lambda_q2, lambda_k2, diff_norm_g, ssm_conv_w, ssm_conv_b, dt_bias_f, dt_bias_b, a_log_f, a_log_b, d_skip, ssm_norm_g, w_out1, final_norm_g):
    p = _prepare(norm_g, w_in0, conv_w, conv_b, conv_ln_g, conv_ln_b, sink, w_out0, w_in1, lambda_q1, lambda_k1,
                 lambda_q2, lambda_k2, diff_norm_g, ssm_conv_w, ssm_conv_b, dt_bias_f, dt_bias_b, a_log_f, a_log_b,
                 d_skip, ssm_norm_g, w_out1, final_norm_g)
    return (_trunk(x_prompt, p), _trunk(x_sample, p))
```

```python
import functools
import math

import jax
import jax.numpy as jnp
from jax import lax
from jax.experimental import pallas as pl
from jax.experimental.pallas import tpu as pltpu

F32 = jnp.float32
BF16 = jnp.bfloat16

D_MODEL = 1024
EPS = 1e-6
ROPE_THETA = 10000.0
HEAD_DIM = 64
LANES = 128
SUBLANES = 8
HALO = 16
CONV_CH = 512
CONV_WIDTH = 31
SWA_HEADS = 8
BLOCK = 128
DIFF_HEADS = 4
SSM_INNER = 512
SSM_HEADS = 8
SSM_STATE = 128
SSM_CONV = 5
SSM_CONV_DIM = 1024
HALF = D_MODEL // 2


def _col_ranges(**widths):
    out, start = {}, 0
    for name, width in widths.items():
        out[name] = (start, start + width)
        start += width
    return out


EVEN = _col_ranges(a_val=CONV_CH, a_glu=CONV_CH, a_gate=CONV_CH, q=HALF, k=LANES, v=LANES, b_gate=HALF)
ODD = _col_ranges(q=HALF, k=HALF, v=HALF, c_gate=HALF, z=SSM_INNER, xbc=SSM_CONV_DIM, dt=LANES)
IN_EVEN = EVEN["b_gate"][1]
IN_ODD = ODD["dt"][0] + 2 * SSM_HEADS
IN_ODD_PAD = ODD["dt"][1]
NEG_BIG = -1e30
LOG2E = math.log2(math.e)
VMEM_LIMIT = 56 * 1024 * 1024


def _cparams(sem):
    return pltpu.CompilerParams(dimension_semantics=sem, vmem_limit_bytes=VMEM_LIMIT)


def _sigmoid(x):
    return 1.0 / (1.0 + jnp.exp2(x * (-LOG2E)))


def _silu(x):
    return x * _sigmoid(x)


def _rmsnorm(x, g):
    return x * lax.rsqrt(jnp.mean(x * x, axis=-1, keepdims=True) + EPS) * g


def _rope_cols(x, cos, sa, sb):
    outs = []
    for j in range(x.shape[1] // LANES):
        xj = x[:, j * LANES:(j + 1) * LANES]
        outs.append(xj * cos + pltpu.roll(xj, LANES - 32, 1) * sa + pltpu.roll(xj, 32, 1) * sb)
    return outs[0] if len(outs) == 1 else jnp.concatenate(outs, axis=1)


def _inproj0_kernel(x_ref, g_ref, w_ref, cos_ref, sa_ref, sb_ref,
                    apre_ref, ga_ref, q_ref, k_ref, v_ref, gb_ref):
    parts = 2
    rows = x_ref.shape[0] // parts
    groups = [slice(part * rows, (part + 1) * rows) for part in range(parts)]
    hbs = [_rmsnorm(x_ref[r, :], g_ref[...]).astype(BF16) for r in groups]
    for r, hb in zip(groups, hbs):
        def mm(a, b):
            return jnp.dot(hb, w_ref[:, a:b], preferred_element_type=F32)

        cos, sa, sb = cos_ref[r, :], sa_ref[r, :], sb_ref[r, :]
        apre_ref[r, :] = (mm(*EVEN["a_val"]) * _sigmoid(mm(*EVEN["a_glu"]))).astype(BF16)
        ga_ref[r, :] = _silu(mm(*EVEN["a_gate"])).astype(BF16)
        gb_ref[r, :] = _silu(mm(*EVEN["b_gate"])).astype(BF16)
        q_ref[r, :] = (_rope_cols(mm(*EVEN["q"]), cos, sa, sb) * (HEAD_DIM ** -0.5 * LOG2E)).astype(BF16)
        kv = mm(EVEN["k"][0], EVEN["v"][1])
        k_ref[r, :] = _rope_cols(kv[:, 0:LANES], cos, sa, sb).astype(BF16)
        v_ref[r, :] = kv[:, LANES:2 * LANES].astype(BF16)


def _inproj0(x2, g, w, cos, sa, sb, seq, tm):
    t = x2.shape[0]
    ns = seq // tm
    row = lambda i: (i, 0)
    pos = lambda i: (i % ns, 0)
    full = lambda i: (0, 0)
    widths = (CONV_CH, CONV_CH, HALF, LANES, LANES, HALF)
    return pl.pallas_call(
        _inproj0_kernel,
        grid=(t // tm,),
        in_specs=[pl.BlockSpec((tm, D_MODEL), row), pl.BlockSpec((1, D_MODEL), full),
                  pl.BlockSpec((D_MODEL, IN_EVEN), full),
                  pl.BlockSpec((tm, LANES), pos), pl.BlockSpec((tm, LANES), pos), pl.BlockSpec((tm, LANES), pos)],
        out_specs=[pl.BlockSpec((tm, wd), row) for wd in widths],
        out_shape=[jax.ShapeDtypeStruct((t, wd), BF16) for wd in widths],
        compiler_params=_cparams(("parallel",)),
        name="inproj0",
    )(x2, g, w, cos, sa, sb)


def _convmod_kernel(xc_ref, xp_ref, xn_ref, ga_ref, cw_ref, cb_ref, lg_ref, lb_ref, o_ref, rol_ref, *, ts, rs):
    i = pl.program_id(1)
    last = pl.num_programs(1) - 1
    xp, xn = xp_ref[0], xn_ref[0]
    padded = jnp.concatenate([jnp.where(i > 0, xp, jnp.zeros_like(xp)), xc_ref[0],
                              jnp.where(i < last, xn, jnp.zeros_like(xn))], axis=0)
    rows = ts + 2 * HALO
    rol_ref[0] = padded.astype(F32)
    for rb in range(0, rows, BLOCK):
        nr = min(BLOCK, rows - rb)
        nk = min(BLOCK + HALO, rows - rb)
        r_i = lax.broadcasted_iota(jnp.int32, (nr, nk), 0)
        c_i = lax.broadcasted_iota(jnp.int32, (nr, nk), 1)
        window = padded[rb:rb + nk]
        for j in range(1, SUBLANES):
            shift = jnp.where(c_i == r_i + j, 1.0, 0.0).astype(BF16)
            rol_ref[j, rb:rb + nr] = jnp.dot(shift, window, preferred_element_type=F32)
    first = HALO - (CONV_WIDTH - 1) // 2

    for r0 in range(0, ts, rs):
        acc = jnp.broadcast_to(cb_ref[...], (rs, CONV_CH))
        for j in range(SUBLANES):
            part = None
            for w in range(CONV_WIDTH):
                a, jw = divmod(first + w, SUBLANES)
                if jw != j:
                    continue
                lo = r0 + a * SUBLANES
                term = rol_ref[j, lo:lo + rs, :] * jnp.concatenate([cw_ref[w]] * (rs // SUBLANES), axis=0)
                part = term if part is None else part + term
            acc = acc + part
        mu = jnp.mean(acc, axis=-1, keepdims=True)
        xc = acc - mu
        var = jnp.mean(xc * xc, axis=-1, keepdims=True)
        y = xc * lax.rsqrt(var + EPS) * lg_ref[...] + lb_ref[...]
        o_ref[0, r0:r0 + rs, :] = (_silu(y) * ga_ref[0, r0:r0 + rs, :].astype(F32)).astype(BF16)


def _halo_specs(ts, seq, width):
    hb = ts // HALO
    nh = seq // HALO
    return [pl.BlockSpec((1, ts, width), lambda b, i: (b, i, 0)),
            pl.BlockSpec((1, HALO, width), lambda b, i: (b, jnp.maximum(i * hb - 1, 0), 0)),
            pl.BlockSpec((1, HALO, width), lambda b, i: (b, jnp.minimum((i + 1) * hb, nh - 1), 0))]


def _convmod(apre, ga, cw, cb, lg, lb, ts=1024, rs=32):
    bsz, seq, _ = apre.shape
    vec = pl.BlockSpec((1, CONV_CH), lambda b, i: (0, 0))
    return pl.pallas_call(
        functools.partial(_convmod_kernel, ts=ts, rs=rs),
        grid=(bsz, seq // ts),
        in_specs=_halo_specs(ts, seq, CONV_CH) + [
            pl.BlockSpec((1, ts, CONV_CH), lambda b, i: (b, i, 0)),
            pl.BlockSpec((CONV_WIDTH, SUBLANES, CONV_CH), lambda b, i: (0, 0, 0)), vec, vec, vec],
        out_specs=pl.BlockSpec((1, ts, CONV_CH), lambda b, i: (b, i, 0)),
        out_shape=jax.ShapeDtypeStruct((bsz, seq, CONV_CH), BF16),
        scratch_shapes=[pltpu.VMEM((SUBLANES, ts + 2 * HALO, CONV_CH), F32)],
        compiler_params=_cparams(("parallel", "parallel")),
        name="convmod",
    )(apre, apre, apre, ga, cw, cb, lg, lb)


def _swa_kernel(sink_ref, q_ref, kp_ref, kc_ref, kn_ref, vp_ref, vc_ref, vn_ref, gb_ref, o_ref, *, nblk):
    n = pl.program_id(1)
    last = pl.num_programs(1) - 1
    kcat = jnp.concatenate([kp_ref[0], kc_ref[0], kn_ref[0]], axis=0)
    vcat = jnp.concatenate([vp_ref[0], vc_ref[0], vn_ref[0]], axis=0)
    vaug = jnp.concatenate([vcat, jnp.ones_like(vcat)], axis=1)
    rows = lax.broadcasted_iota(jnp.int32, (2 * BLOCK, 3 * BLOCK), 0)
    qi = jnp.where(rows >= BLOCK, rows - BLOCK, rows)
    ci = lax.broadcasted_iota(jnp.int32, (2 * BLOCK, 3 * BLOCK), 1)
    rel = ci - BLOCK - qi
    band = (rel >= -BLOCK) & (rel <= BLOCK)
    first_key = jnp.where(n > 0, 0, BLOCK)
    end_key = jnp.where(n < last, 3 * BLOCK, 2 * BLOCK)
    lane = lax.broadcasted_iota(jnp.int32, (BLOCK, LANES), 1)
    low = lane < HEAD_DIM
    row1 = lax.broadcasted_iota(jnp.int32, (2 * BLOCK, 1), 0)
    npair = SWA_HEADS // 2
    sinks = [jnp.where(row1 < BLOCK, sink_ref[j], sink_ref[j + npair]) * LOG2E for j in range(npair)]
    masks = []
    for b in range(nblk):
        mask = band
        if b == 0:
            mask = mask & (ci >= first_key)
        if b == nblk - 1:
            mask = mask & (ci < end_key)
        masks.append(mask)
    csl = [slice(j * LANES, (j + 1) * LANES) for j in range(npair)]
    for b in range(nblk):
        rsl = slice(b * BLOCK, (b + 1) * BLOCK)
        kwin = kcat[b * BLOCK:(b + 3) * BLOCK]
        vwin = vaug[b * BLOCK:(b + 3) * BLOCK]
        qqs = []
        for j in range(npair):
            qv = q_ref[0, rsl, csl[j]]
            zero = jnp.zeros_like(qv)
            qqs.append(jnp.concatenate([jnp.where(low, qv, zero), jnp.where(low, zero, qv)], axis=0))
        ss = [jnp.where(masks[b], lax.dot_general(qq, kwin, (((1,), (1,)), ((), ())), preferred_element_type=F32),
                        NEG_BIG) for qq in qqs]
        ms = [jnp.maximum(jnp.max(ss[j], axis=-1, keepdims=True), sinks[j]) for j in range(npair)]
        pvs = [jnp.dot(jnp.exp2(ss[j] - ms[j]).astype(BF16), vwin, preferred_element_type=F32) for j in range(npair)]
        for j in range(npair):
            pv = pvs[j][:, 0:LANES] / (pvs[j][:, LANES:2 * LANES] + jnp.exp2(sinks[j] - ms[j]))
            o = jnp.where(low, pv[:BLOCK], pv[BLOCK:])
            o_ref[0, rsl, csl[j]] = (o * gb_ref[0, rsl, csl[j]].astype(F32)).astype(BF16)


def _swa(sink, q, k, v, gb, nblk=8):
    bsz, seq, _ = q.shape
    rows = nblk * BLOCK
    nb = seq // BLOCK
    cur = lambda b, n: (b, n, 0)
    prv = lambda b, n: (b, jnp.maximum(n * nblk - 1, 0), 0)
    nxt = lambda b, n: (b, jnp.minimum((n + 1) * nblk, nb - 1), 0)
    edge = lambda im: pl.BlockSpec((1, BLOCK, LANES), im)
    mid = pl.BlockSpec((1, rows, LANES), cur)
    wide = pl.BlockSpec((1, rows, HALF), cur)
    return pl.pallas_call(
        functools.partial(_swa_kernel, nblk=nblk),
        grid=(bsz, seq // rows),
        in_specs=[pl.BlockSpec(memory_space=pltpu.SMEM), wide,
                  edge(prv), mid, edge(nxt), edge(prv), mid, edge(nxt), wide],
        out_specs=wide,
        out_shape=jax.ShapeDtypeStruct((bsz, seq, HALF), BF16),
        compiler_params=_cparams(("parallel", "parallel")),
        name="swa",
    )(sink, q, k, k, k, v, v, v, gb)


def _mid_kernel(x_ref, a_ref, o_ref, wo_ref, g_ref, w_ref, cos_ref, sa_ref, sb_ref,
                x1_ref, q_ref, k_ref, v_ref, cg_ref, zg_ref, xbc_ref, dt_ref):
    parts = 2
    rows = x_ref.shape[0] // parts
    groups = [slice(part * rows, (part + 1) * rows) for part in range(parts)]
    hbs = []
    for r in groups:
        x1 = (x_ref[r, :] + jnp.dot(a_ref[r, :], wo_ref[0:HALF, :], preferred_element_type=F32)
              + jnp.dot(o_ref[r, :], wo_ref[HALF:D_MODEL, :], preferred_element_type=F32))
        x1_ref[r, :] = x1
        hbs.append(_rmsnorm(x1, g_ref[...]).astype(BF16))
    ones = jnp.ones((rows, LANES), BF16)
    for r, hb in zip(groups, hbs):
        def mm(a, b):
            return jnp.dot(hb, w_ref[:, a:b], preferred_element_type=F32)

        cos, sa, sb = cos_ref[r, :], sa_ref[r, :], sb_ref[r, :]
        q_ref[r, :] = (_rope_cols(mm(*ODD["q"]), cos, sa, sb) * (HEAD_DIM ** -0.5 * LOG2E)).astype(BF16)
        k_ref[r, :] = _rope_cols(mm(*ODD["k"]), cos, sa, sb).astype(BF16)
        v = mm(*ODD["v"]).astype(BF16)
        for h in range(DIFF_HEADS):
            v_ref[r, 2 * h * LANES:(2 * h + 1) * LANES] = v[:, h * LANES:(h + 1) * LANES]
            v_ref[r, (2 * h + 1) * LANES:(2 * h + 2) * LANES] = ones
        dt_ref[r, :] = mm(*ODD["dt"])
        cg_ref[r, :] = _silu(mm(*ODD["c_gate"])).astype(BF16)
        zg_ref[r, :] = _silu(mm(*ODD["z"])).astype(BF16)
        xbc_ref[r, :] = mm(*ODD["xbc"]).astype(BF16)


def _mid(x2, a2, o2, wo, g, w, cos, sa, sb, seq, tm):
    t = x2.shape[0]
    ns = seq // tm
    row = lambda i: (i, 0)
    pos = lambda i: (i % ns, 0)
    full = lambda i: (0, 0)
    outs = ((D_MODEL, F32), (HALF, BF16), (HALF, BF16), (2 * HALF, BF16), (HALF, BF16), (SSM_INNER, BF16),
            (SSM_CONV_DIM, BF16), (LANES, F32))
    return pl.pallas_call(
        _mid_kernel,
        grid=(t // tm,),
        in_specs=[pl.BlockSpec((tm, D_MODEL), row), pl.BlockSpec((tm, HALF), row), pl.BlockSpec((tm, HALF), row),
                  pl.BlockSpec((D_MODEL, D_MODEL), full), pl.BlockSpec((1, D_MODEL), full),
                  pl.BlockSpec((D_MODEL, IN_ODD_PAD), full),
                  pl.BlockSpec((tm, LANES), pos), pl.BlockSpec((tm, LANES), pos), pl.BlockSpec((tm, LANES), pos)],
        out_specs=[pl.BlockSpec((tm, wd), row) for wd, _ in outs],
        out_shape=[jax.ShapeDtypeStruct((t, wd), dt) for wd, dt in outs],
        compiler_params=_cparams(("parallel",)),
        name="outproj0_inproj1",
    )(x2, a2, o2, wo, g, w, cos, sa, sb)


def _diff_kernel(lq1_ref, lk1_ref, lq2_ref, lk2_ref, g_ref, q_ref, qn_ref, k_ref, v_ref, cg_ref, o_ref,
                 s0_ref, s1_ref, mx0_ref, mx1_ref, m_ref, acc_ref, *, tq, kc, lam_init):
    nsub = q_ref.shape[1] // tq
    nch = k_ref.shape[1] // kc
    lane = lax.broadcasted_iota(jnp.int32, (tq, LANES), 1)

    def stack_maps(q):
        zero = jnp.zeros_like(q)
        return jnp.concatenate([jnp.where(lane < HEAD_DIM, q, zero), jnp.where(lane < HEAD_DIM, zero, q)], axis=0)

    slots = ((s0_ref, mx0_ref), (s1_ref, mx1_ref))

    def scores(qs, c, slot):
        s_ref, mx_ref = slots[slot]
        kch = k_ref[0, pl.ds(pl.multiple_of(c * kc, kc), kc), :]
        s = lax.dot_general(qs, kch, (((1,), (1,)), ((), ())), preferred_element_type=F32)
        s_ref[...] = s
        mx_ref[...] = jnp.broadcast_to(jnp.max(s, axis=-1, keepdims=True), mx_ref.shape)

    def update(c, slot, sub):
        s_ref, mx_ref = slots[slot]
        vch = v_ref[0, pl.ds(pl.multiple_of(c * kc, kc), kc), :]
        m_old = m_ref[sub]
        m_new = jnp.maximum(m_old, mx_ref[...])
        alpha = jnp.exp2(m_old - m_new)
        p = jnp.exp2(s_ref[...] - jnp.concatenate([m_new] * (kc // LANES), axis=1)).astype(BF16)
        acc_ref[sub] = acc_ref[sub] * jnp.concatenate([alpha, alpha], axis=1) + jnp.dot(
            p, vch, preferred_element_type=F32)
        m_ref[sub] = m_new

    @pl.when(pl.program_id(2) == 0)
    def _():
        scores(stack_maps(q_ref[0, 0:tq]), 0, 0)

    qqs = [stack_maps(q_ref[0, sub * tq:(sub + 1) * tq]) for sub in range(nsub)] + [stack_maps(qn_ref[0])]
    lam = (jnp.exp(jnp.sum(lq1_ref[...] * lk1_ref[...], axis=-1, keepdims=True))
           - jnp.exp(jnp.sum(lq2_ref[...] * lk2_ref[...], axis=-1, keepdims=True)) + lam_init)
    for sub in range(nsub):
        rows = slice(sub * tq, (sub + 1) * tq)
        m_ref[sub] = jnp.full(m_ref.shape[1:], NEG_BIG, F32)
        acc_ref[sub] = jnp.zeros(acc_ref.shape[1:], F32)
        for c in range(nch):
            if c + 1 < nch:
                scores(qqs[sub], c + 1, (c + 1) % 2)
            else:
                scores(qqs[sub + 1], 0, 0)
            update(c, c % 2, sub)
        o0 = acc_ref[sub, 0:tq, 0:LANES] / acc_ref[sub, 0:tq, LANES:2 * LANES]
        o1 = acc_ref[sub, tq:2 * tq, 0:LANES] / acc_ref[sub, tq:2 * tq, LANES:2 * LANES]
        o = o0 - lam * o1
        o = o * lax.rsqrt(jnp.mean(o * o, axis=-1, keepdims=True) + EPS) * g_ref[...] * (1.0 - lam_init)
        o_ref[0, rows, :] = (o * cg_ref[0, rows, :].astype(F32)).astype(BF16)


def _diff_attn(lq1, lk1, lq2, lk2, g, q, k, v, cg, lam_init, tq=512):
    bsz, seq, _ = q.shape
    kc = next(c for c in (1024, 512, 256) if seq % (2 * c) == 0)
    nsub = next(n for n in (16 * kc // seq, 2, 1) if n >= 1 and (seq // tq) % n == 0)
    nsteps = seq // (nsub * tq)
    small = pl.BlockSpec((1, HEAD_DIM), lambda b, h, i: (0, 0))
    tile = pl.BlockSpec((1, nsub * tq, LANES), lambda b, h, i: (b, i, h))
    next_tile = pl.BlockSpec((1, tq, LANES), lambda b, h, i: (b, jnp.minimum(i + 1, nsteps - 1) * nsub, h))
    return pl.pallas_call(
        functools.partial(_diff_kernel, tq=tq, kc=kc, lam_init=lam_init),
        grid=(bsz, DIFF_HEADS, nsteps),
        in_specs=[small, small, small, small, pl.BlockSpec((1, LANES), lambda b, h, i: (0, 0)), tile, next_tile,
                  pl.BlockSpec((1, seq, LANES), lambda b, h, i: (b, 0, h)),
                  pl.BlockSpec((1, seq, 2 * LANES), lambda b, h, i: (b, 0, h)), tile],
        out_specs=tile,
        out_shape=jax.ShapeDtypeStruct((bsz, seq, HALF), BF16),
        scratch_shapes=[pltpu.VMEM((2 * tq, kc), F32), pltpu.VMEM((2 * tq, kc), F32),
                        pltpu.VMEM((2 * tq, LANES), F32), pltpu.VMEM((2 * tq, LANES), F32),
                        pltpu.VMEM((nsub, 2 * tq, LANES), F32), pltpu.VMEM((nsub, 2 * tq, 2 * LANES), F32)],
        compiler_params=_cparams(("parallel", "parallel", "arbitrary")),
        name="diff_attn",
    )(lq1, lk1, lq2, lk2, g, q, q, k, v, cg)


def _split3(x):
    hi = x.astype(BF16)
    r = x - hi.astype(F32)
    mid = r.astype(BF16)
    lo = (r - mid.astype(F32)).astype(BF16)
    return hi, mid, lo


def _ssd_fwd_kernel(xc_ref, xp_ref, xn_ref, dt_ref, cw_ref, cb_ref, dtb_ref, a_ref, dsk_ref, y_ref, xconv_ref,
                    carry_ref, xf_ref):
    L = BLOCK
    nchunk = xc_ref.shape[1] // L
    c = pl.program_id(1)
    last = pl.num_programs(1) - 1

    @pl.when(c == 0)
    def _():
        carry_ref[...] = jnp.zeros(carry_ref.shape, F32)

    xp, xn = xp_ref[0], xn_ref[0]
    halo_p = jnp.where(c > 0, xp, jnp.zeros_like(xp))
    halo_n = jnp.where(c < last, xn, jnp.zeros_like(xn))
    r_i = lax.broadcasted_iota(jnp.int32, (L, L + 2 * HALO), 0)
    c_i = lax.broadcasted_iota(jnp.int32, (L, L + 2 * HALO), 1)
    first = HALO - (SSM_CONV - 1) // 2
    shifts = [None if first + w == HALO else jnp.where(c_i == r_i + first + w, 1.0, 0.0).astype(BF16)
              for w in range(SSM_CONV)]
    wd = 2 * LANES
    for nb in range(SSM_CONV_DIM // wd):
        cols = slice(nb * wd, (nb + 1) * wd)
        padded = jnp.concatenate([halo_p[:, cols], xc_ref[0, :, cols], halo_n[:, cols]], axis=0)
        for ci in range(nchunk):
            r0 = ci * L
            window = padded[r0:r0 + L + 2 * HALO]
            acc = jnp.broadcast_to(cb_ref[:, cols], (L, wd))
            for w in range(SSM_CONV):
                tap = (xc_ref[0, r0:r0 + L, cols].astype(F32) if shifts[w] is None
                       else jnp.dot(shifts[w], window, preferred_element_type=F32))
                acc = acc + tap * cw_ref[w:w + 1, cols]
            xbc = _silu(acc)
            xconv_ref[0, r0:r0 + L, cols] = xbc.astype(BF16)
            xf_ref[r0:r0 + L, cols] = xbc

    def xcols(k, a, b, dtype):
        return xf_ref[k * L:(k + 1) * L, a:b].astype(dtype)

    def emit(k, j, y):
        cols = slice(j * LANES, (j + 1) * LANES)
        y_ref[0, k * L:(k + 1) * L, cols] = y + xcols(k, j * LANES, (j + 1) * LANES, F32) * dsk_ref[:, cols]

    _ssd_scan_block(xcols, list(range(nchunk)), dt_ref, dtb_ref, a_ref, carry_ref, emit, reverse=False, col0=0)


def _ssd_bwd_kernel(xconv_ref, dt_ref, dtb_ref, a_ref, yf_ref, zg_ref, sg_ref, x1_ref, c_ref, wo_ref, fg_ref, o_ref,
                    carry_ref, yb_ref):
    L = BLOCK
    nchunk = xconv_ref.shape[1] // L

    @pl.when(pl.program_id(1) == 0)
    def _():
        carry_ref[...] = jnp.zeros(carry_ref.shape, F32)

    x2 = x1_ref[0] + jnp.dot(c_ref[0], wo_ref[0:HALF, :], preferred_element_type=F32)

    def xcols(k, a, b, dtype):
        return xconv_ref[0, k * L:(k + 1) * L, a:b].astype(dtype)

    def emit(k, j, y):
        yb_ref[k * L:(k + 1) * L, j * LANES:(j + 1) * LANES] = y

    _ssd_scan_block(xcols, list(reversed(range(nchunk))), dt_ref, dtb_ref, a_ref, carry_ref, emit,
                    reverse=True, col0=SSM_HEADS)
    y = (yf_ref[0] + yb_ref[...]) * zg_ref[0].astype(F32)
    d = _rmsnorm(y, sg_ref[...]).astype(BF16)
    x2 = x2 + jnp.dot(d, wo_ref[HALF:D_MODEL, :], preferred_element_type=F32)
    o_ref[0] = _rmsnorm(x2, fg_ref[...])


def _ssd_scan_block(xcols, order, dt_ref, dtb_ref, a_ref, carry_ref, emit, *, reverse, col0):
    L = BLOCK
    nchunk = len(order)
    chunk = lambda m, k: m[k * L:(k + 1) * L]

    z = dt_ref[0] + dtb_ref[...]
    dt = jnp.maximum(z, 0.0) + jnp.log(1.0 + jnp.exp(-jnp.abs(z)))
    a = dt * a_ref[...]
    r_i = lax.broadcasted_iota(jnp.int32, (L, L), 0)
    c_i = lax.broadcasted_iota(jnp.int32, (L, L), 1)
    causal = (r_i <= c_i) if reverse else (r_i >= c_i)
    tri = jnp.where(causal, 1.0, 0.0).astype(BF16)
    pieces = _split3(a * LOG2E)
    acum = [sum(jnp.dot(tri, chunk(pc, k), preferred_element_type=F32) for pc in pieces) for k in range(nchunk)]
    acum_t = [m.T for m in acum]
    edge = 0 if reverse else L - 1
    a_end = [m[edge:edge + 1, :] for m in acum]

    lane = lax.broadcasted_iota(jnp.int32, (L, LANES), 1)
    low = lane < HEAD_DIM
    lane1 = lax.broadcasted_iota(jnp.int32, (1, LANES), 1)

    cbs, bts, cms = [], [], []
    for k in range(nchunk):
        bm = [xcols(k, SSM_INNER + g * SSM_STATE, SSM_INNER + (g + 1) * SSM_STATE, BF16) for g in range(2)]
        cm = [xcols(k, SSM_INNER + (2 + g) * SSM_STATE, SSM_INNER + (3 + g) * SSM_STATE, BF16) for g in range(2)]
        cbs.append([lax.dot_general(cm[g], bm[g], (((1,), (1,)), ((), ())), preferred_element_type=F32)
                    for g in range(2)])
        bts.append([bm[g].T for g in range(2)])
        cms.append(cm)
    for j in range(SSM_HEADS // 2):
        g = j // 2
        ca = col0 + 2 * j
        ydiag, states, decay_out, decay_in = [], [], [], []
        for k in range(nchunk):
            acb = [jnp.broadcast_to(acum[k][:, ca + hh:ca + hh + 1], (L, LANES)) for hh in range(2)]
            ac_pair = jnp.where(low, acb[0], acb[1])
            dtk = chunk(dt, k)
            dt_pair = jnp.where(low, dtk[:, ca:ca + 1], dtk[:, ca + 1:ca + 2])
            end_pair = jnp.where(lane1 < HEAD_DIM, a_end[k][:, ca:ca + 1], a_end[k][:, ca + 1:ca + 2])
            xdt = xcols(k, j * LANES, (j + 1) * LANES, F32) * dt_pair
            xdt_b = xdt.astype(BF16)
            zero = jnp.zeros_like(xdt_b)
            ms = []
            for hh in range(2):
                seg = acb[hh] - acum_t[k][ca + hh:ca + hh + 1, :]
                dec = jnp.exp2(jnp.where(causal, seg, NEG_BIG))
                ms.append((cbs[k][g] * dec).astype(BF16))
            lhs = jnp.concatenate(ms, axis=1)
            rhs = jnp.concatenate([jnp.where(low, xdt_b, zero), jnp.where(low, zero, xdt_b)], axis=0)
            ydiag.append(jnp.dot(lhs, rhs, preferred_element_type=F32))
            states.append(jnp.dot(bts[k][g], (xdt * jnp.exp2(end_pair - ac_pair)).astype(BF16),
                                  preferred_element_type=F32))
            decay_out.append(jnp.exp2(end_pair))
            decay_in.append(jnp.exp2(ac_pair))
        carry = carry_ref[j]
        for k in order:
            emit(k, j, ydiag[k] + jnp.dot(cms[k][g], carry.astype(BF16), preferred_element_type=F32) * decay_in[k])
            carry = carry * decay_out[k] + states[k]
        carry_ref[j] = carry


_SSD_CARRY = pltpu.VMEM((SSM_HEADS // 2, SSM_STATE, LANES), F32)
_SSD_ROWS = 8 * BLOCK


def _ssd_fwd(xbc, dt, cw, cb, dtb, a, dsk):
    bsz, seq, _ = xbc.shape
    vec = lambda wd: pl.BlockSpec((1, wd), lambda b, c: (0, 0))
    chunk = lambda wd: pl.BlockSpec((1, _SSD_ROWS, wd), lambda b, c: (b, c, 0))
    return pl.pallas_call(
        _ssd_fwd_kernel,
        grid=(bsz, seq // _SSD_ROWS),
        in_specs=_halo_specs(_SSD_ROWS, seq, SSM_CONV_DIM) + [
            chunk(LANES), pl.BlockSpec((SSM_CONV, SSM_CONV_DIM), lambda b, c: (0, 0)), vec(SSM_CONV_DIM),
            vec(LANES), vec(LANES), vec(SSM_INNER)],
        out_specs=[chunk(SSM_INNER), chunk(SSM_CONV_DIM)],
        out_shape=[jax.ShapeDtypeStruct((bsz, seq, SSM_INNER), F32),
                   jax.ShapeDtypeStruct((bsz, seq, SSM_CONV_DIM), BF16)],
        scratch_shapes=[_SSD_CARRY, pltpu.VMEM((_SSD_ROWS, SSM_CONV_DIM), F32)],
        compiler_params=_cparams(("parallel", "arbitrary")),
        name="ssd_fwd",
    )(xbc, xbc, xbc, dt, cw, cb, dtb, a, dsk)


def _ssd_bwd_final(xconv, dt, dtb, a, yf, zg, sg, x1, c, wo, fg):
    bsz, seq, _ = xconv.shape
    nc = seq // _SSD_ROWS
    vec = lambda wd: pl.BlockSpec((1, wd), lambda b, c: (0, 0))
    chunk = lambda wd: pl.BlockSpec((1, _SSD_ROWS, wd), lambda b, c: (b, nc - 1 - c, 0))
    return pl.pallas_call(
        _ssd_bwd_kernel,
        grid=(bsz, nc),
        in_specs=[chunk(SSM_CONV_DIM), chunk(LANES), vec(LANES), vec(LANES), chunk(SSM_INNER), chunk(SSM_INNER),
                  vec(SSM_INNER), chunk(D_MODEL), chunk(HALF), pl.BlockSpec((D_MODEL, D_MODEL), lambda b, c: (0, 0)),
                  vec(D_MODEL)],
        out_specs=chunk(D_MODEL),
        out_shape=jax.ShapeDtypeStruct((bsz, seq, D_MODEL), F32),
        scratch_shapes=[_SSD_CARRY, pltpu.VMEM((_SSD_ROWS, SSM_INNER), F32)],
        compiler_params=_cparams(("parallel", "arbitrary")),
        name="ssd_bwd_final",
    )(xconv, dt, dtb, a, yf, zg, sg, x1, c, wo, fg)


def _rope_tables(seq):
    inv = 1.0 / (ROPE_THETA ** (jnp.arange(0, HEAD_DIM, 2, dtype=F32) / HEAD_DIM))
    f = jnp.arange(seq, dtype=F32)[:, None] * inv[None, :]
    cos, sin = jnp.cos(f), jnp.sin(f)
    cos = jnp.concatenate([cos] * 4, axis=-1)
    sin = jnp.concatenate([sin] * 4, axis=-1)
    first_half = (jnp.arange(LANES) % HEAD_DIM) < HEAD_DIM // 2
    return cos, jnp.where(first_half, -sin, 0.0), jnp.where(first_half, 0.0, sin)


def _pair_perm():
    idx = []
    for j in range(SWA_HEADS // 2):
        idx += list(range(j * HEAD_DIM, (j + 1) * HEAD_DIM))
        idx += list(range((j + 4) * HEAD_DIM, (j + 5) * HEAD_DIM))
    return jnp.asarray(idx, jnp.int32)


def _pad_lanes(v, offset):
    return jnp.zeros((1, LANES), F32).at[0, offset:offset + v.shape[0]].set(v.astype(F32))


def _prepare(norm_g, w_in0, conv_w, conv_b, conv_ln_g, conv_ln_b, sink, w_out0, w_in1, lambda_q1, lambda_k1,
             lambda_q2, lambda_k2, diff_norm_g, ssm_conv_w, ssm_conv_b, dt_bias_f, dt_bias_b, a_log_f, a_log_b,
             d_skip, ssm_norm_g, w_out1, final_norm_g):
    perm = _pair_perm()
    w0 = w_in0[0].astype(BF16)
    (q0, q1), (g0, g1) = EVEN["q"], EVEN["b_gate"]
    w0 = jnp.concatenate([w0[:, :q0], w0[:, q0:q1][:, perm], w0[:, q1:g0], w0[:, g0:g1][:, perm]], axis=1)
    wo0 = w_out0[0].astype(BF16)
    wo0 = jnp.concatenate([wo0[:HALF], wo0[HALF:][perm]], axis=0)
    w1 = jnp.pad(w_in1[0].astype(BF16), ((0, 0), (0, IN_ODD_PAD - IN_ODD)))
    row = lambda v: v.astype(F32).reshape(1, -1)
    return dict(
        g0=row(norm_g[0]), g1=row(norm_g[1]), w0=w0, wo0=wo0, w1=w1,
        wo1=w_out1[0].astype(BF16), cb=row(conv_b[0]), lg=row(conv_ln_g[0]),
        cw=jnp.broadcast_to(conv_w[0].astype(F32)[:, None, :], (CONV_WIDTH, SUBLANES, CONV_CH)),
        lb=row(conv_ln_b[0]), sink=sink[0].astype(F32),
        lq1=row(lambda_q1[0]), lk1=row(lambda_k1[0]), lq2=row(lambda_q2[0]), lk2=row(lambda_k2[0]),
        dg=row(diff_norm_g[0]), scw=ssm_conv_w[0].astype(F32), scb=row(ssm_conv_b[0]),
        dtb_f=_pad_lanes(dt_bias_f[0], 0), dtb_b=_pad_lanes(dt_bias_b[0], SSM_HEADS),
        a_f=_pad_lanes(-jnp.exp(a_log_f[0].astype(F32)), 0), a_b=_pad_lanes(-jnp.exp(a_log_b[0].astype(F32)), SSM_HEADS),
        dsk=row(jnp.repeat(d_skip[0].astype(F32), HEAD_DIM)), sg=row(ssm_norm_g[0]), fg=row(final_norm_g))


def _trunk(x, p, tm=512):
    bsz, seq, _ = x.shape
    assert seq % tm == 0 and seq % _SSD_ROWS == 0
    t = bsz * seq
    lam_init = 0.8 - 0.6 * math.exp(-0.3 * 1)
    cos, sa, sb = _rope_tables(seq)
    x2 = x.reshape(t, D_MODEL)
    apre, ga, q, k, v, gb = _inproj0(x2, p["g0"], p["w0"], cos, sa, sb, seq, tm)
    r3 = lambda arr: arr.reshape(bsz, seq, arr.shape[-1])
    a = _convmod(r3(apre), r3(ga), p["cw"], p["cb"], p["lg"], p["lb"])
    o = _swa(p["sink"], r3(q), r3(k), r3(v), r3(gb))
    x1, q1, k1, v1, cg, zg, xbc, dt = _mid(x2, a.reshape(t, HALF), o.reshape(t, HALF), p["wo0"], p["g1"], p["w1"],
                                           cos, sa, sb, seq, tm)
    c = _diff_attn(p["lq1"], p["lk1"], p["lq2"], p["lk2"], p["dg"], r3(q1), r3(k1), r3(v1), r3(cg), lam_init)
    yf, xconv = _ssd_fwd(r3(xbc), r3(dt), p["scw"], p["scb"], p["dtb_f"], p["a_f"], p["dsk"])
    return _ssd_bwd_final(xconv, r3(dt), p["dtb_b"], p["a_b"], yf, r3(zg), p["sg"], r3(x1), c, p["wo1"], p["fg"])


def kernel(x_prompt, x_sample, norm_g, w_in0, conv_w, conv_b, conv_ln_g, conv_ln_b, sink, w_out0, w_in1, lambda_q1, lambda_k1, lambda_q2, lambda_k2, diff_norm_g, ssm_conv_w, ssm_conv_b, dt_bias_f, dt_bias_b, a_log_f, a_log_b, d_skip, ssm_norm_g, w_out1, final_norm_g):
    p = _prepare(norm_g, w_in0, conv_w, conv_b, conv_ln_g, conv_ln_b, sink, w_out0, w_in1, lambda_q1, lambda_k1,
                 lambda_q2, lambda_k2, diff_norm_g, ssm_conv_w, ssm_conv_b, dt_bias_f, dt_bias_b, a_log_f, a_log_b,
                 d_skip, ssm_norm_g, w_out1, final_norm_g)
    return (_trunk(x_prompt, p), _trunk(x_sample, p))
```
